```python
import math
import jax, jax.numpy as jnp
from jax import lax
import numpy as np

D_MODEL = 1024
BATCH = 8
SEQ = 4096
DEPTH = 4

CHUNK = 64
N_META = 16
N_HEADS = 8
N_KV_HEADS = 2
HEAD_DIM = 128
KV_GROUP = N_HEADS // N_KV_HEADS
ATTN_WIDTH = N_HEADS * HEAD_DIM
IDX_HEADS = 8
IDX_DIM = 64
TOPK_MAX = 256
Q_BLOCK = 64
ROPE_THETA = 10000.0
RNN_WIDTH = D_MODEL
RNN_BLOCKS = 8
RNN_BLOCK_DIM = RNN_WIDTH // RNN_BLOCKS
RNN_CONV = 4
LRU_C = 8.0
CONV_WIDTH = D_MODEL
CONV_KERNEL = 31
D_FF = -(-8 * D_MODEL // (3 * 256)) * 256
N_BRANCH = 3
NORM_EPS = 1e-6

IN_SPLITS = (
    N_HEADS * HEAD_DIM,
    N_KV_HEADS * HEAD_DIM,
    N_KV_HEADS * HEAD_DIM,
    IDX_HEADS * IDX_DIM,
    IDX_DIM,
    IDX_HEADS,
    RNN_WIDTH,
    RNN_WIDTH,
    2 * CONV_WIDTH,
    N_BRANCH * D_MODEL,
)
IN_WIDTH = sum(IN_SPLITS)

kernel_name = "hybrid_dsa_rglru_conformer_streaming"


def _split_cols(a):
    pts, acc = [], 0
    for w in IN_SPLITS[:-1]:
        acc += w
        pts.append(acc)
    return jnp.split(a, pts, axis=-1)


def rms_norm(x, g):
    xf = x.astype(jnp.float32)
    y = xf * lax.rsqrt(jnp.mean(xf * xf, axis=-1, keepdims=True) + NORM_EPS)
    return (y * g.astype(jnp.float32)).astype(x.dtype)


def layer_norm(x, g, b):
    xf = x.astype(jnp.float32)
    mu = jnp.mean(xf, axis=-1, keepdims=True)
    xc = xf - mu
    y = xc * lax.rsqrt(jnp.mean(xc * xc, axis=-1, keepdims=True) + NORM_EPS)
    return (y * g.astype(jnp.float32) + b.astype(jnp.float32)).astype(x.dtype)


def rope_tables(n, dim):
    inv = ROPE_THETA ** (-jnp.arange(0, dim, 2, dtype=jnp.float32) / dim)
    ang = jnp.arange(n, dtype=jnp.float32)[:, None] * inv[None, :]
    return jnp.cos(ang), jnp.sin(ang)


def rotary(x, cos, sin):
    xf = x.astype(jnp.float32)
    x1, x2 = jnp.split(xf, 2, axis=-1)
    c = cos[None, :, None, :]
    s = sin[None, :, None, :]
    return jnp.concatenate([x1 * c - x2 * s, x2 * c + x1 * s], axis=-1).astype(x.dtype)


def chunk_ids(n_valid, n_total):
    p = jnp.arange(n_total, dtype=jnp.int32)
    cid = jnp.where(p < N_META, 0, 1 + (p - N_META) // CHUNK)
    return jnp.where(p < n_valid, cid, jnp.iinfo(jnp.int32).max).astype(jnp.int32)


def causal_depthwise_conv(x, w, b):
    k = w.shape[0]
    y = lax.conv_general_dilated(
        x, w[:, None, :].astype(x.dtype), window_strides=(1,), padding=[(k - 1, 0)],
        dimension_numbers=("NWC", "WIO", "NWC"), feature_group_count=x.shape[-1])
    return y + b.astype(y.dtype)


def dsa_attention(q, k, v, qi, ki, wi):
    B, T = q.shape[0], q.shape[1]
    n_blocks = -(-T // Q_BLOCK)
    Tp = n_blocks * Q_BLOCK
    pad = Tp - T

    def pad_t(a):
        return jnp.pad(a, [(0, 0), (0, pad)] + [(0, 0)] * (a.ndim - 2))

    q, k, v, qi, ki, wi = (pad_t(a) for a in (q, k, v, qi, ki, wi))
    cid = chunk_ids(T, Tp)
    topk = min(TOPK_MAX, SEQ // 4)
    scale = HEAD_DIM ** -0.5

    def to_blocks(a):
        return a.reshape(B, n_blocks, Q_BLOCK, *a.shape[2:]).swapaxes(0, 1)

    def block(args):
        qb, qib, wib, cidb = args
        logits = jnp.einsum("bqhd,bsd->bqhs", qib, ki)
        iscore = jnp.einsum("bqhs,bqh->bqs", jax.nn.relu(logits), wib).astype(jnp.float32)
        admiss = cid[None, :] <= cidb[:, None]
        iscore = jnp.where(admiss[None], iscore, -jnp.inf)
        _, sel = lax.top_k(iscore, topk)
        valid = cid[sel] <= cidb[None, :, None]
        gather = jax.vmap(lambda tb, ib: tb[ib])
        ksel = gather(k, sel)
        vsel = gather(v, sel)
        qg = qb.reshape(B, Q_BLOCK, N_KV_HEADS, KV_GROUP, HEAD_DIM)
        s = jnp.einsum("bqngd,bqknd->bngqk", qg, ksel).astype(jnp.float32) * scale
        s = jnp.where(valid[:, None, None], s, -jnp.inf)
        p = jax.nn.softmax(s, axis=-1).astype(vsel.dtype)
        o = jnp.einsum("bngqk,bqknd->bqngd", p, vsel)
        return o.reshape(B, Q_BLOCK, N_HEADS, HEAD_DIM)

    out = lax.map(block, (to_blocks(q), to_blocks(qi), to_blocks(wi), cid.reshape(n_blocks, Q_BLOCK)))
    return out.swapaxes(0, 1).reshape(B, Tp, N_HEADS * HEAD_DIM)[:, :T]


def rg_lru_branch(xr, yg, conv_w, conv_b, wa, ba, wx, bx, lam):
    u = causal_depthwise_conv(xr, conv_w, conv_b)
    B, T, R = u.shape
    ub = u.reshape(B, T, RNN_BLOCKS, RNN_BLOCK_DIM)
    r = jax.nn.sigmoid(jnp.einsum("btnd,nde->btne", ub, wa).reshape(B, T, R).astype(jnp.float32)
                       + ba.astype(jnp.float32))
    i = jax.nn.sigmoid(jnp.einsum("btnd,nde->btne", ub, wx).reshape(B, T, R).astype(jnp.float32)
                       + bx.astype(jnp.float32))
    log_a = -LRU_C * r * jax.nn.softplus(-lam.astype(jnp.float32))
    a = jnp.exp(log_a)
    b_in = jnp.sqrt(-jnp.expm1(2.0 * log_a)) * (i * u.astype(jnp.float32))

    def step(h, ab):
        a_t, b_t = ab
        h = a_t * h + b_t
        return h, h

    _, hs = lax.scan(step, jnp.zeros((B, R), jnp.float32), (a.swapaxes(0, 1), b_in.swapaxes(0, 1)))
    h = hs.swapaxes(0, 1).astype(xr.dtype)
    return h * jax.nn.gelu(yg)


def conformer_conv_branch(c, dw_w, dw_b, ln_g, ln_b):
    a, g = jnp.split(c, 2, axis=-1)
    u = a * jax.nn.sigmoid(g)
    u = causal_depthwise_conv(u, dw_w, dw_b)
    u = layer_norm(u, ln_g, ln_b)
    return jax.nn.silu(u)


def setup_inputs(seed: int = 0) -> dict:
    key = jax.random.key(seed)
    ks = iter(jax.random.split(key, 32))
    f32 = jnp.float32

    def nrm(shape, scale):
        return jax.random.normal(next(ks), shape, f32) * scale

    def gain(shape):
        return 1.0 + 0.01 * jax.random.normal(next(ks), shape, f32)

    a0 = jax.random.uniform(next(ks), (DEPTH, RNN_WIDTH), f32, minval=0.9, maxval=0.999)
    base = a0 ** (1.0 / LRU_C)
    rnn_lambda = jnp.log(base) - jnp.log1p(-base)
    return {
        "x": nrm((BATCH, SEQ, D_MODEL), 1.0),
        "meta": nrm((N_META, D_MODEL), 1.0),
        "mix_norm_g": gain((DEPTH, D_MODEL)),
        "w_in": nrm((DEPTH, D_MODEL, IN_WIDTH), D_MODEL ** -0.5),
        "q_norm_g": gain((DEPTH, HEAD_DIM)),
        "k_norm_g": gain((DEPTH, HEAD_DIM)),
        "rnn_conv_w": nrm((DEPTH, RNN_CONV, RNN_WIDTH), RNN_CONV ** -0.5),
        "rnn_conv_b": nrm((DEPTH, RNN_WIDTH), 0.01),
        "rnn_wa": nrm((DEPTH, RNN_BLOCKS, RNN_BLOCK_DIM, RNN_BLOCK_DIM), RNN_BLOCK_DIM ** -0.5),
        "rnn_ba": nrm((DEPTH, RNN_WIDTH), 0.01),
        "rnn_wx": nrm((DEPTH, RNN_BLOCKS, RNN_BLOCK_DIM, RNN_BLOCK_DIM), RNN_BLOCK_DIM ** -0.5),
        "rnn_bx": nrm((DEPTH, RNN_WIDTH), 0.01),
        "rnn_lambda": rnn_lambda,
        "conv_dw_w": nrm((DEPTH, CONV_KERNEL, CONV_WIDTH), CONV_KERNEL ** -0.5),
        "conv_dw_b": nrm((DEPTH, CONV_WIDTH), 0.01),
        "conv_ln_g": gain((DEPTH, CONV_WIDTH)),
        "conv_ln_b": nrm((DEPTH, CONV_WIDTH), 0.01),
        "w_o_attn": nrm((DEPTH, ATTN_WIDTH, D_MODEL), ATTN_WIDTH ** -0.5),
        "w_o_rnn": nrm((DEPTH, RNN_WIDTH, D_MODEL), RNN_WIDTH ** -0.5),
        "w_o_conv": nrm((DEPTH, CONV_WIDTH, D_MODEL), CONV_WIDTH ** -0.5),
        "w_out": nrm((DEPTH, D_MODEL, D_MODEL), D_MODEL ** -0.5),
        "ffn_norm_g": gain((DEPTH, D_MODEL)),
        "w_ffn_gate": nrm((DEPTH, D_MODEL, D_FF), D_MODEL ** -0.5),
        "w_ffn_up": nrm((DEPTH, D_MODEL, D_FF), D_MODEL ** -0.5),
        "w_ffn_down": nrm((DEPTH, D_FF, D_MODEL), D_FF ** -0.5),
    }


def reference(x, meta, mix_norm_g, w_in, q_norm_g, k_norm_g, rnn_conv_w, rnn_conv_b, rnn_wa, rnn_ba,
              rnn_wx, rnn_bx, rnn_lambda, conv_dw_w, conv_dw_b, conv_ln_g, conv_ln_b, w_o_attn, w_o_rnn,
              w_o_conv, w_out, ffn_norm_g, w_ffn_gate, w_ffn_up, w_ffn_down):
    B = x.shape[0]
    h = jnp.concatenate([jnp.broadcast_to(meta[None].astype(x.dtype), (B, N_META, D_MODEL)), x], axis=1)
    T = h.shape[1]
    cos_a, sin_a = rope_tables(T, HEAD_DIM)
    cos_i, sin_i = rope_tables(T, IDX_DIM)
    idx_scale = (IDX_HEADS ** -0.5) * (IDX_DIM ** -0.5)

    for l in range(DEPTH):
        n = rms_norm(h, mix_norm_g[l])
        proj = jnp.einsum("btd,de->bte", n, w_in[l])
        q, k, v, qi, ki, wi, xr, yg, cv, gt = _split_cols(proj)

        q = rotary(rms_norm(q.reshape(B, T, N_HEADS, HEAD_DIM), q_norm_g[l]), cos_a, sin_a)
        k = rotary(rms_norm(k.reshape(B, T, N_KV_HEADS, HEAD_DIM), k_norm_g[l]), cos_a, sin_a)
        v = v.reshape(B, T, N_KV_HEADS, HEAD_DIM)
        qi = rotary(qi.reshape(B, T, IDX_HEADS, IDX_DIM), cos_i, sin_i)
        ki = rotary(ki.reshape(B, T, 1, IDX_DIM), cos_i, sin_i)[:, :, 0]
        attn = dsa_attention(q, k, v, qi, ki, wi * idx_scale)

        rnn = rg_lru_branch(xr, yg, rnn_conv_w[l], rnn_conv_b[l], rnn_wa[l], rnn_ba[l],
                            rnn_wx[l], rnn_bx[l], rnn_lambda[l])

        cnv = conformer_conv_branch(cv, conv_dw_w[l], conv_dw_b[l], conv_ln_g[l], conv_ln_b[l])

        g_attn, g_rnn, g_conv = jnp.split(jax.nn.sigmoid(gt), N_BRANCH, axis=-1)
        merged = (g_attn * jnp.einsum("bte,ed->btd", attn, w_o_attn[l])
                  + g_rnn * jnp.einsum("bte,ed->btd", rnn, w_o_rnn[l])
                  + g_conv * jnp.einsum("bte,ed->btd", cnv, w_o_conv[l]))
        h = h + jnp.einsum("btd,de->bte", merged, w_out[l])

        f = rms_norm(h, ffn_norm_g[l])
        a = jax.nn.silu(jnp.einsum("btd,df->btf", f, w_ffn_gate[l])) * jnp.einsum("btd,df->btf", f, w_ffn_up[l])
        h = h + jnp.einsum("btf,fd->btd", a, w_ffn_down[l])

    return h[:, N_META:]
```

```python
import functools

import jax
import jax.numpy as jnp
import numpy as np
from jax import lax
from jax.experimental import pallas as pl
from jax.experimental.pallas import tpu as pltpu

D_MODEL = 1024
CHUNK = 64
N_META = 16
N_HEADS = 8
N_KV_HEADS = 2
HEAD_DIM = 128
KV_GROUP = N_HEADS // N_KV_HEADS
IDX_HEADS = 8
IDX_DIM = 64
TOPK_MAX = 256
ROPE_THETA = 10000.0
RNN_BLOCKS = 8
RNN_BLOCK_DIM = D_MODEL // RNN_BLOCKS
RNN_CONV = 4
LRU_C = 8.0
CONV_KERNEL = 31
NORM_EPS = 1e-6

LANES = 128
SUBLANES = 8
TQ = 128
TK_IDX = 512
TK_ATT = 256
KEY_PAD = 512
VMEM_LIMIT = 56 * 1024 * 1024

INT_MIN = -(2 ** 31)
KEY_MASKED = -2139095041
NEG_BIG = -1e30
POS_BIG = 2 ** 30

F32 = jnp.float32
BF16 = jnp.bfloat16


def _rms(x, g):
    return x * lax.rsqrt(jnp.mean(x * x, axis=-1, keepdims=True) + NORM_EPS) * g


def _expm1(y):
    u = jnp.exp(y)
    near = jnp.where(u == 1.0, y, (u - 1.0) * y / jnp.log(jnp.where(u == 1.0, 2.0, u)))
    return jnp.where(y > -0.5, near, u - 1.0)


def _const_spec(shape):
    nd = len(shape)
    return pl.BlockSpec(shape, lambda *_: (0,) * nd, pipeline_mode=pl.Buffered(1))


def _pick_tile(n, candidates):
    for c in candidates:
        if n % c == 0:
            return c
    raise ValueError(f"no tile for {n}")


def _attn_proj_kernel(x_ref, g_ref, w_ref, qg_ref, kg_ref, cosa_ref, sina_ref, cosi_ref, sini_ref,
                      q_ref, k_ref, v_ref, qi_ref, ki_ref, wi_ref, *, idx_scale):
    n = _rms(x_ref[...], g_ref[...]).astype(BF16)
    cosa, sina = cosa_ref[...], sina_ref[...]
    cosi, sini = cosi_ref[...], sini_ref[...]
    lane = lax.broadcasted_iota(jnp.int32, cosi.shape, 1)
    low_half = (lane % IDX_DIM) < (IDX_DIM // 2)

    def rot_attn(a, g):
        a = _rms(a, g)
        return a * cosa + pltpu.roll(a, HEAD_DIM // 2, axis=1) * sina

    def rot_idx(a):
        partner = jnp.where(low_half, pltpu.roll(a, LANES - IDX_DIM // 2, axis=1),
                            pltpu.roll(a, IDX_DIM // 2, axis=1))
        return a * cosi + partner * sini

    q_w = N_HEADS * HEAD_DIM
    kv_w = N_KV_HEADS * HEAD_DIM
    qi_w = IDX_HEADS * IDX_DIM
    o_k, o_v, o_qi, o_ki = q_w, q_w + kv_w, q_w + 2 * kv_w, q_w + 2 * kv_w + qi_w

    pq = jnp.dot(n, w_ref[:, 0:q_w], preferred_element_type=F32)
    for h in range(N_HEADS):
        sl = slice(h * HEAD_DIM, (h + 1) * HEAD_DIM)
        q_ref[:, sl] = rot_attn(pq[:, sl], qg_ref[...]).astype(BF16)
    pk = jnp.dot(n, w_ref[:, o_k:o_qi], preferred_element_type=F32)
    for h in range(N_KV_HEADS):
        sl = slice(h * HEAD_DIM, (h + 1) * HEAD_DIM)
        k_ref[:, sl] = rot_attn(pk[:, sl], kg_ref[...]).astype(BF16)
    v_ref[...] = pk[:, kv_w:].astype(BF16)
    pi = jnp.dot(n, w_ref[:, o_qi:], preferred_element_type=F32)
    for c in range(qi_w // LANES):
        sl = slice(c * LANES, (c + 1) * LANES)
        qi_ref[:, sl] = rot_idx(pi[:, sl]).astype(BF16)
    last = pi[:, qi_w:]
    ki_ref[...] = jnp.where(lane < IDX_DIM, rot_idx(last), 0.0).astype(BF16)
    wi_ref[...] = last * idx_scale


def _attn_proj(h2, g, w, qg, kg, tabs, tm, tiles_per_seq):
    rows = h2.shape[0]
    wcols = w.shape[1]
    row = lambda width: pl.BlockSpec((tm, width), lambda i: (i, 0))
    tab = pl.BlockSpec((tm, LANES), lambda i: (i % tiles_per_seq, 0))
    idx_scale = (IDX_HEADS ** -0.5) * (IDX_DIM ** -0.5)
    return pl.pallas_call(
        functools.partial(_attn_proj_kernel, idx_scale=idx_scale),
        grid=(rows // tm,),
        in_specs=[row(D_MODEL), _const_spec((1, D_MODEL)), _const_spec((D_MODEL, wcols)),
                  _const_spec((1, HEAD_DIM)), _const_spec((1, HEAD_DIM)), tab, tab, tab, tab],
        out_specs=[row(N_HEADS * HEAD_DIM), row(N_KV_HEADS * HEAD_DIM), row(N_KV_HEADS * HEAD_DIM),
                   row(IDX_HEADS * IDX_DIM), row(LANES), row(LANES)],
        out_shape=[jax.ShapeDtypeStruct((rows, N_HEADS * HEAD_DIM), BF16),
                   jax.ShapeDtypeStruct((rows, N_KV_HEADS * HEAD_DIM), BF16),
                   jax.ShapeDtypeStruct((rows, N_KV_HEADS * HEAD_DIM), BF16),
                   jax.ShapeDtypeStruct((rows, IDX_HEADS * IDX_DIM), BF16),
                   jax.ShapeDtypeStruct((rows, LANES), BF16),
                   jax.ShapeDtypeStruct((rows, LANES), F32)],
        compiler_params=pltpu.CompilerParams(dimension_semantics=("arbitrary",),
                                             vmem_limit_bytes=VMEM_LIMIT),
        name="attn_proj",
    )(h2, g, w, qg, kg, *tabs)


def _dsa_kernel(qit_ref, wit_ref, ki_ref, qt_ref, k_ref, vt_ref, out_ref, key_ref, bias_ref,
                *, n_valid, topk):
    tq = TQ
    t0 = pl.program_id(1) * tq
    tcol = t0 + lax.broadcasted_iota(jnp.int32, (1, tq), 1)
    kb = jnp.where(tcol < N_META, N_META,
                   N_META + CHUNK * (1 + jnp.right_shift(tcol - N_META, 6)))
    kb = jnp.where(tcol < n_valid, jnp.minimum(kb, n_valid), n_valid)
    kext = jnp.minimum(n_valid, N_META + CHUNK * (1 + (t0 + tq - 1 - N_META) // CHUNK))
    n_idx = (kext + TK_IDX - 1) // TK_IDX
    n_att = (kext + TK_ATT - 1) // TK_ATT

    w = wit_ref[0, 0]

    def score_tile(i, carry):
        r0 = pl.multiple_of(i * TK_IDX, TK_IDX)
        kt = ki_ref[0, pl.ds(r0, TK_IDX), :]
        acc = jnp.zeros((TK_IDX, tq), F32)
        for hp in range(IDX_HEADS // 2):
            lg = jnp.dot(kt, qit_ref[0, 0, :, hp * 2 * tq:(hp + 1) * 2 * tq],
                         preferred_element_type=F32)
            for hh in range(2):
                h = 2 * hp + hh
                acc = acc + jnp.maximum(lg[:, hh * tq:(hh + 1) * tq], 0.0) * w[h:h + 1, :]
        acc = jnp.where(acc == 0.0, 0.0, acc)
        pos = r0 + lax.broadcasted_iota(jnp.int32, (TK_IDX, tq), 0)
        bits = pltpu.bitcast(acc, jnp.int32)
        key = jnp.where(bits >= 0, bits, bits ^ 0x7FFFFFFF)
        key_ref[pl.ds(r0, TK_IDX), :] = jnp.where(pos < kb, key, KEY_MASKED)
        return carry

    lax.fori_loop(0, n_idx, score_tile, 0)

    def count(pred):
        def body(i, acc):
            r0 = pl.multiple_of(i * TK_IDX, TK_IDX)
            blk = key_ref[pl.ds(r0, TK_IDX), :]
            pos = r0 + lax.broadcasted_iota(jnp.int32, (TK_IDX, tq), 0)
            ind = jnp.where(pred(blk, pos), 1, 0).astype(jnp.int32)
            return acc + ind.reshape(TK_IDX // SUBLANES, SUBLANES, tq).sum(axis=0)
        acc = lax.fori_loop(0, n_idx, body, jnp.zeros((SUBLANES, tq), jnp.int32))
        return acc.sum(axis=0, keepdims=True)

    def value_step(it, lo):
        cand = lo + jnp.left_shift(jnp.int32(1), 31 - it)
        c = count(lambda blk, pos: blk >= cand)
        return jnp.where(c >= topk, cand, lo)

    tau = lax.fori_loop(0, 32, value_step, jnp.full((1, tq), INT_MIN, jnp.int32))

    c_gt = count(lambda blk, pos: blk > tau)
    c_ge = count(lambda blk, pos: blk >= tau)
    rank = topk - c_gt

    def tie_positions():
        def pos_step(it, lo):
            cand = lo + jnp.left_shift(jnp.int32(1), 14 - it)
            c = count(lambda blk, pos: (blk == tau) & (pos < cand))
            return jnp.where(c <= rank - 1, cand, lo)
        return lax.fori_loop(0, 15, pos_step, jnp.zeros((1, tq), jnp.int32))

    cut = jnp.max(c_ge - topk) > 0
    p_last = lax.cond(cut, tie_positions, lambda: jnp.full((1, tq), POS_BIG, jnp.int32))

    def bias_tile(i, carry):
        r0 = pl.multiple_of(i * TK_IDX, TK_IDX)
        blk = key_ref[pl.ds(r0, TK_IDX), :]
        pos = r0 + lax.broadcasted_iota(jnp.int32, (TK_IDX, tq), 0)
        keep = ((blk > tau) | ((blk == tau) & (pos <= p_last))) & (pos < kb)
        bias_ref[pl.ds(r0, TK_IDX), :] = jnp.where(keep, 0.0, NEG_BIG)
        return carry

    lax.fori_loop(0, n_idx, bias_tile, 0)

    scale = HEAD_DIM ** -0.5
    gw = KV_GROUP * tq
    for n in range(N_KV_HEADS):
        qn = qt_ref[0, 0, n]

        def att_tile(i, carry, n=n, qn=qn):
            m, l, acc = carry
            r0 = pl.multiple_of(i * TK_ATT, TK_ATT)
            s = jnp.dot(k_ref[0, n, i], qn, preferred_element_type=F32) * scale
            bt = bias_ref[pl.ds(r0, TK_ATT), :]
            s = s + jnp.concatenate([bt] * KV_GROUP, axis=1)
            m_new = jnp.maximum(m, jnp.max(s, axis=0, keepdims=True))
            alpha = jnp.exp(m - m_new)
            p = jnp.exp(s - m_new)
            l = alpha * l + jnp.sum(p, axis=0, keepdims=True)
            acc = alpha * acc + jnp.dot(vt_ref[0, n, i], p.astype(BF16), preferred_element_type=F32)
            return m_new, l, acc

        m0 = jnp.full((1, gw), NEG_BIG, F32)
        l0 = jnp.zeros((1, gw), F32)
        a0 = jnp.zeros((HEAD_DIM, gw), F32)
        _, l, acc = lax.fori_loop(0, n_att, att_tile, (m0, l0, a0))
        out_ref[0, 0, n] = (acc / l).astype(BF16)


def _dsa_attention(qit, wit, ki, qt, k5, vt5, n_valid, topk):
    bsz, nblk = qit.shape[0], qit.shape[1]
    kpad = ki.shape[1]
    n_att = kpad // TK_ATT
    gw = KV_GROUP * TQ
    return pl.pallas_call(
        functools.partial(_dsa_kernel, n_valid=n_valid, topk=topk),
        grid=(bsz, nblk),
        in_specs=[
            pl.BlockSpec((1, 1, LANES, IDX_HEADS * TQ), lambda b, j: (b, j, 0, 0)),
            pl.BlockSpec((1, 1, IDX_HEADS, TQ), lambda b, j: (b, j, 0, 0)),
            pl.BlockSpec((1, kpad, LANES), lambda b, j: (b, 0, 0)),
            pl.BlockSpec((1, 1, N_KV_HEADS, HEAD_DIM, gw), lambda b, j: (b, j, 0, 0, 0)),
            pl.BlockSpec((1, N_KV_HEADS, n_att, TK_ATT, HEAD_DIM), lambda b, j: (b, 0, 0, 0, 0)),
            pl.BlockSpec((1, N_KV_HEADS, n_att, HEAD_DIM, TK_ATT), lambda b, j: (b, 0, 0, 0, 0)),
        ],
        out_specs=pl.BlockSpec((1, 1, N_KV_HEADS, HEAD_DIM, gw), lambda b, j: (b, j, 0, 0, 0)),
        out_shape=jax.ShapeDtypeStruct((bsz, nblk, N_KV_HEADS, HEAD_DIM, gw), BF16),
        scratch_shapes=[pltpu.VMEM((kpad, TQ), jnp.int32), pltpu.VMEM((kpad, TQ), F32)],
        compiler_params=pltpu.CompilerParams(dimension_semantics=("arbitrary", "arbitrary"),
                                             vmem_limit_bytes=VMEM_LIMIT),
        name="dsa_attention",
    )(qit, wit, ki, qt, k5, vt5)


def _rglru_kernel(h_ref, g_ref, w_ref, cw_ref, cb_ref, wa_ref, ba_ref, wx_ref, bx_ref, lam_ref,
                  out_ref, xext_ref, a_ref, b_ref, state_ref):
    tm = h_ref.shape[0]
    hist = SUBLANES

    @pl.when(pl.program_id(1) == 0)
    def _():
        xext_ref[0:hist, :] = jnp.zeros((hist, D_MODEL), F32)
        state_ref[...] = jnp.zeros_like(state_ref)

    n = _rms(h_ref[...], g_ref[...]).astype(BF16)
    xext_ref[hist:hist + tm, :] = jnp.dot(n, w_ref[:, 0:D_MODEL], preferred_element_type=F32)
    u = cb_ref[...] + cw_ref[0:1, :] * xext_ref[hist - RNN_CONV + 1:hist - RNN_CONV + 1 + tm, :]
    for kk in range(1, RNN_CONV):
        o = hist - RNN_CONV + 1 + kk
        u = u + cw_ref[kk:kk + 1, :] * xext_ref[o:o + tm, :]
    xext_ref[0:hist, :] = xext_ref[tm:tm + hist, :]

    ub = u.astype(BF16)
    rs, is_ = [], []
    for blk in range(RNN_BLOCKS):
        sl = slice(blk * RNN_BLOCK_DIM, (blk + 1) * RNN_BLOCK_DIM)
        rs.append(jnp.dot(ub[:, sl], wa_ref[blk], preferred_element_type=F32))
        is_.append(jnp.dot(ub[:, sl], wx_ref[blk], preferred_element_type=F32))
    r = jax.nn.sigmoid(jnp.concatenate(rs, axis=1) + ba_ref[...])
    ig = jax.nn.sigmoid(jnp.concatenate(is_, axis=1) + bx_ref[...])
    nl = -lam_ref[...]
    softplus = jnp.maximum(nl, 0.0) + jnp.log1p(jnp.exp(-jnp.abs(nl)))
    log_a = -LRU_C * r * softplus
    a_ref[...] = jnp.exp(log_a)
    b_ref[...] = jnp.sqrt(-_expm1(2.0 * log_a)) * (ig * u)

    def step(t, hcur):
        hcur = a_ref[pl.ds(t, 1), :] * hcur + b_ref[pl.ds(t, 1), :]
        b_ref[pl.ds(t, 1), :] = hcur
        return hcur

    state_ref[...] = lax.fori_loop(0, tm, step, state_ref[...], unroll=8)
    yg = jnp.dot(n, w_ref[:, D_MODEL:], preferred_element_type=F32)
    out_ref[...] = (b_ref[...] * jax.nn.gelu(yg)).astype(BF16)


def _rglru(h3, g, w, cw, cb, wa, ba, wx, bx, lam, tm):
    bsz, tp, _ = h3.shape
    row = pl.BlockSpec((None, tm, D_MODEL), lambda b, t: (b, t, 0))
    vec = _const_spec((1, D_MODEL))
    blkw = _const_spec((RNN_BLOCKS, RNN_BLOCK_DIM, RNN_BLOCK_DIM))
    return pl.pallas_call(
        _rglru_kernel,
        grid=(bsz, tp // tm),
        in_specs=[row, vec, _const_spec((D_MODEL, 2 * D_MODEL)), _const_spec((RNN_CONV, D_MODEL)), vec,
                  blkw, vec, blkw, vec, vec],
        out_specs=row,
        out_shape=jax.ShapeDtypeStruct((bsz, tp, D_MODEL), BF16),
        scratch_shapes=[pltpu.VMEM((tm + SUBLANES, D_MODEL), F32), pltpu.VMEM((tm, D_MODEL), F32),
                        pltpu.VMEM((tm, D_MODEL), F32), pltpu.VMEM((1, D_MODEL), F32)],
        compiler_params=pltpu.CompilerParams(dimension_semantics=("arbitrary", "arbitrary"),
                                             vmem_limit_bytes=VMEM_LIMIT),
        name="rglru",
    )(h3, g, w, cw, cb, wa, ba, wx, bx, lam)


CONV_HIST = 32
CONV_ROWS = 128


def _conformer_kernel(h_ref, g_ref, w_ref, dw_ref, db_ref, lg_ref, lb_ref, out_ref, xext_ref, y_ref):
    tm = h_ref.shape[0]

    @pl.when(pl.program_id(1) == 0)
    def _():
        xext_ref[0:CONV_HIST, :] = jnp.zeros((CONV_HIST, D_MODEL), F32)

    n = _rms(h_ref[...], g_ref[...]).astype(BF16)
    a = jnp.dot(n, w_ref[:, 0:D_MODEL], preferred_element_type=F32)
    gate = jnp.dot(n, w_ref[:, D_MODEL:], preferred_element_type=F32)
    xext_ref[CONV_HIST:CONV_HIST + tm, :] = a * jax.nn.sigmoid(gate)

    first = CONV_HIST - CONV_KERNEL + 1
    for r0 in range(0, tm, CONV_ROWS):
        for c0 in range(0, D_MODEL, LANES):
            cs = slice(c0, c0 + LANES)
            acc = jnp.broadcast_to(db_ref[:, cs], (CONV_ROWS, LANES))
            for kk in range(CONV_KERNEL):
                o = r0 + first + kk
                acc = acc + dw_ref[kk:kk + 1, cs] * xext_ref[o:o + CONV_ROWS, cs]
            y_ref[r0:r0 + CONV_ROWS, cs] = acc
    xext_ref[0:CONV_HIST, :] = xext_ref[tm:tm + CONV_HIST, :]

    y = y_ref[...]
    mu = jnp.mean(y, axis=-1, keepdims=True)
    yc = y - mu
    z = yc * lax.rsqrt(jnp.mean(yc * yc, axis=-1, keepdims=True) + NORM_EPS) * lg_ref[...] + lb_ref[...]
    out_ref[...] = jax.nn.silu(z).astype(BF16)


def _conformer(h3, g, w, dw, db, lg, lb, tm):
    bsz, tp, _ = h3.shape
    row = pl.BlockSpec((None, tm, D_MODEL), lambda b, t: (b, t, 0))
    vec = _const_spec((1, D_MODEL))
    return pl.pallas_call(
        _conformer_kernel,
        grid=(bsz, tp // tm),
        in_specs=[row, vec, _const_spec((D_MODEL, 2 * D_MODEL)), _const_spec((CONV_KERNEL, D_MODEL)),
                  vec, vec, vec],
        out_specs=row,
        out_shape=jax.ShapeDtypeStruct((bsz, tp, D_MODEL), BF16),
        scratch_shapes=[pltpu.VMEM((tm + CONV_HIST, D_MODEL), F32), pltpu.VMEM((tm, D_MODEL), F32)],
        compiler_params=pltpu.CompilerParams(dimension_semantics=("arbitrary", "arbitrary"),
                                             vmem_limit_bytes=VMEM_LIMIT),
        name="conformer",
    )(h3, g, w, dw, db, lg, lb)


def _merge_kernel(h_ref, at_ref, rn_ref, cv_ref, g_ref, wg_ref, wa_ref, wr_ref, wc_ref, wo_ref, out_ref):
    h = h_ref[...]
    n = _rms(h, g_ref[...]).astype(BF16)
    merged = None
    for i, (src, wref) in enumerate(((at_ref, wa_ref), (rn_ref, wr_ref), (cv_ref, wc_ref))):
        gate = jax.nn.sigmoid(jnp.dot(n, wg_ref[:, i * D_MODEL:(i + 1) * D_MODEL],
                                      preferred_element_type=F32))
        term = gate * jnp.dot(src[...], wref[...], preferred_element_type=F32)
        merged = term if merged is None else merged + term
    out_ref[...] = h + jnp.dot(merged.astype(BF16), wo_ref[...], preferred_element_type=F32)


def _merge(h2, attn, rnn, cnv, g, wg, wa, wr, wc, wo, tm):
    rows = h2.shape[0]
    row = pl.BlockSpec((tm, D_MODEL), lambda i: (i, 0))
    sq = _const_spec((D_MODEL, D_MODEL))
    return pl.pallas_call(
        _merge_kernel,
        grid=(rows // tm,),
        in_specs=[row, row, row, row, _const_spec((1, D_MODEL)), _const_spec((D_MODEL, 3 * D_MODEL)),
                  sq, sq, sq, sq],
        out_specs=row,
        out_shape=jax.ShapeDtypeStruct((rows, D_MODEL), F32),
        compiler_params=pltpu.CompilerParams(dimension_semantics=("arbitrary",),
                                             vmem_limit_bytes=VMEM_LIMIT),
        name="merge",
    )(h2, attn, rnn, cnv, g, wg, wa, wr, wc, wo)


def _ffn_kernel(h_ref, g_ref, wg_ref, wu_ref, wd_ref, out_ref):
    h = h_ref[...]
    f = _rms(h, g_ref[...]).astype(BF16)
    gate = jnp.dot(f, wg_ref[...], preferred_element_type=F32)
    up = jnp.dot(f, wu_ref[...], preferred_element_type=F32)
    act = (jax.nn.silu(gate) * up).astype(BF16)
    out_ref[...] = h + jnp.dot(act, wd_ref[...], preferred_element_type=F32)


def _ffn(h2, g, wg, wu, wd, tm):
    rows = h2.shape[0]
    dff = wg.shape[1]
    row = pl.BlockSpec((tm, D_MODEL), lambda i: (i, 0))
    return pl.pallas_call(
        _ffn_kernel,
        grid=(rows // tm,),
        in_specs=[row, _const_spec((1, D_MODEL)), _const_spec((D_MODEL, dff)), _const_spec((D_MODEL, dff)),
                  _const_spec((dff, D_MODEL))],
        out_specs=row,
        out_shape=jax.ShapeDtypeStruct((rows, D_MODEL), F32),
        compiler_params=pltpu.CompilerParams(dimension_semantics=("arbitrary",),
                                             vmem_limit_bytes=VMEM_LIMIT),
        name="ffn",
    )(h2, g, wg, wu, wd)


def _rope_tables(n, dim):
    inv = ROPE_THETA ** (-jnp.arange(0, dim, 2, dtype=F32) / dim)
    ang = jnp.arange(n, dtype=F32)[:, None] * inv[None, :]
    return jnp.cos(ang), jnp.sin(ang)


def kernel(x, meta, mix_norm_g, w_in, q_norm_g, k_norm_g, rnn_conv_w, rnn_conv_b, rnn_wa, rnn_ba, rnn_wx, rnn_bx, rnn_lambda, conv_dw_w, conv_dw_b, conv_ln_g, conv_ln_b, w_o_attn, w_o_rnn, w_o_conv, w_out, ffn_norm_g, w_ffn_gate, w_ffn_up, w_ffn_down):
    bsz, seq, _ = x.shape
    depth = w_in.shape[0]
    t_valid = seq + N_META
    tp = -(-t_valid // TQ) * TQ
    nblk = tp // TQ
    kpad = -(-tp // KEY_PAD) * KEY_PAD
    topk = min(TOPK_MAX, seq // 4)
    rows = bsz * tp
    tm_seq = _pick_tile(tp, (384, 256, 128))
    tm_flat = _pick_tile(rows, (512, 384, 256, 128))

    q_w, kv_w, qi_w = N_HEADS * HEAD_DIM, N_KV_HEADS * HEAD_DIM, IDX_HEADS * IDX_DIM
    o = np.cumsum([0, q_w, kv_w, kv_w, qi_w, IDX_DIM, IDX_HEADS, D_MODEL, D_MODEL, 2 * D_MODEL, 3 * D_MODEL])
    pad_w = LANES - IDX_DIM - IDX_HEADS
    w_attn = jnp.concatenate([w_in[:, :, o[0]:o[6]], jnp.zeros((depth, D_MODEL, pad_w), w_in.dtype)],
                             axis=2).astype(BF16)
    w_rnn = w_in[:, :, o[6]:o[8]].astype(BF16)
    w_cnv = w_in[:, :, o[8]:o[9]].astype(BF16)
    w_gate = w_in[:, :, o[9]:o[10]].astype(BF16)
    wa_b, wx_b = rnn_wa.astype(BF16), rnn_wx.astype(BF16)
    w_oa, w_or, w_oc, w_ot = (a.astype(BF16) for a in (w_o_attn, w_o_rnn, w_o_conv, w_out))
    w_fg, w_fu, w_fd = (a.astype(BF16) for a in (w_ffn_gate, w_ffn_up, w_ffn_down))

    cos_a, sin_a = _rope_tables(tp, HEAD_DIM)
    cos_i, sin_i = _rope_tables(tp, IDX_DIM)
    tabs = (jnp.concatenate([cos_a, cos_a], axis=1), jnp.concatenate([-sin_a, sin_a], axis=1),
            jnp.concatenate([cos_i] * 4, axis=1), jnp.concatenate([-sin_i, sin_i] * 2, axis=1))

    h = jnp.concatenate([jnp.broadcast_to(meta[None].astype(x.dtype), (bsz, N_META, D_MODEL)), x,
                         jnp.zeros((bsz, tp - t_valid, D_MODEL), x.dtype)], axis=1)
    vec = lambda a: a.reshape(1, -1)
    n_att = kpad // TK_ATT

    for l in range(depth):
        h2 = h.reshape(rows, D_MODEL)
        q, k, v, qi, ki, wi = _attn_proj(h2, vec(mix_norm_g[l]), w_attn[l], vec(q_norm_g[l]),
                                         vec(k_norm_g[l]), tabs, tm_seq, tp // tm_seq)
        qt = q.reshape(bsz, nblk, TQ, N_KV_HEADS, KV_GROUP, HEAD_DIM).transpose(0, 1, 3, 5, 4, 2)
        qt = qt.reshape(bsz, nblk, N_KV_HEADS, HEAD_DIM, KV_GROUP * TQ)
        qit = qi.reshape(bsz, nblk, TQ, IDX_HEADS, IDX_DIM).transpose(0, 1, 4, 3, 2)
        qit = qit.reshape(bsz, nblk, IDX_DIM, IDX_HEADS * TQ)
        qit = jnp.pad(qit, ((0, 0), (0, 0), (0, LANES - IDX_DIM), (0, 0)))
        wit = wi.reshape(bsz, nblk, TQ, LANES)[..., IDX_DIM:IDX_DIM + IDX_HEADS].transpose(0, 1, 3, 2)
        kip = jnp.pad(ki.reshape(bsz, tp, LANES), ((0, 0), (0, kpad - tp), (0, 0)))
        k4 = jnp.pad(k.reshape(bsz, tp, N_KV_HEADS, HEAD_DIM), ((0, 0), (0, kpad - tp), (0, 0), (0, 0)))
        k5 = k4.transpose(0, 2, 1, 3).reshape(bsz, N_KV_HEADS, n_att, TK_ATT, HEAD_DIM)
        v4 = jnp.pad(v.reshape(bsz, tp, N_KV_HEADS, HEAD_DIM), ((0, 0), (0, kpad - tp), (0, 0), (0, 0)))
        vt5 = v4.reshape(bsz, n_att, TK_ATT, N_KV_HEADS, HEAD_DIM).transpose(0, 3, 1, 4, 2)
        attn_t = _dsa_attention(qit, wit, kip, qt, k5, vt5, t_valid, topk)
        attn = attn_t.reshape(bsz, nblk, N_KV_HEADS, HEAD_DIM, KV_GROUP, TQ).transpose(0, 1, 5, 2, 4, 3)
        attn = attn.reshape(rows, N_HEADS * HEAD_DIM)

        rnn = _rglru(h, vec(mix_norm_g[l]), w_rnn[l], rnn_conv_w[l], vec(rnn_conv_b[l]), wa_b[l],
                     vec(rnn_ba[l]), wx_b[l], vec(rnn_bx[l]), vec(rnn_lambda[l]), tm_seq)
        cnv = _conformer(h, vec(mix_norm_g[l]), w_cnv[l], conv_dw_w[l], vec(conv_dw_b[l]),
                         vec(conv_ln_g[l]), vec(conv_ln_b[l]), tm_seq)
        h2 = _merge(h2, attn, rnn.reshape(rows, D_MODEL), cnv.reshape(rows, D_MODEL), vec(mix_norm_g[l]),
                    w_gate[l], w_oa[l], w_or[l], w_oc[l], w_ot[l], tm_flat)
        h2 = _ffn(h2, vec(ffn_norm_g[l]), w_fg[l], w_fu[l], w_fd[l], tm_flat)
        h = h2.reshape(bsz, tp, D_MODEL)

    return h[:, N_META:t_valid]
```

```python
import functools

import jax
import jax.numpy as jnp
import numpy as np
from jax import lax
from jax.experimental import pallas as pl
from jax.experimental.pallas import tpu as pltpu

D_MODEL = 1024
CHUNK = 64
N_META = 16
N_HEADS = 8
N_KV_HEADS = 2
HEAD_DIM = 128
KV_GROUP = N_HEADS // N_KV_HEADS
IDX_HEADS = 8
IDX_DIM = 64
TOPK_MAX = 256
ROPE_THETA = 10000.0
RNN_BLOCKS = 8
RNN_BLOCK_DIM = D_MODEL // RNN_BLOCKS
RNN_CONV = 4
LRU_C = 8.0
CONV_KERNEL = 31
NORM_EPS = 1e-6

LANES = 128
SUBLANES = 8
TQ = 128
VMEM_LIMIT = 56 * 1024 * 1024
LOG2E = 1.4426950408889634

INT_MIN = -(2 ** 31)
KEY_MASKED = -2139095041
NEG_BIG = -1e30
POS_BIG = 2 ** 30

F32 = jnp.float32
BF16 = jnp.bfloat16


def _rms(x, g):
    return x * lax.rsqrt(jnp.mean(x * x, axis=-1, keepdims=True) + NORM_EPS) * g


def _expm1(y):
    u = jnp.exp(y)
    near = jnp.where(u == 1.0, y, (u - 1.0) * y / jnp.log(jnp.where(u == 1.0, 2.0, u)))
    return jnp.where(y > -0.5, near, u - 1.0)


def _const_spec(shape):
    nd = len(shape)
    return pl.BlockSpec(shape, lambda *_: (0,) * nd, pipeline_mode=pl.Buffered(1))


def _pick_tile(n, candidates):
    for c in candidates:
        if n % c == 0:
            return c
    raise ValueError(f"no tile for {n}")


def _attn_proj_kernel(x_ref, g_ref, w_ref, qg_ref, kg_ref, cosa_ref, sina_ref, cosi_ref, sini_ref,
                      q_ref, k_ref, vt_ref, qi_ref, ki_ref, wi_ref, *, idx_scale):
    n = _rms(x_ref[...], g_ref[...]).astype(BF16)
    cosa, sina = cosa_ref[...], sina_ref[...]
    cosi, sini = cosi_ref[...], sini_ref[...]
    lane = lax.broadcasted_iota(jnp.int32, cosi.shape, 1)
    low_half = (lane % IDX_DIM) < (IDX_DIM // 2)

    def rot_attn(a, g):
        a = _rms(a, g)
        return a * cosa + pltpu.roll(a, HEAD_DIM // 2, axis=1) * sina

    def rot_idx(a):
        partner = jnp.where(low_half, pltpu.roll(a, LANES - IDX_DIM // 2, axis=1),
                            pltpu.roll(a, IDX_DIM // 2, axis=1))
        return a * cosi + partner * sini

    q_w = N_HEADS * HEAD_DIM
    kv_w = N_KV_HEADS * HEAD_DIM
    qi_w = IDX_HEADS * IDX_DIM
    o_k, o_v, o_qi, o_ki = q_w, q_w + kv_w, q_w + 2 * kv_w, q_w + 2 * kv_w + qi_w

    q_scale = (HEAD_DIM ** -0.5) * LOG2E
    pq = jnp.dot(n, w_ref[:, 0:q_w], preferred_element_type=F32)
    for h in range(N_HEADS):
        sl = slice(h * HEAD_DIM, (h + 1) * HEAD_DIM)
        q_ref[:, sl] = (rot_attn(pq[:, sl], qg_ref[...]) * q_scale).astype(BF16)
    pk = jnp.dot(n, w_ref[:, o_k:o_qi], preferred_element_type=F32)
    for h in range(N_KV_HEADS):
        sl = slice(h * HEAD_DIM, (h + 1) * HEAD_DIM)
        k_ref[:, sl] = rot_attn(pk[:, sl], kg_ref[...]).astype(BF16)
        vt_ref[sl, :] = pk[:, kv_w + h * HEAD_DIM:kv_w + (h + 1) * HEAD_DIM].T.astype(BF16)
    pi = jnp.dot(n, w_ref[:, o_qi:], preferred_element_type=F32)
    for c in range(qi_w // LANES):
        sl = slice(c * LANES, (c + 1) * LANES)
        qi_ref[:, sl] = rot_idx(pi[:, sl]).astype(BF16)
    last = pi[:, qi_w:]
    ki = rot_idx(last)
    ki_ref[...] = jnp.where(lane < IDX_DIM, ki, pltpu.roll(ki, IDX_DIM, axis=1)).astype(BF16)
    wi_ref[...] = last * idx_scale


def _attn_proj(h2, g, w, qg, kg, tabs, tm, tiles_per_seq):
    rows = h2.shape[0]
    wcols = w.shape[1]
    row = lambda width: pl.BlockSpec((tm, width), lambda i: (i, 0))
    tab = pl.BlockSpec((tm, LANES), lambda i: (i % tiles_per_seq, 0))
    idx_scale = (IDX_HEADS ** -0.5) * (IDX_DIM ** -0.5)
    return pl.pallas_call(
        functools.partial(_attn_proj_kernel, idx_scale=idx_scale),
        grid=(rows // tm,),
        in_specs=[row(D_MODEL), _const_spec((1, D_MODEL)), _const_spec((D_MODEL, wcols)),
                  _const_spec((1, HEAD_DIM)), _const_spec((1, HEAD_DIM)), tab, tab, tab, tab],
        out_specs=[row(N_HEADS * HEAD_DIM), row(N_KV_HEADS * HEAD_DIM),
                   pl.BlockSpec((None, N_KV_HEADS * HEAD_DIM, tm), lambda i: (i, 0, 0)),
                   row(IDX_HEADS * IDX_DIM), row(LANES), row(LANES)],
        out_shape=[jax.ShapeDtypeStruct((rows, N_HEADS * HEAD_DIM), BF16),
                   jax.ShapeDtypeStruct((rows, N_KV_HEADS * HEAD_DIM), BF16),
                   jax.ShapeDtypeStruct((rows // tm, N_KV_HEADS * HEAD_DIM, tm), BF16),
                   jax.ShapeDtypeStruct((rows, IDX_HEADS * IDX_DIM), BF16),
                   jax.ShapeDtypeStruct((rows, LANES), BF16),
                   jax.ShapeDtypeStruct((rows, LANES), F32)],
        compiler_params=pltpu.CompilerParams(dimension_semantics=("arbitrary",),
                                             vmem_limit_bytes=VMEM_LIMIT),
        name="attn_proj",
    )(h2, g, w, qg, kg, *tabs)


def _dsa_kernel(q_ref, qi_ref, wi_ref, ki_ref, k_ref, vt_ref, out_ref,
                key_ref, bias_ref, qt_ref, qit_ref, acc_ref, *, n_valid, topk, tk):
    tq = TQ
    gw = KV_GROUP * tq
    t0 = pl.program_id(1) * tq
    tcol = t0 + lax.broadcasted_iota(jnp.int32, (1, tq), 1)
    kb = jnp.where(tcol < N_META, N_META,
                   N_META + CHUNK * (1 + jnp.right_shift(tcol - N_META, 6)))
    kb = jnp.where(tcol < n_valid, jnp.minimum(kb, n_valid), n_valid)
    kext = jnp.minimum(n_valid, N_META + CHUNK * (1 + (t0 + tq - 1 - N_META) // CHUNK))
    n_tiles = (kext + tk - 1) // tk

    for h in range(N_HEADS):
        qh = q_ref[:, h * HEAD_DIM:(h + 1) * HEAD_DIM].astype(F32).T
        qt_ref[h // KV_GROUP, :, (h % KV_GROUP) * tq:(h % KV_GROUP + 1) * tq] = qh.astype(BF16)
    row = lax.broadcasted_iota(jnp.int32, (LANES, tq), 0)
    for hp in range(IDX_HEADS // 2):
        pair = qi_ref[:, hp * LANES:(hp + 1) * LANES].astype(F32).T
        qit_ref[:, (2 * hp) * tq:(2 * hp + 1) * tq] = jnp.where(row < IDX_DIM, pair, 0.0).astype(BF16)
        qit_ref[:, (2 * hp + 1) * tq:(2 * hp + 2) * tq] = jnp.where(row >= IDX_DIM, pair, 0.0).astype(BF16)
    w = wi_ref[...].T[IDX_DIM:IDX_DIM + IDX_HEADS, :]

    def score_tile(i, carry):
        r0 = pl.multiple_of(i * tk, tk)
        kt = ki_ref[pl.ds(r0, tk), :]
        acc = jnp.zeros((tk, tq), F32)
        for hp in range(IDX_HEADS // 2):
            lg = jnp.dot(kt, qit_ref[:, hp * 2 * tq:(hp + 1) * 2 * tq],
                         preferred_element_type=F32)
            for hh in range(2):
                h = 2 * hp + hh
                acc = acc + jnp.maximum(lg[:, hh * tq:(hh + 1) * tq], 0.0) * w[h:h + 1, :]
        acc = jnp.where(acc == 0.0, 0.0, acc)
        pos = r0 + lax.broadcasted_iota(jnp.int32, (tk, tq), 0)
        bits = pltpu.bitcast(acc, jnp.int32)
        key = jnp.where(bits >= 0, bits, bits ^ 0x7FFFFFFF)
        key_ref[pl.ds(r0, tk), :] = jnp.where(pos < kb, key, KEY_MASKED)
        return carry

    lax.fori_loop(0, n_tiles, score_tile, 0)

    def count(pred):
        def body(i, acc):
            r0 = pl.multiple_of(i * tk, tk)
            blk = key_ref[pl.ds(r0, tk), :]
            pos = r0 + lax.broadcasted_iota(jnp.int32, (tk, tq), 0)
            ind = jnp.where(pred(blk, pos), 1, 0).astype(jnp.int32)
            return acc + ind.reshape(tk // SUBLANES, SUBLANES, tq).sum(axis=0)
        acc = lax.fori_loop(0, n_tiles, body, jnp.zeros((SUBLANES, tq), jnp.int32))
        return acc.sum(axis=0, keepdims=True)

    def value_step(it, lo):
        cand = lo + jnp.left_shift(jnp.int32(1), 31 - it)
        c = count(lambda blk, pos: blk >= cand)
        return jnp.where(c >= topk, cand, lo)

    tau = lax.fori_loop(0, 32, value_step, jnp.full((1, tq), INT_MIN, jnp.int32))

    c_gt = count(lambda blk, pos: blk > tau)
    c_ge = count(lambda blk, pos: blk >= tau)
    rank = topk - c_gt

    def tie_positions():
        def pos_step(it, lo):
            cand = lo + jnp.left_shift(jnp.int32(1), 14 - it)
            c = count(lambda blk, pos: (blk == tau) & (pos < cand))
            return jnp.where(c <= rank - 1, cand, lo)
        return lax.fori_loop(0, 15, pos_step, jnp.zeros((1, tq), jnp.int32))

    cut = jnp.max(c_ge - topk) > 0
    p_last = lax.cond(cut, tie_positions, lambda: jnp.full((1, tq), POS_BIG, jnp.int32))

    def bias_tile(i, carry):
        r0 = pl.multiple_of(i * tk, tk)
        blk = key_ref[pl.ds(r0, tk), :]
        pos = r0 + lax.broadcasted_iota(jnp.int32, (tk, tq), 0)
        keep = ((blk > tau) | ((blk == tau) & (pos <= p_last))) & (pos < kb)
        bias_ref[pl.ds(r0, tk), :] = jnp.where(keep, 0.0, NEG_BIG)
        return carry

    lax.fori_loop(0, n_tiles, bias_tile, 0)

    acc_ref[...] = jnp.zeros_like(acc_ref)

    def att_tile(i, carry):
        r0 = pl.multiple_of(i * tk, tk)
        bt = bias_ref[pl.ds(r0, tk), :]
        bt = jnp.concatenate([bt] * KV_GROUP, axis=1)
        out = []
        for n in range(N_KV_HEADS):
            m, l = carry[2 * n], carry[2 * n + 1]
            kt = k_ref[pl.ds(r0, tk), n * HEAD_DIM:(n + 1) * HEAD_DIM]
            s = jnp.dot(kt, qt_ref[n], preferred_element_type=F32) + bt
            m_new = jnp.maximum(m, jnp.max(s, axis=0, keepdims=True))
            alpha = jnp.exp2(m - m_new)
            p = jnp.exp2(s - m_new)
            l = alpha * l + jnp.sum(p, axis=0, keepdims=True)
            vt = vt_ref[i, n * HEAD_DIM:(n + 1) * HEAD_DIM, :]
            acc_ref[n] = alpha * acc_ref[n] + jnp.dot(vt, p.astype(BF16), preferred_element_type=F32)
            out += [m_new, l]
        return tuple(out)

    m0 = jnp.full((1, gw), NEG_BIG, F32)
    l0 = jnp.zeros((1, gw), F32)
    stats = lax.fori_loop(0, n_tiles, att_tile, (m0, l0) * N_KV_HEADS)
    for h in range(N_HEADS):
        n, g = h // KV_GROUP, h % KV_GROUP
        inv_l = 1.0 / stats[2 * n + 1][:, g * tq:(g + 1) * tq]
        oh = acc_ref[n, :, g * tq:(g + 1) * tq] * inv_l
        out_ref[:, h * HEAD_DIM:(h + 1) * HEAD_DIM] = oh.T.astype(BF16)


def _dsa_attention(q, qi, wi, ki, k, vt, n_valid, topk, tk):
    bsz, tp, _ = q.shape
    gw = KV_GROUP * TQ
    qrow = lambda width: pl.BlockSpec((None, TQ, width), lambda b, j: (b, j, 0))
    seq = lambda width: pl.BlockSpec((None, tp, width), lambda b, j: (b, 0, 0))
    return pl.pallas_call(
        functools.partial(_dsa_kernel, n_valid=n_valid, topk=topk, tk=tk),
        grid=(bsz, tp // TQ),
        in_specs=[qrow(N_HEADS * HEAD_DIM), qrow(IDX_HEADS * IDX_DIM), qrow(LANES), seq(LANES),
                  seq(N_KV_HEADS * HEAD_DIM),
                  pl.BlockSpec((None, tp // tk, N_KV_HEADS * HEAD_DIM, tk), lambda b, j: (b, 0, 0, 0))],
        out_specs=qrow(N_HEADS * HEAD_DIM),
        out_shape=jax.ShapeDtypeStruct((bsz, tp, N_HEADS * HEAD_DIM), BF16),
        scratch_shapes=[pltpu.VMEM((tp, TQ), jnp.int32), pltpu.VMEM((tp, TQ), F32),
                        pltpu.VMEM((N_KV_HEADS, HEAD_DIM, gw), BF16),
                        pltpu.VMEM((LANES, IDX_HEADS * TQ), BF16),
                        pltpu.VMEM((N_KV_HEADS, HEAD_DIM, gw), F32)],
        compiler_params=pltpu.CompilerParams(dimension_semantics=("arbitrary", "arbitrary"),
                                             vmem_limit_bytes=VMEM_LIMIT),
        name="dsa_attention",
    )(q, qi, wi, ki, k, vt)


def _rglru_kernel(h_ref, g_ref, w_ref, cw_ref, cb_ref, wa_ref, ba_ref, wx_ref, bx_ref, lam_ref,
                  out_ref, xext_ref, a_ref, b_ref, state_ref):
    tm = h_ref.shape[0]
    hist = SUBLANES

    @pl.when(pl.program_id(1) == 0)
    def _():
        xext_ref[0:hist, :] = jnp.zeros((hist, D_MODEL), F32)
        state_ref[...] = jnp.zeros_like(state_ref)

    n = _rms(h_ref[...], g_ref[...]).astype(BF16)
    xext_ref[hist:hist + tm, :] = jnp.dot(n, w_ref[:, 0:D_MODEL], preferred_element_type=F32)
    u = cb_ref[...] + cw_ref[0:1, :] * xext_ref[hist - RNN_CONV + 1:hist - RNN_CONV + 1 + tm, :]
    for kk in range(1, RNN_CONV):
        o = hist - RNN_CONV + 1 + kk
        u = u + cw_ref[kk:kk + 1, :] * xext_ref[o:o + tm, :]
    xext_ref[0:hist, :] = xext_ref[tm:tm + hist, :]

    ub = u.astype(BF16)
    rs, is_ = [], []
    for blk in range(RNN_BLOCKS):
        sl = slice(blk * RNN_BLOCK_DIM, (blk + 1) * RNN_BLOCK_DIM)
        rs.append(jnp.dot(ub[:, sl], wa_ref[blk], preferred_element_type=F32))
        is_.append(jnp.dot(ub[:, sl], wx_ref[blk], preferred_element_type=F32))
    r = jax.nn.sigmoid(jnp.concatenate(rs, axis=1) + ba_ref[...])
    ig = jax.nn.sigmoid(jnp.concatenate(is_, axis=1) + bx_ref[...])
    nl = -lam_ref[...]
    softplus = jnp.maximum(nl, 0.0) + jnp.log1p(jnp.exp(-jnp.abs(nl)))
    log_a = -LRU_C * r * softplus
    a_ref[...] = jnp.exp(log_a)
    b_ref[...] = jnp.sqrt(-_expm1(2.0 * log_a)) * (ig * u)

    def step(t, hcur):
        hcur = a_ref[pl.ds(t, 1), :] * hcur + b_ref[pl.ds(t, 1), :]
        b_ref[pl.ds(t, 1), :] = hcur
        return hcur

    state_ref[...] = lax.fori_loop(0, tm, step, state_ref[...], unroll=8)
    yg = jnp.dot(n, w_ref[:, D_MODEL:], preferred_element_type=F32)
    out_ref[...] = (b_ref[...] * jax.nn.gelu(yg)).astype(BF16)


def _rglru(h3, g, w, cw, cb, wa, ba, wx, bx, lam, tm):
    bsz, tp, _ = h3.shape
    row = pl.BlockSpec((None, tm, D_MODEL), lambda b, t: (b, t, 0))
    vec = _const_spec((1, D_MODEL))
    blkw = _const_spec((RNN_BLOCKS, RNN_BLOCK_DIM, RNN_BLOCK_DIM))
    return pl.pallas_call(
        _rglru_kernel,
        grid=(bsz, tp // tm),
        in_specs=[row, vec, _const_spec((D_MODEL, 2 * D_MODEL)), _const_spec((RNN_CONV, D_MODEL)), vec,
                  blkw, vec, blkw, vec, vec],
        out_specs=row,
        out_shape=jax.ShapeDtypeStruct((bsz, tp, D_MODEL), BF16),
        scratch_shapes=[pltpu.VMEM((tm + SUBLANES, D_MODEL), F32), pltpu.VMEM((tm, D_MODEL), F32),
                        pltpu.VMEM((tm, D_MODEL), F32), pltpu.VMEM((1, D_MODEL), F32)],
        compiler_params=pltpu.CompilerParams(dimension_semantics=("arbitrary", "arbitrary"),
                                             vmem_limit_bytes=VMEM_LIMIT),
        name="rglru",
    )(h3, g, w, cw, cb, wa, ba, wx, bx, lam)


CONV_HIST = 32
CONV_ROWS = 128


def _conformer_kernel(h_ref, g_ref, w_ref, dw_ref, db_ref, lg_ref, lb_ref, out_ref, xext_ref, y_ref):
    tm = h_ref.shape[0]

    @pl.when(pl.program_id(1) == 0)
    def _():
        xext_ref[0:CONV_HIST, :] = jnp.zeros((CONV_HIST, D_MODEL), F32)

    n = _rms(h_ref[...], g_ref[...]).astype(BF16)
    a = jnp.dot(n, w_ref[:, 0:D_MODEL], preferred_element_type=F32)
    gate = jnp.dot(n, w_ref[:, D_MODEL:], preferred_element_type=F32)
    xext_ref[CONV_HIST:CONV_HIST + tm, :] = a * jax.nn.sigmoid(gate)

    first = CONV_HIST - CONV_KERNEL + 1
    for r0 in range(0, tm, CONV_ROWS):
        for c0 in range(0, D_MODEL, LANES):
            cs = slice(c0, c0 + LANES)
            acc = jnp.broadcast_to(db_ref[:, cs], (CONV_ROWS, LANES))
            for kk in range(CONV_KERNEL):
                o = r0 + first + kk
                acc = acc + dw_ref[kk:kk + 1, cs] * xext_ref[o:o + CONV_ROWS, cs]
            y_ref[r0:r0 + CONV_ROWS, cs] = acc
    xext_ref[0:CONV_HIST, :] = xext_ref[tm:tm + CONV_HIST, :]

    y = y_ref[...]
    mu = jnp.mean(y, axis=-1, keepdims=True)
    yc = y - mu
    z = yc * lax.rsqrt(jnp.mean(yc * yc, axis=-1, keepdims=True) + NORM_EPS) * lg_ref[...] + lb_ref[...]
    out_ref[...] = jax.nn.silu(z).astype(BF16)


def _conformer(h3, g, w, dw, db, lg, lb, tm):
    bsz, tp, _ = h3.shape
    row = pl.BlockSpec((None, tm, D_MODEL), lambda b, t: (b, t, 0))
    vec = _const_spec((1, D_MODEL))
    return pl.pallas_call(
        _conformer_kernel,
        grid=(bsz, tp // tm),
        in_specs=[row, vec, _const_spec((D_MODEL, 2 * D_MODEL)), _const_spec((CONV_KERNEL, D_MODEL)),
                  vec, vec, vec],
        out_specs=row,
        out_shape=jax.ShapeDtypeStruct((bsz, tp, D_MODEL), BF16),
        scratch_shapes=[pltpu.VMEM((tm + CONV_HIST, D_MODEL), F32), pltpu.VMEM((tm, D_MODEL), F32)],
        compiler_params=pltpu.CompilerParams(dimension_semantics=("arbitrary", "arbitrary"),
                                             vmem_limit_bytes=VMEM_LIMIT),
        name="conformer",
    )(h3, g, w, dw, db, lg, lb)


def _merge_kernel(h_ref, at_ref, rn_ref, cv_ref, g_ref, wg_ref, wa_ref, wr_ref, wc_ref, wo_ref, out_ref):
    h = h_ref[...]
    n = _rms(h, g_ref[...]).astype(BF16)
    merged = None
    for i, (src, wref) in enumerate(((at_ref, wa_ref), (rn_ref, wr_ref), (cv_ref, wc_ref))):
        gate = jax.nn.sigmoid(jnp.dot(n, wg_ref[:, i * D_MODEL:(i + 1) * D_MODEL],
                                      preferred_element_type=F32))
        term = gate * jnp.dot(src[...], wref[...], preferred_element_type=F32)
        merged = term if merged is None else merged + term
    out_ref[...] = h + jnp.dot(merged.astype(BF16), wo_ref[...], preferred_element_type=F32)


def _merge(h2, attn, rnn, cnv, g, wg, wa, wr, wc, wo, tm):
    rows = h2.shape[0]
    row = pl.BlockSpec((tm, D_MODEL), lambda i: (i, 0))
    sq = _const_spec((D_MODEL, D_MODEL))
    return pl.pallas_call(
        _merge_kernel,
        grid=(rows // tm,),
        in_specs=[row, row, row, row, _const_spec((1, D_MODEL)), _const_spec((D_MODEL, 3 * D_MODEL)),
                  sq, sq, sq, sq],
        out_specs=row,
        out_shape=jax.ShapeDtypeStruct((rows, D_MODEL), F32),
        compiler_params=pltpu.CompilerParams(dimension_semantics=("arbitrary",),
                                             vmem_limit_bytes=VMEM_LIMIT),
        name="merge",
    )(h2, attn, rnn, cnv, g, wg, wa, wr, wc, wo)


def _ffn_kernel(h_ref, g_ref, wg_ref, wu_ref, wd_ref, out_ref):
    h = h_ref[...]
    f = _rms(h, g_ref[...]).astype(BF16)
    gate = jnp.dot(f, wg_ref[...], preferred_element_type=F32)
    up = jnp.dot(f, wu_ref[...], preferred_element_type=F32)
    act = (jax.nn.silu(gate) * up).astype(BF16)
    out_ref[...] = h + jnp.dot(act, wd_ref[...], preferred_element_type=F32)


def _ffn(h2, g, wg, wu, wd, tm):
    rows = h2.shape[0]
    dff = wg.shape[1]
    row = pl.BlockSpec((tm, D_MODEL), lambda i: (i, 0))
    return pl.pallas_call(
        _ffn_kernel,
        grid=(rows // tm,),
        in_specs=[row, _const_spec((1, D_MODEL)), _const_spec((D_MODEL, dff)), _const_spec((D_MODEL, dff)),
                  _const_spec((dff, D_MODEL))],
        out_specs=row,
        out_shape=jax.ShapeDtypeStruct((rows, D_MODEL), F32),
        compiler_params=pltpu.CompilerParams(dimension_semantics=("arbitrary",),
                                             vmem_limit_bytes=VMEM_LIMIT),
        name="ffn",
    )(h2, g, wg, wu, wd)


def _rope_tables(n, dim):
    inv = ROPE_THETA ** (-jnp.arange(0, dim, 2, dtype=F32) / dim)
    ang = jnp.arange(n, dtype=F32)[:, None] * inv[None, :]
    return jnp.cos(ang), jnp.sin(ang)


def kernel(x, meta, mix_norm_g, w_in, q_norm_g, k_norm_g, rnn_conv_w, rnn_conv_b, rnn_wa, rnn_ba, rnn_wx, rnn_bx, rnn_lambda, conv_dw_w, conv_dw_b, conv_ln_g, conv_ln_b, w_o_attn, w_o_rnn, w_o_conv, w_out, ffn_norm_g, w_ffn_gate, w_ffn_up, w_ffn_down):
    bsz, seq, _ = x.shape
    depth = w_in.shape[0]
    t_valid = seq + N_META
    tp = -(-t_valid // TQ) * TQ
    topk = min(TOPK_MAX, seq // 4)
    rows = bsz * tp
    tm_seq = _pick_tile(tp, (384, 256, 128))
    tm_flat = _pick_tile(rows, (512, 384, 256, 128))

    q_w, kv_w, qi_w = N_HEADS * HEAD_DIM, N_KV_HEADS * HEAD_DIM, IDX_HEADS * IDX_DIM
    o = np.cumsum([0, q_w, kv_w, kv_w, qi_w, IDX_DIM, IDX_HEADS, D_MODEL, D_MODEL, 2 * D_MODEL, 3 * D_MODEL])
    pad_w = LANES - IDX_DIM - IDX_HEADS
    w_attn = jnp.concatenate([w_in[:, :, o[0]:o[6]], jnp.zeros((depth, D_MODEL, pad_w), w_in.dtype)],
                             axis=2).astype(BF16)
    w_rnn = w_in[:, :, o[6]:o[8]].astype(BF16)
    w_cnv = w_in[:, :, o[8]:o[9]].astype(BF16)
    w_gate = w_in[:, :, o[9]:o[10]].astype(BF16)
    wa_b, wx_b = rnn_wa.astype(BF16), rnn_wx.astype(BF16)
    w_oa, w_or, w_oc, w_ot = (a.astype(BF16) for a in (w_o_attn, w_o_rnn, w_o_conv, w_out))
    w_fg, w_fu, w_fd = (a.astype(BF16) for a in (w_ffn_gate, w_ffn_up, w_ffn_down))

    cos_a, sin_a = _rope_tables(tp, HEAD_DIM)
    cos_i, sin_i = _rope_tables(tp, IDX_DIM)
    tabs = (jnp.concatenate([cos_a, cos_a], axis=1), jnp.concatenate([-sin_a, sin_a], axis=1),
            jnp.concatenate([cos_i] * 4, axis=1), jnp.concatenate([-sin_i, sin_i] * 2, axis=1))

    h = jnp.concatenate([jnp.broadcast_to(meta[None].astype(x.dtype), (bsz, N_META, D_MODEL)), x,
                         jnp.zeros((bsz, tp - t_valid, D_MODEL), x.dtype)], axis=1)
    vec = lambda a: a.reshape(1, -1)

    for l in range(depth):
        h2 = h.reshape(rows, D_MODEL)
        q, k, vt, qi, ki, wi = _attn_proj(h2, vec(mix_norm_g[l]), w_attn[l], vec(q_norm_g[l]),
                                          vec(k_norm_g[l]), tabs, tm_seq, tp // tm_seq)
        seq3 = lambda a: a.reshape(bsz, tp, a.shape[-1])
        attn = _dsa_attention(seq3(q), seq3(qi), seq3(wi), seq3(ki), seq3(k),
                              vt.reshape(bsz, tp // tm_seq, N_KV_HEADS * HEAD_DIM, tm_seq),
                              t_valid, topk, tm_seq)
        attn = attn.reshape(rows, N_HEADS * HEAD_DIM)

        rnn = _rglru(h, vec(mix_norm_g[l]), w_rnn[l], rnn_conv_w[l], vec(rnn_conv_b[l]), wa_b[l],
                     vec(rnn_ba[l]), wx_b[l], vec(rnn_bx[l]), vec(rnn_lambda[l]), tm_seq)
        cnv = _conformer(h, vec(mix_norm_g[l]), w_cnv[l], conv_dw_w[l], vec(conv_dw_b[l]),
                         vec(conv_ln_g[l]), vec(conv_ln_b[l]), tm_seq)
        h2 = _merge(h2, attn, rnn.reshape(rows, D_MODEL), cnv.reshape(rows, D_MODEL), vec(mix_norm_g[l]),
                    w_gate[l], w_oa[l], w_or[l], w_oc[l], w_ot[l], tm_flat)
        h2 = _ffn(h2, vec(ffn_norm_g[l]), w_fg[l], w_fu[l], w_fd[l], tm_flat)
        h = h2.reshape(bsz, tp, D_MODEL)

    return h[:, N_META:t_valid]
```

```python
import functools

import jax
import jax.numpy as jnp
import numpy as np
from jax import lax
from jax.experimental import pallas as pl
from jax.experimental.pallas import tpu as pltpu

D_MODEL = 1024
CHUNK = 64
N_META = 16
N_HEADS = 8
N_KV_HEADS = 2
HEAD_DIM = 128
KV_GROUP = N_HEADS // N_KV_HEADS
IDX_HEADS = 8
IDX_DIM = 64
TOPK_MAX = 256
ROPE_THETA = 10000.0
RNN_BLOCKS = 8
RNN_BLOCK_DIM = D_MODEL // RNN_BLOCKS
RNN_CONV = 4
LRU_C = 8.0
CONV_KERNEL = 31
NORM_EPS = 1e-6

LANES = 128
SUBLANES = 8
PACKED_ROWS = 16
TQ = 128
VMEM_LIMIT = 56 * 1024 * 1024
LOG2E = 1.4426950408889634

I16_MIN = -(2 ** 15)
KEY_MASKED = -2139095041
NEG_BIG = -1e30
POS_BIG = 2 ** 30

F32 = jnp.float32
BF16 = jnp.bfloat16


def _rms(x, g):
    return x * lax.rsqrt(jnp.mean(x * x, axis=-1, keepdims=True) + NORM_EPS) * g


def _expm1(y):
    u = jnp.exp(y)
    near = jnp.where(u == 1.0, y, (u - 1.0) * y / jnp.log(jnp.where(u == 1.0, 2.0, u)))
    return jnp.where(y > -0.5, near, u - 1.0)


def _const_spec(shape):
    nd = len(shape)
    return pl.BlockSpec(shape, lambda *_: (0,) * nd, pipeline_mode=pl.Buffered(1))


def _pick_tile(n, candidates):
    for c in candidates:
        if n % c == 0:
            return c
    raise ValueError(f"no tile for {n}")


def _attn_proj_kernel(x_ref, g_ref, w_ref, qg_ref, kg_ref, cosa_ref, sina_ref, cosi_ref, sini_ref,
                      q_ref, k_ref, vt_ref, qi_ref, ki_ref, wi_ref, *, idx_scale):
    n = _rms(x_ref[...], g_ref[...]).astype(BF16)
    cosa, sina = cosa_ref[...], sina_ref[...]
    cosi, sini = cosi_ref[...], sini_ref[...]
    lane = lax.broadcasted_iota(jnp.int32, cosi.shape, 1)
    low_half = (lane % IDX_DIM) < (IDX_DIM // 2)

    def rot_attn(a, g):
        a = _rms(a, g)
        return a * cosa + pltpu.roll(a, HEAD_DIM // 2, axis=1) * sina

    def rot_idx(a):
        partner = jnp.where(low_half, pltpu.roll(a, LANES - IDX_DIM // 2, axis=1),
                            pltpu.roll(a, IDX_DIM // 2, axis=1))
        return a * cosi + partner * sini

    q_w = N_HEADS * HEAD_DIM
    kv_w = N_KV_HEADS * HEAD_DIM
    qi_w = IDX_HEADS * IDX_DIM
    o_k, o_v, o_qi, o_ki = q_w, q_w + kv_w, q_w + 2 * kv_w, q_w + 2 * kv_w + qi_w

    q_scale = (HEAD_DIM ** -0.5) * LOG2E
    pq = jnp.dot(n, w_ref[:, 0:q_w], preferred_element_type=F32)
    for h in range(N_HEADS):
        sl = slice(h * HEAD_DIM, (h + 1) * HEAD_DIM)
        q_ref[:, sl] = (rot_attn(pq[:, sl], qg_ref[...]) * q_scale).astype(BF16)
    pk = jnp.dot(n, w_ref[:, o_k:o_qi], preferred_element_type=F32)
    for h in range(N_KV_HEADS):
        sl = slice(h * HEAD_DIM, (h + 1) * HEAD_DIM)
        k_ref[:, sl] = rot_attn(pk[:, sl], kg_ref[...]).astype(BF16)
        vt_ref[sl, :] = pk[:, kv_w + h * HEAD_DIM:kv_w + (h + 1) * HEAD_DIM].T.astype(BF16)
    pi = jnp.dot(n, w_ref[:, o_qi:], preferred_element_type=F32)
    for c in range(qi_w // LANES):
        sl = slice(c * LANES, (c + 1) * LANES)
        qi_ref[:, sl] = rot_idx(pi[:, sl]).astype(BF16)
    last = pi[:, qi_w:]
    ki = rot_idx(last)
    ki_ref[...] = jnp.where(lane < IDX_DIM, ki, pltpu.roll(ki, IDX_DIM, axis=1)).astype(BF16)
    wi_ref[...] = last * idx_scale


def _attn_proj(h2, g, w, qg, kg, tabs, tm, tiles_per_seq):
    rows = h2.shape[0]
    wcols = w.shape[1]
    row = lambda width: pl.BlockSpec((tm, width), lambda i: (i, 0))
    tab = pl.BlockSpec((tm, LANES), lambda i: (i % tiles_per_seq, 0))
    idx_scale = (IDX_HEADS ** -0.5) * (IDX_DIM ** -0.5)
    return pl.pallas_call(
        functools.partial(_attn_proj_kernel, idx_scale=idx_scale),
        grid=(rows // tm,),
        in_specs=[row(D_MODEL), _const_spec((1, D_MODEL)), _const_spec((D_MODEL, wcols)),
                  _const_spec((1, HEAD_DIM)), _const_spec((1, HEAD_DIM)), tab, tab, tab, tab],
        out_specs=[row(N_HEADS * HEAD_DIM), row(N_KV_HEADS * HEAD_DIM),
                   pl.BlockSpec((None, N_KV_HEADS * HEAD_DIM, tm), lambda i: (i, 0, 0)),
                   row(IDX_HEADS * IDX_DIM), row(LANES), row(LANES)],
        out_shape=[jax.ShapeDtypeStruct((rows, N_HEADS * HEAD_DIM), BF16),
                   jax.ShapeDtypeStruct((rows, N_KV_HEADS * HEAD_DIM), BF16),
                   jax.ShapeDtypeStruct((rows // tm, N_KV_HEADS * HEAD_DIM, tm), BF16),
                   jax.ShapeDtypeStruct((rows, IDX_HEADS * IDX_DIM), BF16),
                   jax.ShapeDtypeStruct((rows, LANES), BF16),
                   jax.ShapeDtypeStruct((rows, LANES), F32)],
        compiler_params=pltpu.CompilerParams(dimension_semantics=("arbitrary",),
                                             vmem_limit_bytes=VMEM_LIMIT),
        name="attn_proj",
    )(h2, g, w, qg, kg, *tabs)


def _dsa_kernel(q_ref, qi_ref, wi_ref, ki_ref, k_ref, vt_ref, out_ref,
                key_ref, hi_ref, lo_ref, bias_ref, qt_ref, qit_ref, acc_ref, m_ref, l_ref, sa_ref, sb_ref,
                *, n_valid, topk, tk):
    tq = TQ
    gw = KV_GROUP * tq
    t0 = pl.program_id(1) * tq
    tcol = t0 + lax.broadcasted_iota(jnp.int32, (1, tq), 1)
    kb = jnp.where(tcol < N_META, N_META,
                   N_META + CHUNK * (1 + jnp.right_shift(tcol - N_META, 6)))
    kb = jnp.where(tcol < n_valid, jnp.minimum(kb, n_valid), n_valid)
    kext = jnp.minimum(n_valid, N_META + CHUNK * (1 + (t0 + tq - 1 - N_META) // CHUNK))
    n_tiles = (kext + tk - 1) // tk

    for h in range(N_HEADS):
        qh = q_ref[:, h * HEAD_DIM:(h + 1) * HEAD_DIM].astype(F32).T
        qt_ref[h // KV_GROUP, :, (h % KV_GROUP) * tq:(h % KV_GROUP + 1) * tq] = qh.astype(BF16)
    row = lax.broadcasted_iota(jnp.int32, (LANES, tq), 0)
    for hp in range(IDX_HEADS // 2):
        pair = qi_ref[:, hp * LANES:(hp + 1) * LANES].astype(F32).T
        qit_ref[:, (2 * hp) * tq:(2 * hp + 1) * tq] = jnp.where(row < IDX_DIM, pair, 0.0).astype(BF16)
        qit_ref[:, (2 * hp + 1) * tq:(2 * hp + 2) * tq] = jnp.where(row >= IDX_DIM, pair, 0.0).astype(BF16)
    w = wi_ref[...].T[IDX_DIM:IDX_DIM + IDX_HEADS, :]

    def score_tile(i, carry):
        r0 = pl.multiple_of(i * tk, tk)
        kt = ki_ref[pl.ds(r0, tk), :]
        acc = jnp.zeros((tk, tq), F32)
        for hp in range(IDX_HEADS // 2):
            lg = jnp.dot(kt, qit_ref[:, hp * 2 * tq:(hp + 1) * 2 * tq],
                         preferred_element_type=F32)
            for hh in range(2):
                h = 2 * hp + hh
                acc = acc + jnp.maximum(lg[:, hh * tq:(hh + 1) * tq], 0.0) * w[h:h + 1, :]
        acc = jnp.where(acc == 0.0, 0.0, acc)
        pos = r0 + lax.broadcasted_iota(jnp.int32, (tk, tq), 0)
        bits = pltpu.bitcast(acc, jnp.int32)
        key = jnp.where(bits >= 0, bits, bits ^ 0x7FFFFFFF)
        key = jnp.where(pos < kb, key, KEY_MASKED)
        key_ref[pl.ds(r0, tk), :] = key
        hi_ref[pl.ds(r0, tk), :] = lax.shift_right_arithmetic(key, 16).astype(jnp.int16)
        return carry

    lax.fori_loop(0, n_tiles, score_tile, 0)

    def search16(ref, kth):
        def count_ge(cand):
            c16 = cand.astype(jnp.int16)

            def body(i, acc):
                r0 = pl.multiple_of(i * tk, tk)
                ind = jnp.where(ref[pl.ds(r0, tk), :] >= c16, jnp.bfloat16(1), jnp.bfloat16(0))
                parts = [ind[r * PACKED_ROWS:(r + 1) * PACKED_ROWS] for r in range(tk // PACKED_ROWS)]
                while len(parts) > 1:
                    nxt = [parts[r] + parts[r + 1] for r in range(0, len(parts) - 1, 2)]
                    parts = nxt + ([parts[-1]] if len(parts) % 2 else [])
                return acc + parts[0].astype(F32)

            acc = lax.fori_loop(0, n_tiles, body, jnp.zeros((PACKED_ROWS, tq), F32))
            return acc.sum(axis=0, keepdims=True)

        def step(it, lo):
            cand = lo + jnp.left_shift(jnp.int32(1), 15 - it)
            return jnp.where(count_ge(cand) >= kth, cand, lo)

        return lax.fori_loop(0, 16, step, jnp.full((1, tq), I16_MIN, jnp.int32))

    tau_hi = search16(hi_ref, jnp.full((1, tq), float(topk), F32))

    def low_tile(i, acc):
        r0 = pl.multiple_of(i * tk, tk)
        blk = key_ref[pl.ds(r0, tk), :]
        hi = lax.shift_right_arithmetic(blk, 16)
        low = jnp.where(hi == tau_hi, (blk & 0xFFFF) + I16_MIN, I16_MIN)
        lo_ref[pl.ds(r0, tk), :] = low.astype(jnp.int16)
        above = jnp.where(hi > tau_hi, 1, 0).astype(jnp.int32)
        return acc + above.reshape(tk // SUBLANES, SUBLANES, tq).sum(axis=0)

    c_above = lax.fori_loop(0, n_tiles, low_tile, jnp.zeros((SUBLANES, tq), jnp.int32))
    c_above = c_above.sum(axis=0, keepdims=True)
    tau_lo = search16(lo_ref, (topk - c_above).astype(F32))
    tau = tau_hi * 65536 + (tau_lo - I16_MIN)

    def count(pred):
        def body(i, acc):
            r0 = pl.multiple_of(i * tk, tk)
            blk = key_ref[pl.ds(r0, tk), :]
            pos = r0 + lax.broadcasted_iota(jnp.int32, (tk, tq), 0)
            ind = jnp.where(pred(blk, pos), 1, 0).astype(jnp.int32)
            return acc + ind.reshape(tk // SUBLANES, SUBLANES, tq).sum(axis=0)
        acc = lax.fori_loop(0, n_tiles, body, jnp.zeros((SUBLANES, tq), jnp.int32))
        return acc.sum(axis=0, keepdims=True)

    c_gt = count(lambda blk, pos: blk > tau)
    c_ge = count(lambda blk, pos: blk >= tau)
    rank = topk - c_gt

    def tie_positions():
        def pos_step(it, lo):
            cand = lo + jnp.left_shift(jnp.int32(1), 14 - it)
            c = count(lambda blk, pos: (blk == tau) & (pos < cand))
            return jnp.where(c <= rank - 1, cand, lo)
        return lax.fori_loop(0, 15, pos_step, jnp.zeros((1, tq), jnp.int32))

    cut = jnp.max(c_ge - topk) > 0
    p_last = lax.cond(cut, tie_positions, lambda: jnp.full((1, tq), POS_BIG, jnp.int32))

    def bias_tile(i, carry):
        r0 = pl.multiple_of(i * tk, tk)
        blk = key_ref[pl.ds(r0, tk), :]
        pos = r0 + lax.broadcasted_iota(jnp.int32, (tk, tq), 0)
        keep = ((blk > tau) | ((blk == tau) & (pos <= p_last))) & (pos < kb)
        bias_ref[pl.ds(r0, tk), :] = jnp.where(keep, 0.0, NEG_BIG)
        return carry

    lax.fori_loop(0, n_tiles, bias_tile, 0)

    acc_ref[...] = jnp.zeros_like(acc_ref)
    m_ref[...] = jnp.full(m_ref.shape, NEG_BIG, F32)
    l_ref[...] = jnp.zeros_like(l_ref)

    def scores(t, s_ref):
        r0 = pl.multiple_of(t * tk, tk)
        for n in range(N_KV_HEADS):
            kt = k_ref[pl.ds(r0, tk), n * HEAD_DIM:(n + 1) * HEAD_DIM]
            s_ref[n] = jnp.dot(kt, qt_ref[n], preferred_element_type=F32)

    def softmax_pv(t, s_ref):
        r0 = pl.multiple_of(t * tk, tk)
        bt = bias_ref[pl.ds(r0, tk), :]
        bt = jnp.concatenate([bt] * KV_GROUP, axis=1)
        for n in range(N_KV_HEADS):
            s = s_ref[n] + bt
            m = m_ref[n]
            m_new = jnp.maximum(m, jnp.max(s, axis=0, keepdims=True))
            alpha = jnp.exp2(m - m_new)
            p = jnp.exp2(s - m_new)
            l_ref[n] = alpha * l_ref[n] + jnp.sum(p, axis=0, keepdims=True)
            m_ref[n] = m_new
            vt = vt_ref[t, n * HEAD_DIM:(n + 1) * HEAD_DIM, :]
            acc_ref[n] = alpha * acc_ref[n] + jnp.dot(vt, p.astype(BF16), preferred_element_type=F32)

    last_tile = k_ref.shape[0] // tk - 1
    scores(0, sa_ref)

    def tile_pair(i, carry):
        t = 2 * i
        scores(t + 1, sb_ref)
        softmax_pv(t, sa_ref)
        scores(jnp.minimum(t + 2, last_tile), sa_ref)
        softmax_pv(t + 1, sb_ref)
        return carry

    lax.fori_loop(0, n_tiles // 2, tile_pair, 0)

    @pl.when(n_tiles % 2 == 1)
    def _():
        softmax_pv(n_tiles - 1, sa_ref)

    for h in range(N_HEADS):
        n, g = h // KV_GROUP, h % KV_GROUP
        inv_l = 1.0 / l_ref[n][:, g * tq:(g + 1) * tq]
        oh = acc_ref[n, :, g * tq:(g + 1) * tq] * inv_l
        out_ref[:, h * HEAD_DIM:(h + 1) * HEAD_DIM] = oh.T.astype(BF16)


def _dsa_attention(q, qi, wi, ki, k, vt, n_valid, topk, tk):
    bsz, tp, _ = q.shape
    gw = KV_GROUP * TQ
    qrow = lambda width: pl.BlockSpec((None, TQ, width), lambda b, j: (b, j, 0))
    seq = lambda width: pl.BlockSpec((None, tp, width), lambda b, j: (b, 0, 0))
    return pl.pallas_call(
        functools.partial(_dsa_kernel, n_valid=n_valid, topk=topk, tk=tk),
        grid=(bsz, tp // TQ),
        in_specs=[qrow(N_HEADS * HEAD_DIM), qrow(IDX_HEADS * IDX_DIM), qrow(LANES), seq(LANES),
                  seq(N_KV_HEADS * HEAD_DIM),
                  pl.BlockSpec((None, tp // tk, N_KV_HEADS * HEAD_DIM, tk), lambda b, j: (b, 0, 0, 0))],
        out_specs=qrow(N_HEADS * HEAD_DIM),
        out_shape=jax.ShapeDtypeStruct((bsz, tp, N_HEADS * HEAD_DIM), BF16),
        scratch_shapes=[pltpu.VMEM((tp, TQ), jnp.int32), pltpu.VMEM((tp, TQ), jnp.int16),
                        pltpu.VMEM((tp, TQ), jnp.int16), pltpu.VMEM((tp, TQ), F32),
                        pltpu.VMEM((N_KV_HEADS, HEAD_DIM, gw), BF16),
                        pltpu.VMEM((LANES, IDX_HEADS * TQ), BF16),
                        pltpu.VMEM((N_KV_HEADS, HEAD_DIM, gw), F32),
                        pltpu.VMEM((N_KV_HEADS, 1, gw), F32), pltpu.VMEM((N_KV_HEADS, 1, gw), F32),
                        pltpu.VMEM((N_KV_HEADS, tk, gw), F32), pltpu.VMEM((N_KV_HEADS, tk, gw), F32)],
        compiler_params=pltpu.CompilerParams(dimension_semantics=("arbitrary", "arbitrary"),
                                             vmem_limit_bytes=VMEM_LIMIT),
        name="dsa_attention",
    )(q, qi, wi, ki, k, vt)


def _rglru_kernel(h_ref, g_ref, w_ref, cw_ref, cb_ref, wa_ref, ba_ref, wx_ref, bx_ref, lam_ref,
                  out_ref, xext_ref, a_ref, b_ref, state_ref):
    tm = h_ref.shape[0]
    hist = SUBLANES

    @pl.when(pl.program_id(1) == 0)
    def _():
        xext_ref[0:hist, :] = jnp.zeros((hist, D_MODEL), F32)
        state_ref[...] = jnp.zeros_like(state_ref)

    n = _rms(h_ref[...], g_ref[...]).astype(BF16)
    xext_ref[hist:hist + tm, :] = jnp.dot(n, w_ref[:, 0:D_MODEL], preferred_element_type=F32)
    u = cb_ref[...] + cw_ref[0:1, :] * xext_ref[hist - RNN_CONV + 1:hist - RNN_CONV + 1 + tm, :]
    for kk in range(1, RNN_CONV):
        o = hist - RNN_CONV + 1 + kk
        u = u + cw_ref[kk:kk + 1, :] * xext_ref[o:o + tm, :]
    xext_ref[0:hist, :] = xext_ref[tm:tm + hist, :]

    ub = u.astype(BF16)
    rs, is_ = [], []
    for blk in range(RNN_BLOCKS):
        sl = slice(blk * RNN_BLOCK_DIM, (blk + 1) * RNN_BLOCK_DIM)
        rs.append(jnp.dot(ub[:, sl], wa_ref[blk], preferred_element_type=F32))
        is_.append(jnp.dot(ub[:, sl], wx_ref[blk], preferred_element_type=F32))
    r = jax.nn.sigmoid(jnp.concatenate(rs, axis=1) + ba_ref[...])
    ig = jax.nn.sigmoid(jnp.concatenate(is_, axis=1) + bx_ref[...])
    nl = -lam_ref[...]
    softplus = jnp.maximum(nl, 0.0) + jnp.log1p(jnp.exp(-jnp.abs(nl)))
    log_a = -LRU_C * r * softplus
    a_ref[...] = jnp.exp(log_a)
    b_ref[...] = jnp.sqrt(-_expm1(2.0 * log_a)) * (ig * u)

    def step(t, hcur):
        hcur = a_ref[pl.ds(t, 1), :] * hcur + b_ref[pl.ds(t, 1), :]
        b_ref[pl.ds(t, 1), :] = hcur
        return hcur

    state_ref[...] = lax.fori_loop(0, tm, step, state_ref[...], unroll=8)
    yg = jnp.dot(n, w_ref[:, D_MODEL:], preferred_element_type=F32)
    out_ref[...] = (b_ref[...] * jax.nn.gelu(yg)).astype(BF16)


def _rglru(h3, g, w, cw, cb, wa, ba, wx, bx, lam, tm):
    bsz, tp, _ = h3.shape
    row = pl.BlockSpec((None, tm, D_MODEL), lambda b, t: (b, t, 0))
    vec = _const_spec((1, D_MODEL))
    blkw = _const_spec((RNN_BLOCKS, RNN_BLOCK_DIM, RNN_BLOCK_DIM))
    return pl.pallas_call(
        _rglru_kernel,
        grid=(bsz, tp // tm),
        in_specs=[row, vec, _const_spec((D_MODEL, 2 * D_MODEL)), _const_spec((RNN_CONV, D_MODEL)), vec,
                  blkw, vec, blkw, vec, vec],
        out_specs=row,
        out_shape=jax.ShapeDtypeStruct((bsz, tp, D_MODEL), BF16),
        scratch_shapes=[pltpu.VMEM((tm + SUBLANES, D_MODEL), F32), pltpu.VMEM((tm, D_MODEL), F32),
                        pltpu.VMEM((tm, D_MODEL), F32), pltpu.VMEM((1, D_MODEL), F32)],
        compiler_params=pltpu.CompilerParams(dimension_semantics=("arbitrary", "arbitrary"),
                                             vmem_limit_bytes=VMEM_LIMIT),
        name="rglru",
    )(h3, g, w, cw, cb, wa, ba, wx, bx, lam)


CONV_HIST = 32
CONV_ROWS = 128


def _conformer_kernel(h_ref, g_ref, w_ref, dw_ref, db_ref, lg_ref, lb_ref, out_ref, xext_ref, y_ref):
    tm = h_ref.shape[0]

    @pl.when(pl.program_id(1) == 0)
    def _():
        xext_ref[0:CONV_HIST, :] = jnp.zeros((CONV_HIST, D_MODEL), F32)

    n = _rms(h_ref[...], g_ref[...]).astype(BF16)
    a = jnp.dot(n, w_ref[:, 0:D_MODEL], preferred_element_type=F32)
    gate = jnp.dot(n, w_ref[:, D_MODEL:], preferred_element_type=F32)
    xext_ref[CONV_HIST:CONV_HIST + tm, :] = a * jax.nn.sigmoid(gate)

    first = CONV_HIST - CONV_KERNEL + 1
    for r0 in range(0, tm, CONV_ROWS):
        for c0 in range(0, D_MODEL, LANES):
            cs = slice(c0, c0 + LANES)
            acc = jnp.broadcast_to(db_ref[:, cs], (CONV_ROWS, LANES))
            for kk in range(CONV_KERNEL):
                o = r0 + first + kk
                acc = acc + dw_ref[kk:kk + 1, cs] * xext_ref[o:o + CONV_ROWS, cs]
            y_ref[r0:r0 + CONV_ROWS, cs] = acc
    xext_ref[0:CONV_HIST, :] = xext_ref[tm:tm + CONV_HIST, :]

    y = y_ref[...]
    mu = jnp.mean(y, axis=-1, keepdims=True)
    yc = y - mu
    z = yc * lax.rsqrt(jnp.mean(yc * yc, axis=-1, keepdims=True) + NORM_EPS) * lg_ref[...] + lb_ref[...]
    out_ref[...] = jax.nn.silu(z).astype(BF16)


def _conformer(h3, g, w, dw, db, lg, lb, tm):
    bsz, tp, _ = h3.shape
    row = pl.BlockSpec((None, tm, D_MODEL), lambda b, t: (b, t, 0))
    vec = _const_spec((1, D_MODEL))
    return pl.pallas_call(
        _conformer_kernel,
        grid=(bsz, tp // tm),
        in_specs=[row, vec, _const_spec((D_MODEL, 2 * D_MODEL)), _const_spec((CONV_KERNEL, D_MODEL)),
                  vec, vec, vec],
        out_specs=row,
        out_shape=jax.ShapeDtypeStruct((bsz, tp, D_MODEL), BF16),
        scratch_shapes=[pltpu.VMEM((tm + CONV_HIST, D_MODEL), F32), pltpu.VMEM((tm, D_MODEL), F32)],
        compiler_params=pltpu.CompilerParams(dimension_semantics=("arbitrary", "arbitrary"),
                                             vmem_limit_bytes=VMEM_LIMIT),
        name="conformer",
    )(h3, g, w, dw, db, lg, lb)


def _merge_kernel(h_ref, at_ref, rn_ref, cv_ref, g_ref, wg_ref, wa_ref, wr_ref, wc_ref, wo_ref, out_ref):
    h = h_ref[...]
    n = _rms(h, g_ref[...]).astype(BF16)
    merged = None
    for i, (src, wref) in enumerate(((at_ref, wa_ref), (rn_ref, wr_ref), (cv_ref, wc_ref))):
        gate = jax.nn.sigmoid(jnp.dot(n, wg_ref[:, i * D_MODEL:(i + 1) * D_MODEL],
                                      preferred_element_type=F32))
        term = gate * jnp.dot(src[...], wref[...], preferred_element_type=F32)
        merged = term if merged is None else merged + term
    out_ref[...] = h + jnp.dot(merged.astype(BF16), wo_ref[...], preferred_element_type=F32)


def _merge(h2, attn, rnn, cnv, g, wg, wa, wr, wc, wo, tm):
    rows = h2.shape[0]
    row = pl.BlockSpec((tm, D_MODEL), lambda i: (i, 0))
    sq = _const_spec((D_MODEL, D_MODEL))
    return pl.pallas_call(
        _merge_kernel,
        grid=(rows // tm,),
        in_specs=[row, row, row, row, _const_spec((1, D_MODEL)), _const_spec((D_MODEL, 3 * D_MODEL)),
                  sq, sq, sq, sq],
        out_specs=row,
        out_shape=jax.ShapeDtypeStruct((rows, D_MODEL), F32),
        compiler_params=pltpu.CompilerParams(dimension_semantics=("arbitrary",),
                                             vmem_limit_bytes=VMEM_LIMIT),
        name="merge",
    )(h2, attn, rnn, cnv, g, wg, wa, wr, wc, wo)


def _ffn_kernel(h_ref, g_ref, wg_ref, wu_ref, wd_ref, out_ref):
    h = h_ref[...]
    f = _rms(h, g_ref[...]).astype(BF16)
    gate = jnp.dot(f, wg_ref[...], preferred_element_type=F32)
    up = jnp.dot(f, wu_ref[...], preferred_element_type=F32)
    act = (jax.nn.silu(gate) * up).astype(BF16)
    out_ref[...] = h + jnp.dot(act, wd_ref[...], preferred_element_type=F32)


def _ffn(h2, g, wg, wu, wd, tm):
    rows = h2.shape[0]
    dff = wg.shape[1]
    row = pl.BlockSpec((tm, D_MODEL), lambda i: (i, 0))
    return pl.pallas_call(
        _ffn_kernel,
        grid=(rows // tm,),
        in_specs=[row, _const_spec((1, D_MODEL)), _const_spec((D_MODEL, dff)), _const_spec((D_MODEL, dff)),
                  _const_spec((dff, D_MODEL))],
        out_specs=row,
        out_shape=jax.ShapeDtypeStruct((rows, D_MODEL), F32),
        compiler_params=pltpu.CompilerParams(dimension_semantics=("arbitrary",),
                                             vmem_limit_bytes=VMEM_LIMIT),
        name="ffn",
    )(h2, g, wg, wu, wd)


def _rope_tables(n, dim):
    inv = ROPE_THETA ** (-jnp.arange(0, dim, 2, dtype=F32) / dim)
    ang = jnp.arange(n, dtype=F32)[:, None] * inv[None, :]
    return jnp.cos(ang), jnp.sin(ang)


def kernel(x, meta, mix_norm_g, w_in, q_norm_g, k_norm_g, rnn_conv_w, rnn_conv_b, rnn_wa, rnn_ba, rnn_wx, rnn_bx, rnn_lambda, conv_dw_w, conv_dw_b, conv_ln_g, conv_ln_b, w_o_attn, w_o_rnn, w_o_conv, w_out, ffn_norm_g, w_ffn_gate, w_ffn_up, w_ffn_down):
    bsz, seq, _ = x.shape
    depth = w_in.shape[0]
    t_valid = seq + N_META
    tp = -(-t_valid // TQ) * TQ
    topk = min(TOPK_MAX, seq // 4)
    rows = bsz * tp
    tm_seq = _pick_tile(tp, (384, 256, 128))
    tm_flat = _pick_tile(rows, (512, 384, 256, 128))

    q_w, kv_w, qi_w = N_HEADS * HEAD_DIM, N_KV_HEADS * HEAD_DIM, IDX_HEADS * IDX_DIM
    o = np.cumsum([0, q_w, kv_w, kv_w, qi_w, IDX_DIM, IDX_HEADS, D_MODEL, D_MODEL, 2 * D_MODEL, 3 * D_MODEL])
    pad_w = LANES - IDX_DIM - IDX_HEADS
    w_attn = jnp.concatenate([w_in[:, :, o[0]:o[6]], jnp.zeros((depth, D_MODEL, pad_w), w_in.dtype)],
                             axis=2).astype(BF16)
    w_rnn = w_in[:, :, o[6]:o[8]].astype(BF16)
    w_cnv = w_in[:, :, o[8]:o[9]].astype(BF16)
    w_gate = w_in[:, :, o[9]:o[10]].astype(BF16)
    wa_b, wx_b = rnn_wa.astype(BF16), rnn_wx.astype(BF16)
    w_oa, w_or, w_oc, w_ot = (a.astype(BF16) for a in (w_o_attn, w_o_rnn, w_o_conv, w_out))
    w_fg, w_fu, w_fd = (a.astype(BF16) for a in (w_ffn_gate, w_ffn_up, w_ffn_down))

    cos_a, sin_a = _rope_tables(tp, HEAD_DIM)
    cos_i, sin_i = _rope_tables(tp, IDX_DIM)
    tabs = (jnp.concatenate([cos_a, cos_a], axis=1), jnp.concatenate([-sin_a, sin_a], axis=1),
            jnp.concatenate([cos_i] * 4, axis=1), jnp.concatenate([-sin_i, sin_i] * 2, axis=1))

    h = jnp.concatenate([jnp.broadcast_to(meta[None].astype(x.dtype), (bsz, N_META, D_MODEL)), x,
                         jnp.zeros((bsz, tp - t_valid, D_MODEL), x.dtype)], axis=1)
    vec = lambda a: a.reshape(1, -1)

    for l in range(depth):
        h2 = h.reshape(rows, D_MODEL)
        q, k, vt, qi, ki, wi = _attn_proj(h2, vec(mix_norm_g[l]), w_attn[l], vec(q_norm_g[l]),
                                          vec(k_norm_g[l]), tabs, tm_seq, tp // tm_seq)
        seq3 = lambda a: a.reshape(bsz, tp, a.shape[-1])
        attn = _dsa_attention(seq3(q), seq3(qi), seq3(wi), seq3(ki), seq3(k),
                              vt.reshape(bsz, tp // tm_seq, N_KV_HEADS * HEAD_DIM, tm_seq),
                              t_valid, topk, tm_seq)
        attn = attn.reshape(rows, N_HEADS * HEAD_DIM)

        rnn = _rglru(h, vec(mix_norm_g[l]), w_rnn[l], rnn_conv_w[l], vec(rnn_conv_b[l]), wa_b[l],
                     vec(rnn_ba[l]), wx_b[l], vec(rnn_bx[l]), vec(rnn_lambda[l]), tm_seq)
        cnv = _conformer(h, vec(mix_norm_g[l]), w_cnv[l], conv_dw_w[l], vec(conv_dw_b[l]),
                         vec(conv_ln_g[l]), vec(conv_ln_b[l]), tm_seq)
        h2 = _merge(h2, attn, rnn.reshape(rows, D_MODEL), cnv.reshape(rows, D_MODEL), vec(mix_norm_g[l]),
                    w_gate[l], w_oa[l], w_or[l], w_oc[l], w_ot[l], tm_flat)
        h2 = _ffn(h2, vec(ffn_norm_g[l]), w_fg[l], w_fu[l], w_fd[l], tm_flat)
        h = h2.reshape(bsz, tp, D_MODEL)

    return h[:, N_META:t_valid]
```

```python
import functools

import jax
import jax.numpy as jnp
import numpy as np
from jax import lax
from jax.experimental import pallas as pl
from jax.experimental.pallas import tpu as pltpu

D_MODEL = 1024
CHUNK = 64
N_META = 16
N_HEADS = 8
N_KV_HEADS = 2
HEAD_DIM = 128
KV_GROUP = N_HEADS // N_KV_HEADS
IDX_HEADS = 8
IDX_DIM = 64
TOPK_MAX = 256
ROPE_THETA = 10000.0
RNN_BLOCKS = 8
RNN_BLOCK_DIM = D_MODEL // RNN_BLOCKS
RNN_CONV = 4
LRU_C = 8.0
CONV_KERNEL = 31
NORM_EPS = 1e-6

LANES = 128
SUBLANES = 8
PACKED_ROWS = 16
TQ = 128
VMEM_LIMIT = 56 * 1024 * 1024
LOG2E = 1.4426950408889634

I16_MIN = -(2 ** 15)
KEY_MASKED = -2139095041
NEG_BIG = -1e30

F32 = jnp.float32
BF16 = jnp.bfloat16


def _rms(x, g):
    return x * lax.rsqrt(jnp.mean(x * x, axis=-1, keepdims=True) + NORM_EPS) * g


def _expm1(y):
    u = jnp.exp(y)
    near = jnp.where(u == 1.0, y, (u - 1.0) * y / jnp.log(jnp.where(u == 1.0, 2.0, u)))
    return jnp.where(y > -0.5, near, u - 1.0)


def _const_spec(shape):
    nd = len(shape)
    return pl.BlockSpec(shape, lambda *_: (0,) * nd, pipeline_mode=pl.Buffered(1))


def _pick_tile(n, candidates):
    for c in candidates:
        if n % c == 0:
            return c
    raise ValueError(f"no tile for {n}")


def _attn_proj_kernel(x_ref, g_ref, w_ref, qg_ref, kg_ref, cosa_ref, sina_ref, cosi_ref, sini_ref,
                      q_ref, k_ref, vt_ref, qi_ref, ki_ref, wi_ref, *, idx_scale):
    n = _rms(x_ref[...], g_ref[...]).astype(BF16)
    cosa, sina = cosa_ref[...], sina_ref[...]
    cosi, sini = cosi_ref[...], sini_ref[...]
    lane = lax.broadcasted_iota(jnp.int32, cosi.shape, 1)
    low_half = (lane % IDX_DIM) < (IDX_DIM // 2)

    def rot_attn(a, g):
        a = _rms(a, g)
        return a * cosa + pltpu.roll(a, HEAD_DIM // 2, axis=1) * sina

    def rot_idx(a):
        partner = jnp.where(low_half, pltpu.roll(a, LANES - IDX_DIM // 2, axis=1),
                            pltpu.roll(a, IDX_DIM // 2, axis=1))
        return a * cosi + partner * sini

    q_w = N_HEADS * HEAD_DIM
    kv_w = N_KV_HEADS * HEAD_DIM
    qi_w = IDX_HEADS * IDX_DIM
    o_k, o_v, o_qi, o_ki = q_w, q_w + kv_w, q_w + 2 * kv_w, q_w + 2 * kv_w + qi_w

    q_scale = (HEAD_DIM ** -0.5) * LOG2E
    pq = jnp.dot(n, w_ref[:, 0:q_w], preferred_element_type=F32)
    for h in range(N_HEADS):
        sl = slice(h * HEAD_DIM, (h + 1) * HEAD_DIM)
        q_ref[:, sl] = (rot_attn(pq[:, sl], qg_ref[...]) * q_scale).astype(BF16)
    pk = jnp.dot(n, w_ref[:, o_k:o_qi], preferred_element_type=F32)
    for h in range(N_KV_HEADS):
        sl = slice(h * HEAD_DIM, (h + 1) * HEAD_DIM)
        k_ref[:, sl] = rot_attn(pk[:, sl], kg_ref[...]).astype(BF16)
        vt_ref[sl, :] = pk[:, kv_w + h * HEAD_DIM:kv_w + (h + 1) * HEAD_DIM].T.astype(BF16)
    pi = jnp.dot(n, w_ref[:, o_qi:], preferred_element_type=F32)
    for c in range(qi_w // LANES):
        sl = slice(c * LANES, (c + 1) * LANES)
        qi_ref[:, sl] = rot_idx(pi[:, sl]).astype(BF16)
    last = pi[:, qi_w:]
    ki = rot_idx(last)
    ki_ref[...] = jnp.where(lane < IDX_DIM, ki, pltpu.roll(ki, IDX_DIM, axis=1)).astype(BF16)
    wi_ref[...] = last * idx_scale


def _attn_proj(h2, g, w, qg, kg, tabs, tm, tiles_per_seq):
    rows = h2.shape[0]
    wcols = w.shape[1]
    row = lambda width: pl.BlockSpec((tm, width), lambda i: (i, 0))
    tab = pl.BlockSpec((tm, LANES), lambda i: (i % tiles_per_seq, 0))
    idx_scale = (IDX_HEADS ** -0.5) * (IDX_DIM ** -0.5)
    return pl.pallas_call(
        functools.partial(_attn_proj_kernel, idx_scale=idx_scale),
        grid=(rows // tm,),
        in_specs=[row(D_MODEL), _const_spec((1, D_MODEL)), _const_spec((D_MODEL, wcols)),
                  _const_spec((1, HEAD_DIM)), _const_spec((1, HEAD_DIM)), tab, tab, tab, tab],
        out_specs=[row(N_HEADS * HEAD_DIM), row(N_KV_HEADS * HEAD_DIM),
                   pl.BlockSpec((None, N_KV_HEADS * HEAD_DIM, tm), lambda i: (i, 0, 0)),
                   row(IDX_HEADS * IDX_DIM), row(LANES), row(LANES)],
        out_shape=[jax.ShapeDtypeStruct((rows, N_HEADS * HEAD_DIM), BF16),
                   jax.ShapeDtypeStruct((rows, N_KV_HEADS * HEAD_DIM), BF16),
                   jax.ShapeDtypeStruct((rows // tm, N_KV_HEADS * HEAD_DIM, tm), BF16),
                   jax.ShapeDtypeStruct((rows, IDX_HEADS * IDX_DIM), BF16),
                   jax.ShapeDtypeStruct((rows, LANES), BF16),
                   jax.ShapeDtypeStruct((rows, LANES), F32)],
        compiler_params=pltpu.CompilerParams(dimension_semantics=("arbitrary",),
                                             vmem_limit_bytes=VMEM_LIMIT),
        name="attn_proj",
    )(h2, g, w, qg, kg, *tabs)


def _dsa_kernel(q_ref, qi_ref, wi_ref, ki_ref, k_ref, vt_ref, out_ref,
                key_ref, hi_ref, lo_ref, bias_ref, qt_ref, qit_ref, acc_ref, m_ref, l_ref, sa_ref, sb_ref,
                *, n_valid, topk, tk):
    tq = TQ
    gw = KV_GROUP * tq
    t0 = pl.program_id(1) * tq
    tcol = t0 + lax.broadcasted_iota(jnp.int32, (1, tq), 1)
    kb = jnp.where(tcol < N_META, N_META,
                   N_META + CHUNK * (1 + jnp.right_shift(tcol - N_META, 6)))
    kb = jnp.where(tcol < n_valid, jnp.minimum(kb, n_valid), n_valid)
    kext = jnp.minimum(n_valid, N_META + CHUNK * (1 + (t0 + tq - 1 - N_META) // CHUNK))
    n_tiles = (kext + tk - 1) // tk

    for h in range(N_HEADS):
        qh = q_ref[:, h * HEAD_DIM:(h + 1) * HEAD_DIM].astype(F32).T
        qt_ref[h // KV_GROUP, :, (h % KV_GROUP) * tq:(h % KV_GROUP + 1) * tq] = qh.astype(BF16)
    row = lax.broadcasted_iota(jnp.int32, (LANES, tq), 0)
    for hp in range(IDX_HEADS // 2):
        pair = qi_ref[:, hp * LANES:(hp + 1) * LANES].astype(F32).T
        qit_ref[:, (2 * hp) * tq:(2 * hp + 1) * tq] = jnp.where(row < IDX_DIM, pair, 0.0).astype(BF16)
        qit_ref[:, (2 * hp + 1) * tq:(2 * hp + 2) * tq] = jnp.where(row >= IDX_DIM, pair, 0.0).astype(BF16)
    w = wi_ref[...].T[IDX_DIM:IDX_DIM + IDX_HEADS, :]

    def score_tile(i, carry):
        r0 = pl.multiple_of(i * tk, tk)
        kt = ki_ref[pl.ds(r0, tk), :]
        acc = jnp.zeros((tk, tq), F32)
        for hp in range(IDX_HEADS // 2):
            lg = jnp.dot(kt, qit_ref[:, hp * 2 * tq:(hp + 1) * 2 * tq],
                         preferred_element_type=F32)
            for hh in range(2):
                h = 2 * hp + hh
                acc = acc + jnp.maximum(lg[:, hh * tq:(hh + 1) * tq], 0.0) * w[h:h + 1, :]
        acc = jnp.where(acc == 0.0, 0.0, acc)
        pos = r0 + lax.broadcasted_iota(jnp.int32, (tk, tq), 0)
        bits = pltpu.bitcast(acc, jnp.int32)
        key = jnp.where(bits >= 0, bits, bits ^ 0x7FFFFFFF)
        key = jnp.where(pos < kb, key, KEY_MASKED)
        key_ref[pl.ds(r0, tk), :] = key
        hi_ref[pl.ds(r0, tk), :] = lax.shift_right_arithmetic(key, 16).astype(jnp.int16)
        return carry

    lax.fori_loop(0, n_tiles, score_tile, 0)

    def search16(ref, kth):
        def count_ge(cand):
            c16 = cand.astype(jnp.int16)

            def body(i, acc):
                r0 = pl.multiple_of(i * tk, tk)
                ind = jnp.where(ref[pl.ds(r0, tk), :] >= c16, jnp.bfloat16(1), jnp.bfloat16(0))
                parts = [ind[r * PACKED_ROWS:(r + 1) * PACKED_ROWS] for r in range(tk // PACKED_ROWS)]
                while len(parts) > 1:
                    nxt = [parts[r] + parts[r + 1] for r in range(0, len(parts) - 1, 2)]
                    parts = nxt + ([parts[-1]] if len(parts) % 2 else [])
                return acc + parts[0].astype(F32)

            acc = lax.fori_loop(0, n_tiles, body, jnp.zeros((PACKED_ROWS, tq), F32))
            return acc.sum(axis=0, keepdims=True)

        def step(it, lo):
            cand = lo + jnp.left_shift(jnp.int32(1), 15 - it)
            return jnp.where(count_ge(cand) >= kth, cand, lo)

        return lax.fori_loop(0, 16, step, jnp.full((1, tq), I16_MIN, jnp.int32))

    tau_hi = search16(hi_ref, jnp.full((1, tq), float(topk), F32))

    def low_tile(i, acc):
        r0 = pl.multiple_of(i * tk, tk)
        blk = key_ref[pl.ds(r0, tk), :]
        hi = lax.shift_right_arithmetic(blk, 16)
        low = jnp.where(hi == tau_hi, (blk & 0xFFFF) + I16_MIN, I16_MIN)
        lo_ref[pl.ds(r0, tk), :] = low.astype(jnp.int16)
        above = jnp.where(hi > tau_hi, 1, 0).astype(jnp.int32)
        return acc + above.reshape(tk // SUBLANES, SUBLANES, tq).sum(axis=0)

    c_above = lax.fori_loop(0, n_tiles, low_tile, jnp.zeros((SUBLANES, tq), jnp.int32))
    c_above = c_above.sum(axis=0, keepdims=True)
    tau_lo = search16(lo_ref, (topk - c_above).astype(F32))
    tau = tau_hi * 65536 + (tau_lo - I16_MIN)

    def count(pred):
        def body(i, acc):
            r0 = pl.multiple_of(i * tk, tk)
            ind = jnp.where(pred(key_ref[pl.ds(r0, tk), :]), 1, 0).astype(jnp.int32)
            return acc + ind.reshape(tk // SUBLANES, SUBLANES, tq).sum(axis=0)
        acc = lax.fori_loop(0, n_tiles, body, jnp.zeros((SUBLANES, tq), jnp.int32))
        return acc.sum(axis=0, keepdims=True)

    c_gt = count(lambda blk: blk > tau)
    c_ge = count(lambda blk: blk >= tau)
    cut = jnp.max(c_ge - topk) > 0

    @pl.when(jnp.logical_not(cut))
    def _():
        def bias_tile(i, carry):
            r0 = pl.multiple_of(i * tk, tk)
            pos = r0 + lax.broadcasted_iota(jnp.int32, (tk, tq), 0)
            keep = (key_ref[pl.ds(r0, tk), :] >= tau) & (pos < kb)
            bias_ref[pl.ds(r0, tk), :] = jnp.where(keep, 0.0, NEG_BIG)
            return carry

        lax.fori_loop(0, n_tiles, bias_tile, 0)

    @pl.when(cut)
    def _():
        rank = (topk - c_gt).astype(F32)
        tri = jnp.where(lax.broadcasted_iota(jnp.int32, (tk, tk), 0)
                        >= lax.broadcasted_iota(jnp.int32, (tk, tk), 1), 1.0, 0.0).astype(BF16)

        def bias_tile(i, seen):
            r0 = pl.multiple_of(i * tk, tk)
            blk = key_ref[pl.ds(r0, tk), :]
            pos = r0 + lax.broadcasted_iota(jnp.int32, (tk, tq), 0)
            eq = blk == tau
            nth = seen + jnp.dot(tri, jnp.where(eq, 1.0, 0.0).astype(BF16), preferred_element_type=F32)
            keep = ((blk > tau) | (eq & (nth <= rank))) & (pos < kb)
            bias_ref[pl.ds(r0, tk), :] = jnp.where(keep, 0.0, NEG_BIG)
            return nth[tk - 1:tk, :]

        lax.fori_loop(0, n_tiles, bias_tile, jnp.zeros((1, tq), F32))

    acc_ref[...] = jnp.zeros_like(acc_ref)
    m_ref[...] = jnp.full(m_ref.shape, NEG_BIG, F32)
    l_ref[...] = jnp.zeros_like(l_ref)

    def scores(t, s_ref):
        r0 = pl.multiple_of(t * tk, tk)
        for n in range(N_KV_HEADS):
            kt = k_ref[pl.ds(r0, tk), n * HEAD_DIM:(n + 1) * HEAD_DIM]
            s_ref[n] = jnp.dot(kt, qt_ref[n], preferred_element_type=F32)

    def softmax_pv(t, s_ref):
        r0 = pl.multiple_of(t * tk, tk)
        bt = bias_ref[pl.ds(r0, tk), :]
        bt = jnp.concatenate([bt] * KV_GROUP, axis=1)
        for n in range(N_KV_HEADS):
            s = s_ref[n] + bt
            m = m_ref[n]
            m_new = jnp.maximum(m, jnp.max(s, axis=0, keepdims=True))
            alpha = jnp.exp2(m - m_new)
            p = jnp.exp2(s - m_new)
            l_ref[n] = alpha * l_ref[n] + jnp.sum(p, axis=0, keepdims=True)
            m_ref[n] = m_new
            vt = vt_ref[t, n * HEAD_DIM:(n + 1) * HEAD_DIM, :]
            acc_ref[n] = alpha * acc_ref[n] + jnp.dot(vt, p.astype(BF16), preferred_element_type=F32)

    last_tile = k_ref.shape[0] // tk - 1
    scores(0, sa_ref)

    def tile_pair(i, carry):
        t = 2 * i
        scores(t + 1, sb_ref)
        softmax_pv(t, sa_ref)
        scores(jnp.minimum(t + 2, last_tile), sa_ref)
        softmax_pv(t + 1, sb_ref)
        return carry

    lax.fori_loop(0, n_tiles // 2, tile_pair, 0)

    @pl.when(n_tiles % 2 == 1)
    def _():
        softmax_pv(n_tiles - 1, sa_ref)

    for h in range(N_HEADS):
        n, g = h // KV_GROUP, h % KV_GROUP
        inv_l = 1.0 / l_ref[n][:, g * tq:(g + 1) * tq]
        oh = acc_ref[n, :, g * tq:(g + 1) * tq] * inv_l
        out_ref[:, h * HEAD_DIM:(h + 1) * HEAD_DIM] = oh.T.astype(BF16)


def _dsa_attention(q, qi, wi, ki, k, vt, n_valid, topk, tk):
    bsz, tp, _ = q.shape
    gw = KV_GROUP * TQ
    qrow = lambda width: pl.BlockSpec((None, TQ, width), lambda b, j: (b, j, 0))
    seq = lambda width: pl.BlockSpec((None, tp, width), lambda b, j: (b, 0, 0))
    return pl.pallas_call(
        functools.partial(_dsa_kernel, n_valid=n_valid, topk=topk, tk=tk),
        grid=(bsz, tp // TQ),
        in_specs=[qrow(N_HEADS * HEAD_DIM), qrow(IDX_HEADS * IDX_DIM), qrow(LANES), seq(LANES),
                  seq(N_KV_HEADS * HEAD_DIM),
                  pl.BlockSpec((None, tp // tk, N_KV_HEADS * HEAD_DIM, tk), lambda b, j: (b, 0, 0, 0))],
        out_specs=qrow(N_HEADS * HEAD_DIM),
        out_shape=jax.ShapeDtypeStruct((bsz, tp, N_HEADS * HEAD_DIM), BF16),
        scratch_shapes=[pltpu.VMEM((tp, TQ), jnp.int32), pltpu.VMEM((tp, TQ), jnp.int16),
                        pltpu.VMEM((tp, TQ), jnp.int16), pltpu.VMEM((tp, TQ), F32),
                        pltpu.VMEM((N_KV_HEADS, HEAD_DIM, gw), BF16),
                        pltpu.VMEM((LANES, IDX_HEADS * TQ), BF16),
                        pltpu.VMEM((N_KV_HEADS, HEAD_DIM, gw), F32),
                        pltpu.VMEM((N_KV_HEADS, 1, gw), F32), pltpu.VMEM((N_KV_HEADS, 1, gw), F32),
                        pltpu.VMEM((N_KV_HEADS, tk, gw), F32), pltpu.VMEM((N_KV_HEADS, tk, gw), F32)],
        compiler_params=pltpu.CompilerParams(dimension_semantics=("arbitrary", "arbitrary"),
                                             vmem_limit_bytes=VMEM_LIMIT),
        name="dsa_attention",
    )(q, qi, wi, ki, k, vt)


def _rglru_kernel(h_ref, g_ref, w_ref, cw_ref, cb_ref, wa_ref, ba_ref, wx_ref, bx_ref, lam_ref,
                  out_ref, xext_ref, a_ref, b_ref, state_ref):
    tm = h_ref.shape[0]
    hist = SUBLANES

    @pl.when(pl.program_id(1) == 0)
    def _():
        xext_ref[0:hist, :] = jnp.zeros((hist, D_MODEL), F32)
        state_ref[...] = jnp.zeros_like(state_ref)

    n = _rms(h_ref[...], g_ref[...]).astype(BF16)
    xext_ref[hist:hist + tm, :] = jnp.dot(n, w_ref[:, 0:D_MODEL], preferred_element_type=F32)
    u = cb_ref[...] + cw_ref[0:1, :] * xext_ref[hist - RNN_CONV + 1:hist - RNN_CONV + 1 + tm, :]
    for kk in range(1, RNN_CONV):
        o = hist - RNN_CONV + 1 + kk
        u = u + cw_ref[kk:kk + 1, :] * xext_ref[o:o + tm, :]
    xext_ref[0:hist, :] = xext_ref[tm:tm + hist, :]

    ub = u.astype(BF16)
    rs, is_ = [], []
    for blk in range(RNN_BLOCKS):
        sl = slice(blk * RNN_BLOCK_DIM, (blk + 1) * RNN_BLOCK_DIM)
        rs.append(jnp.dot(ub[:, sl], wa_ref[blk], preferred_element_type=F32))
        is_.append(jnp.dot(ub[:, sl], wx_ref[blk], preferred_element_type=F32))
    r = jax.nn.sigmoid(jnp.concatenate(rs, axis=1) + ba_ref[...])
    ig = jax.nn.sigmoid(jnp.concatenate(is_, axis=1) + bx_ref[...])
    nl = -lam_ref[...]
    softplus = jnp.maximum(nl, 0.0) + jnp.log1p(jnp.exp(-jnp.abs(nl)))
    log_a = -LRU_C * r * softplus
    a_all = jnp.exp(log_a)
    b_all = jnp.sqrt(-_expm1(2.0 * log_a)) * (ig * u)
    nchunk = D_MODEL // LANES
    for c in range(nchunk):
        a_ref[c] = a_all[:, c * LANES:(c + 1) * LANES]
        b_ref[c] = b_all[:, c * LANES:(c + 1) * LANES]

    seg = tm // SCAN_SEGMENTS
    groups = SCAN_SEGMENTS // SUBLANES

    def step(i, carry):
        out = []
        for v in range(groups):
            prod, loc = carry[2 * v], carry[2 * v + 1]
            rows = pl.ds(v * SUBLANES * seg + i, SUBLANES, stride=seg)
            a = a_ref[:, rows, :]
            prod = a * prod
            loc = a * loc + b_ref[:, rows, :]
            a_ref[:, rows, :] = prod
            b_ref[:, rows, :] = loc
            out += [prod, loc]
        return tuple(out)

    init = (jnp.ones((nchunk, SUBLANES, LANES), F32), jnp.zeros((nchunk, SUBLANES, LANES), F32)) * groups
    ends = lax.fori_loop(0, seg, step, init)
    gel = jax.nn.gelu(jnp.dot(n, w_ref[:, D_MODEL:], preferred_element_type=F32))
    hcur = state_ref[...]
    for s in range(SCAN_SEGMENTS):
        rs = slice(s * seg, (s + 1) * seg)
        hs = a_ref[:, rs, :] * hcur + b_ref[:, rs, :]
        for c in range(nchunk):
            cs = slice(c * LANES, (c + 1) * LANES)
            out_ref[rs, cs] = (hs[c] * gel[rs, cs]).astype(BF16)
        v, r = s // SUBLANES, s % SUBLANES
        hcur = ends[2 * v][:, r:r + 1, :] * hcur + ends[2 * v + 1][:, r:r + 1, :]
    state_ref[...] = hcur


def _rglru(h3, g, w, cw, cb, wa, ba, wx, bx, lam, tm):
    bsz, tp, _ = h3.shape
    row = pl.BlockSpec((None, tm, D_MODEL), lambda b, t: (b, t, 0))
    vec = _const_spec((1, D_MODEL))
    blkw = _const_spec((RNN_BLOCKS, RNN_BLOCK_DIM, RNN_BLOCK_DIM))
    return pl.pallas_call(
        _rglru_kernel,
        grid=(bsz, tp // tm),
        in_specs=[row, vec, _const_spec((D_MODEL, 2 * D_MODEL)), _const_spec((RNN_CONV, D_MODEL)), vec,
                  blkw, vec, blkw, vec, vec],
        out_specs=row,
        out_shape=jax.ShapeDtypeStruct((bsz, tp, D_MODEL), BF16),
        scratch_shapes=[pltpu.VMEM((tm + SUBLANES, D_MODEL), F32),
                        pltpu.VMEM((D_MODEL // LANES, tm, LANES), F32),
                        pltpu.VMEM((D_MODEL // LANES, tm, LANES), F32),
                        pltpu.VMEM((D_MODEL // LANES, 1, LANES), F32)],
        compiler_params=pltpu.CompilerParams(dimension_semantics=("arbitrary", "arbitrary"),
                                             vmem_limit_bytes=VMEM_LIMIT),
        name="rglru",
    )(h3, g, w, cw, cb, wa, ba, wx, bx, lam)


SCAN_SEGMENTS = 16
CONV_HIST = 32
CONV_ROWS = 128


def _conformer_kernel(h_ref, g_ref, w_ref, dw_ref, db_ref, lg_ref, lb_ref, out_ref, xs_ref, y_ref):
    tm = h_ref.shape[0]

    @pl.when(pl.program_id(1) == 0)
    def _():
        xs_ref[0, 0:CONV_HIST, :] = jnp.zeros((CONV_HIST, D_MODEL), F32)

    n = _rms(h_ref[...], g_ref[...]).astype(BF16)
    a = jnp.dot(n, w_ref[:, 0:D_MODEL], preferred_element_type=F32)
    gate = jnp.dot(n, w_ref[:, D_MODEL:], preferred_element_type=F32)
    xs_ref[0, CONV_HIST:CONV_HIST + tm, :] = a * jax.nn.sigmoid(gate)
    span = tm + CONV_HIST - SUBLANES
    for s in range(1, SUBLANES):
        xs_ref[s, 0:span, :] = xs_ref[0, s:s + span, :]

    first = CONV_HIST - CONV_KERNEL + 1
    row_chunks = range(0, tm, CONV_ROWS)
    for c0 in range(0, D_MODEL, LANES):
        cs = slice(c0, c0 + LANES)
        accs = [jnp.broadcast_to(db_ref[:, cs], (CONV_ROWS, LANES)) for _ in row_chunks]
        for kk in range(CONV_KERNEL):
            shift = (first + kk) % SUBLANES
            base = first + kk - shift
            wk = dw_ref[kk:kk + 1, cs]
            for ri, r0 in enumerate(row_chunks):
                accs[ri] = accs[ri] + wk * xs_ref[shift, base + r0:base + r0 + CONV_ROWS, cs]
        for ri, r0 in enumerate(row_chunks):
            y_ref[r0:r0 + CONV_ROWS, cs] = accs[ri]
    xs_ref[0, 0:CONV_HIST, :] = xs_ref[0, tm:tm + CONV_HIST, :]

    y = y_ref[...]
    mu = jnp.mean(y, axis=-1, keepdims=True)
    yc = y - mu
    z = yc * lax.rsqrt(jnp.mean(yc * yc, axis=-1, keepdims=True) + NORM_EPS) * lg_ref[...] + lb_ref[...]
    out_ref[...] = jax.nn.silu(z).astype(BF16)


def _conformer(h3, g, w, dw, db, lg, lb, tm):
    bsz, tp, _ = h3.shape
    row = pl.BlockSpec((None, tm, D_MODEL), lambda b, t: (b, t, 0))
    vec = _const_spec((1, D_MODEL))
    return pl.pallas_call(
        _conformer_kernel,
        grid=(bsz, tp // tm),
        in_specs=[row, vec, _const_spec((D_MODEL, 2 * D_MODEL)), _const_spec((CONV_KERNEL, D_MODEL)),
                  vec, vec, vec],
        out_specs=row,
        out_shape=jax.ShapeDtypeStruct((bsz, tp, D_MODEL), BF16),
        scratch_shapes=[pltpu.VMEM((SUBLANES, tm + CONV_HIST, D_MODEL), F32), pltpu.VMEM((tm, D_MODEL), F32)],
        compiler_params=pltpu.CompilerParams(dimension_semantics=("arbitrary", "arbitrary"),
                                             vmem_limit_bytes=VMEM_LIMIT),
        name="conformer",
    )(h3, g, w, dw, db, lg, lb)


def _merge_kernel(h_ref, at_ref, rn_ref, cv_ref, g_ref, wg_ref, wa_ref, wr_ref, wc_ref, wo_ref, out_ref):
    h = h_ref[...]
    n = _rms(h, g_ref[...]).astype(BF16)
    merged = None
    for i, (src, wref) in enumerate(((at_ref, wa_ref), (rn_ref, wr_ref), (cv_ref, wc_ref))):
        gate = jax.nn.sigmoid(jnp.dot(n, wg_ref[:, i * D_MODEL:(i + 1) * D_MODEL],
                                      preferred_element_type=F32))
        term = gate * jnp.dot(src[...], wref[...], preferred_element_type=F32)
        merged = term if merged is None else merged + term
    out_ref[...] = h + jnp.dot(merged.astype(BF16), wo_ref[...], preferred_element_type=F32)


def _merge(h2, attn, rnn, cnv, g, wg, wa, wr, wc, wo, tm):
    rows = h2.shape[0]
    row = pl.BlockSpec((tm, D_MODEL), lambda i: (i, 0))
    sq = _const_spec((D_MODEL, D_MODEL))
    return pl.pallas_call(
        _merge_kernel,
        grid=(rows // tm,),
        in_specs=[row, row, row, row, _const_spec((1, D_MODEL)), _const_spec((D_MODEL, 3 * D_MODEL)),
                  sq, sq, sq, sq],
        out_specs=row,
        out_shape=jax.ShapeDtypeStruct((rows, D_MODEL), F32),
        compiler_params=pltpu.CompilerParams(dimension_semantics=("arbitrary",),
                                             vmem_limit_bytes=VMEM_LIMIT),
        name="merge",
    )(h2, attn, rnn, cnv, g, wg, wa, wr, wc, wo)


def _ffn_kernel(h_ref, g_ref, wg_ref, wu_ref, wd_ref, out_ref):
    h = h_ref[...]
    f = _rms(h, g_ref[...]).astype(BF16)
    gate = jnp.dot(f, wg_ref[...], preferred_element_type=F32)
    up = jnp.dot(f, wu_ref[...], preferred_element_type=F32)
    act = (jax.nn.silu(gate) * up).astype(BF16)
    out_ref[...] = h + jnp.dot(act, wd_ref[...], preferred_element_type=F32)


def _ffn(h2, g, wg, wu, wd, tm):
    rows = h2.shape[0]
    dff = wg.shape[1]
    row = pl.BlockSpec((tm, D_MODEL), lambda i: (i, 0))
    return pl.pallas_call(
        _ffn_kernel,
        grid=(rows // tm,),
        in_specs=[row, _const_spec((1, D_MODEL)), _const_spec((D_MODEL, dff)), _const_spec((D_MODEL, dff)),
                  _const_spec((dff, D_MODEL))],
        out_specs=row,
        out_shape=jax.ShapeDtypeStruct((rows, D_MODEL), F32),
        compiler_params=pltpu.CompilerParams(dimension_semantics=("arbitrary",),
                                             vmem_limit_bytes=VMEM_LIMIT),
        name="ffn",
    )(h2, g, wg, wu, wd)


def _rope_tables(n, dim):
    inv = ROPE_THETA ** (-jnp.arange(0, dim, 2, dtype=F32) / dim)
    ang = jnp.arange(n, dtype=F32)[:, None] * inv[None, :]
    return jnp.cos(ang), jnp.sin(ang)


def kernel(x, meta, mix_norm_g, w_in, q_norm_g, k_norm_g, rnn_conv_w, rnn_conv_b, rnn_wa, rnn_ba, rnn_wx, rnn_bx, rnn_lambda, conv_dw_w, conv_dw_b, conv_ln_g, conv_ln_b, w_o_attn, w_o_rnn, w_o_conv, w_out, ffn_norm_g, w_ffn_gate, w_ffn_up, w_ffn_down):
    bsz, seq, _ = x.shape
    depth = w_in.shape[0]
    t_valid = seq + N_META
    tp = -(-t_valid // TQ) * TQ
    topk = min(TOPK_MAX, seq // 4)
    rows = bsz * tp
    tm_seq = _pick_tile(tp, (384, 256, 128))
    tm_flat = _pick_tile(rows, (512, 384, 256, 128))

    q_w, kv_w, qi_w = N_HEADS * HEAD_DIM, N_KV_HEADS * HEAD_DIM, IDX_HEADS * IDX_DIM
    o = np.cumsum([0, q_w, kv_w, kv_w, qi_w, IDX_DIM, IDX_HEADS, D_MODEL, D_MODEL, 2 * D_MODEL, 3 * D_MODEL])
    pad_w = LANES - IDX_DIM - IDX_HEADS
    w_attn = jnp.concatenate([w_in[:, :, o[0]:o[6]], jnp.zeros((depth, D_MODEL, pad_w), w_in.dtype)],
                             axis=2).astype(BF16)
    w_rnn = w_in[:, :, o[6]:o[8]].astype(BF16)
    w_cnv = w_in[:, :, o[8]:o[9]].astype(BF16)
    w_gate = w_in[:, :, o[9]:o[10]].astype(BF16)
    wa_b, wx_b = rnn_wa.astype(BF16), rnn_wx.astype(BF16)
    w_oa, w_or, w_oc, w_ot = (a.astype(BF16) for a in (w_o_attn, w_o_rnn, w_o_conv, w_out))
    w_fg, w_fu, w_fd = (a.astype(BF16) for a in (w_ffn_gate, w_ffn_up, w_ffn_down))

    cos_a, sin_a = _rope_tables(tp, HEAD_DIM)
    cos_i, sin_i = _rope_tables(tp, IDX_DIM)
    tabs = (jnp.concatenate([cos_a, cos_a], axis=1), jnp.concatenate([-sin_a, sin_a], axis=1),
            jnp.concatenate([cos_i] * 4, axis=1), jnp.concatenate([-sin_i, sin_i] * 2, axis=1))

    h = jnp.concatenate([jnp.broadcast_to(meta[None].astype(x.dtype), (bsz, N_META, D_MODEL)), x,
                         jnp.zeros((bsz, tp - t_valid, D_MODEL), x.dtype)], axis=1)
    vec = lambda a: a.reshape(1, -1)

    for l in range(depth):
        h2 = h.reshape(rows, D_MODEL)
        q, k, vt, qi, ki, wi = _attn_proj(h2, vec(mix_norm_g[l]), w_attn[l], vec(q_norm_g[l]),
                                          vec(k_norm_g[l]), tabs, tm_seq, tp // tm_seq)
        seq3 = lambda a: a.reshape(bsz, tp, a.shape[-1])
        attn = _dsa_attention(seq3(q), seq3(qi), seq3(wi), seq3(ki), seq3(k),
                              vt.reshape(bsz, tp // tm_seq, N_KV_HEADS * HEAD_DIM, tm_seq),
                              t_valid, topk, tm_seq)
        attn = attn.reshape(rows, N_HEADS * HEAD_DIM)

        rnn = _rglru(h, vec(mix_norm_g[l]), w_rnn[l], rnn_conv_w[l], vec(rnn_conv_b[l]), wa_b[l],
                     vec(rnn_ba[l]), wx_b[l], vec(rnn_bx[l]), vec(rnn_lambda[l]), tm_seq)
        cnv = _conformer(h, vec(mix_norm_g[l]), w_cnv[l], conv_dw_w[l], vec(conv_dw_b[l]),
                         vec(conv_ln_g[l]), vec(conv_ln_b[l]), tm_seq)
        h2 = _merge(h2, attn, rnn.reshape(rows, D_MODEL), cnv.reshape(rows, D_MODEL), vec(mix_norm_g[l]),
                    w_gate[l], w_oa[l], w_or[l], w_oc[l], w_ot[l], tm_flat)
        h2 = _ffn(h2, vec(ffn_norm_g[l]), w_fg[l], w_fu[l], w_fd[l], tm_flat)
        h = h2.reshape(bsz, tp, D_MODEL)

    return h[:, N_META:t_valid]
```

```python
import functools

import jax
import jax.numpy as jnp
import numpy as np
from jax import lax
from jax.experimental import pallas as pl
from jax.experimental.pallas import tpu as pltpu

D_MODEL = 1024
CHUNK = 64
N_META = 16
N_HEADS = 8
N_KV_HEADS = 2
HEAD_DIM = 128
KV_GROUP = N_HEADS // N_KV_HEADS
IDX_HEADS = 8
IDX_DIM = 64
TOPK_MAX = 256
ROPE_THETA = 10000.0
RNN_BLOCKS = 8
RNN_BLOCK_DIM = D_MODEL // RNN_BLOCKS
RNN_CONV = 4
LRU_C = 8.0
CONV_KERNEL = 31
NORM_EPS = 1e-6

LANES = 128
SUBLANES = 8
PACKED_ROWS = 16
TQ = 256
VMEM_LIMIT = 56 * 1024 * 1024
LOG2E = 1.4426950408889634

I16_MIN = -(2 ** 15)
KEY_MASKED = -2139095041
NEG_BIG = -1e30

F32 = jnp.float32
BF16 = jnp.bfloat16


def _rms(x, g):
    return x * lax.rsqrt(jnp.mean(x * x, axis=-1, keepdims=True) + NORM_EPS) * g


def _expm1(y):
    u = jnp.exp(y)
    near = jnp.where(u == 1.0, y, (u - 1.0) * y / jnp.log(jnp.where(u == 1.0, 2.0, u)))
    return jnp.where(y > -0.5, near, u - 1.0)


def _const_spec(shape):
    nd = len(shape)
    return pl.BlockSpec(shape, lambda *_: (0,) * nd, pipeline_mode=pl.Buffered(1))


def _pick_tile(n, candidates):
    for c in candidates:
        if n % c == 0:
            return c
    raise ValueError(f"no tile for {n}")


def _attn_proj_kernel(x_ref, g_ref, w_ref, qg_ref, kg_ref, cosa_ref, sina_ref, cosi_ref, sini_ref,
                      q_ref, k_ref, vt_ref, qi_ref, ki_ref, wi_ref, *, idx_scale):
    n = _rms(x_ref[...], g_ref[...]).astype(BF16)
    cosa, sina = cosa_ref[...], sina_ref[...]
    cosi, sini = cosi_ref[...], sini_ref[...]
    lane = lax.broadcasted_iota(jnp.int32, cosi.shape, 1)
    low_half = (lane % IDX_DIM) < (IDX_DIM // 2)

    def rot_attn(a, g):
        a = _rms(a, g)
        return a * cosa + pltpu.roll(a, HEAD_DIM // 2, axis=1) * sina

    def rot_idx(a):
        partner = jnp.where(low_half, pltpu.roll(a, LANES - IDX_DIM // 2, axis=1),
                            pltpu.roll(a, IDX_DIM // 2, axis=1))
        return a * cosi + partner * sini

    q_w = N_HEADS * HEAD_DIM
    kv_w = N_KV_HEADS * HEAD_DIM
    qi_w = IDX_HEADS * IDX_DIM
    o_k, o_v, o_qi, o_ki = q_w, q_w + kv_w, q_w + 2 * kv_w, q_w + 2 * kv_w + qi_w

    q_scale = (HEAD_DIM ** -0.5) * LOG2E
    pq = jnp.dot(n, w_ref[:, 0:q_w], preferred_element_type=F32)
    for h in range(N_HEADS):
        sl = slice(h * HEAD_DIM, (h + 1) * HEAD_DIM)
        q_ref[:, sl] = (rot_attn(pq[:, sl], qg_ref[...]) * q_scale).astype(BF16)
    pk = jnp.dot(n, w_ref[:, o_k:o_qi], preferred_element_type=F32)
    for h in range(N_KV_HEADS):
        sl = slice(h * HEAD_DIM, (h + 1) * HEAD_DIM)
        k_ref[:, sl] = rot_attn(pk[:, sl], kg_ref[...]).astype(BF16)
        vt_ref[sl, :] = pk[:, kv_w + h * HEAD_DIM:kv_w + (h + 1) * HEAD_DIM].T.astype(BF16)
    pi = jnp.dot(n, w_ref[:, o_qi:], preferred_element_type=F32)
    for c in range(qi_w // LANES):
        sl = slice(c * LANES, (c + 1) * LANES)
        qi_ref[:, sl] = rot_idx(pi[:, sl]).astype(BF16)
    last = pi[:, qi_w:]
    ki = rot_idx(last)
    ki_ref[...] = jnp.where(lane < IDX_DIM, ki, pltpu.roll(ki, IDX_DIM, axis=1)).astype(BF16)
    wi_ref[...] = last * idx_scale


def _attn_proj(h2, g, w, qg, kg, tabs, tm, tiles_per_seq):
    rows = h2.shape[0]
    wcols = w.shape[1]
    row = lambda width: pl.BlockSpec((tm, width), lambda i: (i, 0))
    tab = pl.BlockSpec((tm, LANES), lambda i: (i % tiles_per_seq, 0))
    idx_scale = (IDX_HEADS ** -0.5) * (IDX_DIM ** -0.5)
    return pl.pallas_call(
        functools.partial(_attn_proj_kernel, idx_scale=idx_scale),
        grid=(rows // tm,),
        in_specs=[row(D_MODEL), _const_spec((1, D_MODEL)), _const_spec((D_MODEL, wcols)),
                  _const_spec((1, HEAD_DIM)), _const_spec((1, HEAD_DIM)), tab, tab, tab, tab],
        out_specs=[row(N_HEADS * HEAD_DIM), row(N_KV_HEADS * HEAD_DIM),
                   pl.BlockSpec((None, N_KV_HEADS * HEAD_DIM, tm), lambda i: (i, 0, 0)),
                   row(IDX_HEADS * IDX_DIM), row(LANES), row(LANES)],
        out_shape=[jax.ShapeDtypeStruct((rows, N_HEADS * HEAD_DIM), BF16),
                   jax.ShapeDtypeStruct((rows, N_KV_HEADS * HEAD_DIM), BF16),
                   jax.ShapeDtypeStruct((rows // tm, N_KV_HEADS * HEAD_DIM, tm), BF16),
                   jax.ShapeDtypeStruct((rows, IDX_HEADS * IDX_DIM), BF16),
                   jax.ShapeDtypeStruct((rows, LANES), BF16),
                   jax.ShapeDtypeStruct((rows, LANES), F32)],
        compiler_params=pltpu.CompilerParams(dimension_semantics=("arbitrary",),
                                             vmem_limit_bytes=VMEM_LIMIT),
        name="attn_proj",
    )(h2, g, w, qg, kg, *tabs)


def _dsa_kernel(q_ref, qi_ref, wi_ref, ki_ref, k_ref, vt_ref, out_ref,
                key_ref, hi_ref, lo_ref, bias_ref, qt_ref, qit_ref, acc_ref, m_ref, l_ref, sa_ref, sb_ref,
                *, n_valid, topk, tk):
    tq = TQ
    gw = KV_GROUP * tq
    t0 = pl.program_id(1) * tq
    tcol = t0 + lax.broadcasted_iota(jnp.int32, (1, tq), 1)
    kb = jnp.where(tcol < N_META, N_META,
                   N_META + CHUNK * (1 + jnp.right_shift(tcol - N_META, 6)))
    kb = jnp.where(tcol < n_valid, jnp.minimum(kb, n_valid), n_valid)
    kext = jnp.minimum(n_valid, N_META + CHUNK * (1 + (t0 + tq - 1 - N_META) // CHUNK))
    n_tiles = (kext + tk - 1) // tk

    for h in range(N_HEADS):
        qh = q_ref[:, h * HEAD_DIM:(h + 1) * HEAD_DIM].astype(F32).T
        qt_ref[h // KV_GROUP, :, (h % KV_GROUP) * tq:(h % KV_GROUP + 1) * tq] = qh.astype(BF16)
    row = lax.broadcasted_iota(jnp.int32, (LANES, tq), 0)
    for hp in range(IDX_HEADS // 2):
        pair = qi_ref[:, hp * LANES:(hp + 1) * LANES].astype(F32).T
        qit_ref[:, (2 * hp) * tq:(2 * hp + 1) * tq] = jnp.where(row < IDX_DIM, pair, 0.0).astype(BF16)
        qit_ref[:, (2 * hp + 1) * tq:(2 * hp + 2) * tq] = jnp.where(row >= IDX_DIM, pair, 0.0).astype(BF16)
    w = wi_ref[...].T[IDX_DIM:IDX_DIM + IDX_HEADS, :]

    def score_tile(i, carry):
        r0 = pl.multiple_of(i * tk, tk)
        kt = ki_ref[pl.ds(r0, tk), :]
        acc = jnp.zeros((tk, tq), F32)
        for hp in range(IDX_HEADS // 2):
            lg = jnp.dot(kt, qit_ref[:, hp * 2 * tq:(hp + 1) * 2 * tq],
                         preferred_element_type=F32)
            for hh in range(2):
                h = 2 * hp + hh
                acc = acc + jnp.maximum(lg[:, hh * tq:(hh + 1) * tq], 0.0) * w[h:h + 1, :]
        acc = jnp.where(acc == 0.0, 0.0, acc)
        pos = r0 + lax.broadcasted_iota(jnp.int32, (tk, tq), 0)
        bits = pltpu.bitcast(acc, jnp.int32)
        key = jnp.where(bits >= 0, bits, bits ^ 0x7FFFFFFF)
        key = jnp.where(pos < kb, key, KEY_MASKED)
        key_ref[pl.ds(r0, tk), :] = key
        hi_ref[pl.ds(r0, tk), :] = lax.shift_right_arithmetic(key, 16).astype(jnp.int16)
        return carry

    lax.fori_loop(0, n_tiles, score_tile, 0)

    def search16(ref, kth):
        def count_ge(cand):
            c16 = cand.astype(jnp.int16)

            def body(i, acc):
                r0 = pl.multiple_of(i * tk, tk)
                ind = jnp.where(ref[pl.ds(r0, tk), :] >= c16, jnp.bfloat16(1), jnp.bfloat16(0))
                parts = [ind[r * PACKED_ROWS:(r + 1) * PACKED_ROWS] for r in range(tk // PACKED_ROWS)]
                while len(parts) > 1:
                    nxt = [parts[r] + parts[r + 1] for r in range(0, len(parts) - 1, 2)]
                    parts = nxt + ([parts[-1]] if len(parts) % 2 else [])
                return acc + parts[0].astype(F32)

            acc = lax.fori_loop(0, n_tiles, body, jnp.zeros((PACKED_ROWS, tq), F32))
            return acc.sum(axis=0, keepdims=True)

        def step(it, lo):
            cand = lo + jnp.left_shift(jnp.int32(1), 15 - it)
            return jnp.where(count_ge(cand) >= kth, cand, lo)

        return lax.fori_loop(0, 16, step, jnp.full((1, tq), I16_MIN, jnp.int32))

    tau_hi = search16(hi_ref, jnp.full((1, tq), float(topk), F32))

    def low_tile(i, acc):
        r0 = pl.multiple_of(i * tk, tk)
        blk = key_ref[pl.ds(r0, tk), :]
        hi = lax.shift_right_arithmetic(blk, 16)
        low = jnp.where(hi == tau_hi, (blk & 0xFFFF) + I16_MIN, I16_MIN)
        lo_ref[pl.ds(r0, tk), :] = low.astype(jnp.int16)
        above = jnp.where(hi > tau_hi, 1, 0).astype(jnp.int32)
        return acc + above.reshape(tk // SUBLANES, SUBLANES, tq).sum(axis=0)

    c_above = lax.fori_loop(0, n_tiles, low_tile, jnp.zeros((SUBLANES, tq), jnp.int32))
    c_above = c_above.sum(axis=0, keepdims=True)
    tau_lo = search16(lo_ref, (topk - c_above).astype(F32))
    tau = tau_hi * 65536 + (tau_lo - I16_MIN)

    def count(pred):
        def body(i, acc):
            r0 = pl.multiple_of(i * tk, tk)
            ind = jnp.where(pred(key_ref[pl.ds(r0, tk), :]), 1, 0).astype(jnp.int32)
            return acc + ind.reshape(tk // SUBLANES, SUBLANES, tq).sum(axis=0)
        acc = lax.fori_loop(0, n_tiles, body, jnp.zeros((SUBLANES, tq), jnp.int32))
        return acc.sum(axis=0, keepdims=True)

    c_gt = count(lambda blk: blk > tau)
    c_ge = count(lambda blk: blk >= tau)
    cut = jnp.max(c_ge - topk) > 0

    @pl.when(jnp.logical_not(cut))
    def _():
        def bias_tile(i, carry):
            r0 = pl.multiple_of(i * tk, tk)
            pos = r0 + lax.broadcasted_iota(jnp.int32, (tk, tq), 0)
            keep = (key_ref[pl.ds(r0, tk), :] >= tau) & (pos < kb)
            bias_ref[pl.ds(r0, tk), :] = jnp.where(keep, 0.0, NEG_BIG)
            return carry

        lax.fori_loop(0, n_tiles, bias_tile, 0)

    @pl.when(cut)
    def _():
        rank = (topk - c_gt).astype(F32)
        tri = jnp.where(lax.broadcasted_iota(jnp.int32, (tk, tk), 0)
                        >= lax.broadcasted_iota(jnp.int32, (tk, tk), 1), 1.0, 0.0).astype(BF16)

        def bias_tile(i, seen):
            r0 = pl.multiple_of(i * tk, tk)
            blk = key_ref[pl.ds(r0, tk), :]
            pos = r0 + lax.broadcasted_iota(jnp.int32, (tk, tq), 0)
            eq = blk == tau
            nth = seen + jnp.dot(tri, jnp.where(eq, 1.0, 0.0).astype(BF16), preferred_element_type=F32)
            keep = ((blk > tau) | (eq & (nth <= rank))) & (pos < kb)
            bias_ref[pl.ds(r0, tk), :] = jnp.where(keep, 0.0, NEG_BIG)
            return nth[tk - 1:tk, :]

        lax.fori_loop(0, n_tiles, bias_tile, jnp.zeros((1, tq), F32))

    acc_ref[...] = jnp.zeros_like(acc_ref)
    m_ref[...] = jnp.full(m_ref.shape, NEG_BIG, F32)
    l_ref[...] = jnp.zeros_like(l_ref)

    def scores(t, s_ref):
        r0 = pl.multiple_of(t * tk, tk)
        for n in range(N_KV_HEADS):
            kt = k_ref[pl.ds(r0, tk), n * HEAD_DIM:(n + 1) * HEAD_DIM]
            s_ref[n] = jnp.dot(kt, qt_ref[n], preferred_element_type=F32)

    def softmax_pv(t, s_ref):
        r0 = pl.multiple_of(t * tk, tk)
        bt = bias_ref[pl.ds(r0, tk), :]
        bt = jnp.concatenate([bt] * KV_GROUP, axis=1)
        for n in range(N_KV_HEADS):
            s = s_ref[n] + bt
            m = m_ref[n]
            m_new = jnp.maximum(m, jnp.max(s, axis=0, keepdims=True))
            alpha = jnp.exp2(m - m_new)
            p = jnp.exp2(s - m_new)
            l_ref[n] = alpha * l_ref[n] + jnp.sum(p, axis=0, keepdims=True)
            m_ref[n] = m_new
            vt = vt_ref[t, n * HEAD_DIM:(n + 1) * HEAD_DIM, :]
            acc_ref[n] = alpha * acc_ref[n] + jnp.dot(vt, p.astype(BF16), preferred_element_type=F32)

    last_tile = k_ref.shape[0] // tk - 1
    scores(0, sa_ref)

    def tile_pair(i, carry):
        t = 2 * i
        scores(t + 1, sb_ref)
        softmax_pv(t, sa_ref)
        scores(jnp.minimum(t + 2, last_tile), sa_ref)
        softmax_pv(t + 1, sb_ref)
        return carry

    lax.fori_loop(0, n_tiles // 2, tile_pair, 0)

    @pl.when(n_tiles % 2 == 1)
    def _():
        softmax_pv(n_tiles - 1, sa_ref)

    for h in range(N_HEADS):
        n, g = h // KV_GROUP, h % KV_GROUP
        inv_l = 1.0 / l_ref[n][:, g * tq:(g + 1) * tq]
        oh = acc_ref[n, :, g * tq:(g + 1) * tq] * inv_l
        out_ref[:, h * HEAD_DIM:(h + 1) * HEAD_DIM] = oh.T.astype(BF16)


def _dsa_attention(q, qi, wi, ki, k, vt, n_valid, topk, tk):
    bsz, tp, _ = q.shape
    gw = KV_GROUP * TQ
    qrow = lambda width: pl.BlockSpec((None, TQ, width), lambda b, j: (b, j, 0))
    seq = lambda width: pl.BlockSpec((None, tp, width), lambda b, j: (b, 0, 0))
    return pl.pallas_call(
        functools.partial(_dsa_kernel, n_valid=n_valid, topk=topk, tk=tk),
        grid=(bsz, pl.cdiv(tp, TQ)),
        in_specs=[qrow(N_HEADS * HEAD_DIM), qrow(IDX_HEADS * IDX_DIM), qrow(LANES), seq(LANES),
                  seq(N_KV_HEADS * HEAD_DIM),
                  pl.BlockSpec((None, tp // tk, N_KV_HEADS * HEAD_DIM, tk), lambda b, j: (b, 0, 0, 0))],
        out_specs=qrow(N_HEADS * HEAD_DIM),
        out_shape=jax.ShapeDtypeStruct((bsz, tp, N_HEADS * HEAD_DIM), BF16),
        scratch_shapes=[pltpu.VMEM((tp, TQ), jnp.int32), pltpu.VMEM((tp, TQ), jnp.int16),
                        pltpu.VMEM((tp, TQ), jnp.int16), pltpu.VMEM((tp, TQ), F32),
                        pltpu.VMEM((N_KV_HEADS, HEAD_DIM, gw), BF16),
                        pltpu.VMEM((LANES, IDX_HEADS * TQ), BF16),
                        pltpu.VMEM((N_KV_HEADS, HEAD_DIM, gw), F32),
                        pltpu.VMEM((N_KV_HEADS, 1, gw), F32), pltpu.VMEM((N_KV_HEADS, 1, gw), F32),
                        pltpu.VMEM((N_KV_HEADS, tk, gw), F32), pltpu.VMEM((N_KV_HEADS, tk, gw), F32)],
        compiler_params=pltpu.CompilerParams(dimension_semantics=("arbitrary", "arbitrary"),
                                             vmem_limit_bytes=VMEM_LIMIT),
        name="dsa_attention",
    )(q, qi, wi, ki, k, vt)


def _rglru_kernel(h_ref, g_ref, w_ref, cw_ref, cb_ref, wa_ref, ba_ref, wx_ref, bx_ref, lam_ref,
                  out_ref, xext_ref, a_ref, b_ref, state_ref):
    tm = h_ref.shape[0]
    hist = SUBLANES

    @pl.when(pl.program_id(1) == 0)
    def _():
        xext_ref[0:hist, :] = jnp.zeros((hist, D_MODEL), F32)
        state_ref[...] = jnp.zeros_like(state_ref)

    n = _rms(h_ref[...], g_ref[...]).astype(BF16)
    xext_ref[hist:hist + tm, :] = jnp.dot(n, w_ref[:, 0:D_MODEL], preferred_element_type=F32)
    u = cb_ref[...] + cw_ref[0:1, :] * xext_ref[hist - RNN_CONV + 1:hist - RNN_CONV + 1 + tm, :]
    for kk in range(1, RNN_CONV):
        o = hist - RNN_CONV + 1 + kk
        u = u + cw_ref[kk:kk + 1, :] * xext_ref[o:o + tm, :]
    xext_ref[0:hist, :] = xext_ref[tm:tm + hist, :]

    ub = u.astype(BF16)
    rs, is_ = [], []
    for blk in range(RNN_BLOCKS):
        sl = slice(blk * RNN_BLOCK_DIM, (blk + 1) * RNN_BLOCK_DIM)
        rs.append(jnp.dot(ub[:, sl], wa_ref[blk], preferred_element_type=F32))
        is_.append(jnp.dot(ub[:, sl], wx_ref[blk], preferred_element_type=F32))
    r = jax.nn.sigmoid(jnp.concatenate(rs, axis=1) + ba_ref[...])
    ig = jax.nn.sigmoid(jnp.concatenate(is_, axis=1) + bx_ref[...])
    nl = -lam_ref[...]
    softplus = jnp.maximum(nl, 0.0) + jnp.log1p(jnp.exp(-jnp.abs(nl)))
    log_a = -LRU_C * r * softplus
    a_all = jnp.exp(log_a)
    b_all = jnp.sqrt(-_expm1(2.0 * log_a)) * (ig * u)
    nchunk = D_MODEL // LANES
    for c in range(nchunk):
        a_ref[c] = a_all[:, c * LANES:(c + 1) * LANES]
        b_ref[c] = b_all[:, c * LANES:(c + 1) * LANES]

    seg = tm // SCAN_SEGMENTS
    groups = SCAN_SEGMENTS // SUBLANES

    def step(i, carry):
        out = []
        for v in range(groups):
            prod, loc = carry[2 * v], carry[2 * v + 1]
            rows = pl.ds(v * SUBLANES * seg + i, SUBLANES, stride=seg)
            a = a_ref[:, rows, :]
            prod = a * prod
            loc = a * loc + b_ref[:, rows, :]
            a_ref[:, rows, :] = prod
            b_ref[:, rows, :] = loc
            out += [prod, loc]
        return tuple(out)

    init = (jnp.ones((nchunk, SUBLANES, LANES), F32), jnp.zeros((nchunk, SUBLANES, LANES), F32)) * groups
    ends = lax.fori_loop(0, seg, step, init)
    gel = jax.nn.gelu(jnp.dot(n, w_ref[:, D_MODEL:], preferred_element_type=F32))
    hcur = state_ref[...]
    for s in range(SCAN_SEGMENTS):
        rs = slice(s * seg, (s + 1) * seg)
        hs = a_ref[:, rs, :] * hcur + b_ref[:, rs, :]
        for c in range(nchunk):
            cs = slice(c * LANES, (c + 1) * LANES)
            out_ref[rs, cs] = (hs[c] * gel[rs, cs]).astype(BF16)
        v, r = s // SUBLANES, s % SUBLANES
        hcur = ends[2 * v][:, r:r + 1, :] * hcur + ends[2 * v + 1][:, r:r + 1, :]
    state_ref[...] = hcur


def _rglru(h3, g, w, cw, cb, wa, ba, wx, bx, lam, tm):
    bsz, tp, _ = h3.shape
    row = pl.BlockSpec((None, tm, D_MODEL), lambda b, t: (b, t, 0))
    vec = _const_spec((1, D_MODEL))
    blkw = _const_spec((RNN_BLOCKS, RNN_BLOCK_DIM, RNN_BLOCK_DIM))
    return pl.pallas_call(
        _rglru_kernel,
        grid=(bsz, tp // tm),
        in_specs=[row, vec, _const_spec((D_MODEL, 2 * D_MODEL)), _const_spec((RNN_CONV, D_MODEL)), vec,
                  blkw, vec, blkw, vec, vec],
        out_specs=row,
        out_shape=jax.ShapeDtypeStruct((bsz, tp, D_MODEL), BF16),
        scratch_shapes=[pltpu.VMEM((tm + SUBLANES, D_MODEL), F32),
                        pltpu.VMEM((D_MODEL // LANES, tm, LANES), F32),
                        pltpu.VMEM((D_MODEL // LANES, tm, LANES), F32),
                        pltpu.VMEM((D_MODEL // LANES, 1, LANES), F32)],
        compiler_params=pltpu.CompilerParams(dimension_semantics=("arbitrary", "arbitrary"),
                                             vmem_limit_bytes=VMEM_LIMIT),
        name="rglru",
    )(h3, g, w, cw, cb, wa, ba, wx, bx, lam)


SCAN_SEGMENTS = 16
CONV_HIST = 32
CONV_ROWS = 128


def _conformer_kernel(h_ref, g_ref, w_ref, dw_ref, db_ref, lg_ref, lb_ref, out_ref, xs_ref, y_ref):
    tm = h_ref.shape[0]

    @pl.when(pl.program_id(1) == 0)
    def _():
        xs_ref[0, 0:CONV_HIST, :] = jnp.zeros((CONV_HIST, D_MODEL), F32)

    n = _rms(h_ref[...], g_ref[...]).astype(BF16)
    a = jnp.dot(n, w_ref[:, 0:D_MODEL], preferred_element_type=F32)
    gate = jnp.dot(n, w_ref[:, D_MODEL:], preferred_element_type=F32)
    xs_ref[0, CONV_HIST:CONV_HIST + tm, :] = a * jax.nn.sigmoid(gate)
    span = tm + CONV_HIST - SUBLANES
    for s in range(1, SUBLANES):
        xs_ref[s, 0:span, :] = xs_ref[0, s:s + span, :]

    first = CONV_HIST - CONV_KERNEL + 1
    row_chunks = range(0, tm, CONV_ROWS)
    for c0 in range(0, D_MODEL, LANES):
        cs = slice(c0, c0 + LANES)
        accs = [jnp.broadcast_to(db_ref[:, cs], (CONV_ROWS, LANES)) for _ in row_chunks]
        for kk in range(CONV_KERNEL):
            shift = (first + kk) % SUBLANES
            base = first + kk - shift
            wk = dw_ref[kk:kk + 1, cs]
            for ri, r0 in enumerate(row_chunks):
                accs[ri] = accs[ri] + wk * xs_ref[shift, base + r0:base + r0 + CONV_ROWS, cs]
        for ri, r0 in enumerate(row_chunks):
            y_ref[r0:r0 + CONV_ROWS, cs] = accs[ri]
    xs_ref[0, 0:CONV_HIST, :] = xs_ref[0, tm:tm + CONV_HIST, :]

    y = y_ref[...]
    mu = jnp.mean(y, axis=-1, keepdims=True)
    yc = y - mu
    z = yc * lax.rsqrt(jnp.mean(yc * yc, axis=-1, keepdims=True) + NORM_EPS) * lg_ref[...] + lb_ref[...]
    out_ref[...] = jax.nn.silu(z).astype(BF16)


def _conformer(h3, g, w, dw, db, lg, lb, tm):
    bsz, tp, _ = h3.shape
    row = pl.BlockSpec((None, tm, D_MODEL), lambda b, t: (b, t, 0))
    vec = _const_spec((1, D_MODEL))
    return pl.pallas_call(
        _conformer_kernel,
        grid=(bsz, tp // tm),
        in_specs=[row, vec, _const_spec((D_MODEL, 2 * D_MODEL)), _const_spec((CONV_KERNEL, D_MODEL)),
                  vec, vec, vec],
        out_specs=row,
        out_shape=jax.ShapeDtypeStruct((bsz, tp, D_MODEL), BF16),
        scratch_shapes=[pltpu.VMEM((SUBLANES, tm + CONV_HIST, D_MODEL), F32), pltpu.VMEM((tm, D_MODEL), F32)],
        compiler_params=pltpu.CompilerParams(dimension_semantics=("arbitrary", "arbitrary"),
                                             vmem_limit_bytes=VMEM_LIMIT),
        name="conformer",
    )(h3, g, w, dw, db, lg, lb)


def _merge_kernel(h_ref, at_ref, rn_ref, cv_ref, g_ref, wg_ref, wa_ref, wr_ref, wc_ref, wo_ref, out_ref):
    h = h_ref[...]
    n = _rms(h, g_ref[...]).astype(BF16)
    merged = None
    for i, (src, wref) in enumerate(((at_ref, wa_ref), (rn_ref, wr_ref), (cv_ref, wc_ref))):
        gate = jax.nn.sigmoid(jnp.dot(n, wg_ref[:, i * D_MODEL:(i + 1) * D_MODEL],
                                      preferred_element_type=F32))
        term = gate * jnp.dot(src[...], wref[...], preferred_element_type=F32)
        merged = term if merged is None else merged + term
    out_ref[...] = h + jnp.dot(merged.astype(BF16), wo_ref[...], preferred_element_type=F32)


def _merge(h2, attn, rnn, cnv, g, wg, wa, wr, wc, wo, tm):
    rows = h2.shape[0]
    row = pl.BlockSpec((tm, D_MODEL), lambda i: (i, 0))
    sq = _const_spec((D_MODEL, D_MODEL))
    return pl.pallas_call(
        _merge_kernel,
        grid=(rows // tm,),
        in_specs=[row, row, row, row, _const_spec((1, D_MODEL)), _const_spec((D_MODEL, 3 * D_MODEL)),
                  sq, sq, sq, sq],
        out_specs=row,
        out_shape=jax.ShapeDtypeStruct((rows, D_MODEL), F32),
        compiler_params=pltpu.CompilerParams(dimension_semantics=("arbitrary",),
                                             vmem_limit_bytes=VMEM_LIMIT),
        name="merge",
    )(h2, attn, rnn, cnv, g, wg, wa, wr, wc, wo)


def _ffn_kernel(h_ref, g_ref, wg_ref, wu_ref, wd_ref, out_ref):
    h = h_ref[...]
    f = _rms(h, g_ref[...]).astype(BF16)
    gate = jnp.dot(f, wg_ref[...], preferred_element_type=F32)
    up = jnp.dot(f, wu_ref[...], preferred_element_type=F32)
    act = (jax.nn.silu(gate) * up).astype(BF16)
    out_ref[...] = h + jnp.dot(act, wd_ref[...], preferred_element_type=F32)


def _ffn(h2, g, wg, wu, wd, tm):
    rows = h2.shape[0]
    dff = wg.shape[1]
    row = pl.BlockSpec((tm, D_MODEL), lambda i: (i, 0))
    return pl.pallas_call(
        _ffn_kernel,
        grid=(rows // tm,),
        in_specs=[row, _const_spec((1, D_MODEL)), _const_spec((D_MODEL, dff)), _const_spec((D_MODEL, dff)),
                  _const_spec((dff, D_MODEL))],
        out_specs=row,
        out_shape=jax.ShapeDtypeStruct((rows, D_MODEL), F32),
        compiler_params=pltpu.CompilerParams(dimension_semantics=("arbitrary",),
                                             vmem_limit_bytes=VMEM_LIMIT),
        name="ffn",
    )(h2, g, wg, wu, wd)


def _rope_tables(n, dim):
    inv = ROPE_THETA ** (-jnp.arange(0, dim, 2, dtype=F32) / dim)
    ang = jnp.arange(n, dtype=F32)[:, None] * inv[None, :]
    return jnp.cos(ang), jnp.sin(ang)


def kernel(x, meta, mix_norm_g, w_in, q_norm_g, k_norm_g, rnn_conv_w, rnn_conv_b, rnn_wa, rnn_ba, rnn_wx, rnn_bx, rnn_lambda, conv_dw_w, conv_dw_b, conv_ln_g, conv_ln_b, w_o_attn, w_o_rnn, w_o_conv, w_out, ffn_norm_g, w_ffn_gate, w_ffn_up, w_ffn_down):
    bsz, seq, _ = x.shape
    depth = w_in.shape[0]
    t_valid = seq + N_META
    tp = -(-t_valid // LANES) * LANES
    topk = min(TOPK_MAX, seq // 4)
    rows = bsz * tp
    tm_seq = _pick_tile(tp, (384, 256, 128))
    tm_flat = _pick_tile(rows, (512, 384, 256, 128))

    q_w, kv_w, qi_w = N_HEADS * HEAD_DIM, N_KV_HEADS * HEAD_DIM, IDX_HEADS * IDX_DIM
    o = np.cumsum([0, q_w, kv_w, kv_w, qi_w, IDX_DIM, IDX_HEADS, D_MODEL, D_MODEL, 2 * D_MODEL, 3 * D_MODEL])
    pad_w = LANES - IDX_DIM - IDX_HEADS
    w_attn = jnp.concatenate([w_in[:, :, o[0]:o[6]], jnp.zeros((depth, D_MODEL, pad_w), w_in.dtype)],
                             axis=2).astype(BF16)
    w_rnn = w_in[:, :, o[6]:o[8]].astype(BF16)
    w_cnv = w_in[:, :, o[8]:o[9]].astype(BF16)
    w_gate = w_in[:, :, o[9]:o[10]].astype(BF16)
    wa_b, wx_b = rnn_wa.astype(BF16), rnn_wx.astype(BF16)
    w_oa, w_or, w_oc, w_ot = (a.astype(BF16) for a in (w_o_attn, w_o_rnn, w_o_conv, w_out))
    w_fg, w_fu, w_fd = (a.astype(BF16) for a in (w_ffn_gate, w_ffn_up, w_ffn_down))

    cos_a, sin_a = _rope_tables(tp, HEAD_DIM)
    cos_i, sin_i = _rope_tables(tp, IDX_DIM)
    tabs = (jnp.concatenate([cos_a, cos_a], axis=1), jnp.concatenate([-sin_a, sin_a], axis=1),
            jnp.concatenate([cos_i] * 4, axis=1), jnp.concatenate([-sin_i, sin_i] * 2, axis=1))

    h = jnp.concatenate([jnp.broadcast_to(meta[None].astype(x.dtype), (bsz, N_META, D_MODEL)), x,
                         jnp.zeros((bsz, tp - t_valid, D_MODEL), x.dtype)], axis=1)
    vec = lambda a: a.reshape(1, -1)

    for l in range(depth):
        h2 = h.reshape(rows, D_MODEL)
        q, k, vt, qi, ki, wi = _attn_proj(h2, vec(mix_norm_g[l]), w_attn[l], vec(q_norm_g[l]),
                                          vec(k_norm_g[l]), tabs, tm_seq, tp // tm_seq)
        seq3 = lambda a: a.reshape(bsz, tp, a.shape[-1])
        attn = _dsa_attention(seq3(q), seq3(qi), seq3(wi), seq3(ki), seq3(k),
                              vt.reshape(bsz, tp // tm_seq, N_KV_HEADS * HEAD_DIM, tm_seq),
                              t_valid, topk, tm_seq)
        attn = attn.reshape(rows, N_HEADS * HEAD_DIM)

        rnn = _rglru(h, vec(mix_norm_g[l]), w_rnn[l], rnn_conv_w[l], vec(rnn_conv_b[l]), wa_b[l],
                     vec(rnn_ba[l]), wx_b[l], vec(rnn_bx[l]), vec(rnn_lambda[l]), tm_seq)
        cnv = _conformer(h, vec(mix_norm_g[l]), w_cnv[l], conv_dw_w[l], vec(conv_dw_b[l]),
                         vec(conv_ln_g[l]), vec(conv_ln_b[l]), tm_seq)
        h2 = _merge(h2, attn, rnn.reshape(rows, D_MODEL), cnv.reshape(rows, D_MODEL), vec(mix_norm_g[l]),
                    w_gate[l], w_oa[l], w_or[l], w_oc[l], w_ot[l], tm_flat)
        h2 = _ffn(h2, vec(ffn_norm_g[l]), w_fg[l], w_fu[l], w_fd[l], tm_flat)
        h = h2.reshape(bsz, tp, D_MODEL)

    return h[:, N_META:t_valid]
```

```python
import functools

import jax
import jax.numpy as jnp
import numpy as np
from jax import lax
from jax.experimental import pallas as pl
from jax.experimental.pallas import tpu as pltpu

D_MODEL = 1024
CHUNK = 64
N_META = 16
N_HEADS = 8
N_KV_HEADS = 2
HEAD_DIM = 128
KV_GROUP = N_HEADS // N_KV_HEADS
IDX_HEADS = 8
IDX_DIM = 64
TOPK_MAX = 256
ROPE_THETA = 10000.0
RNN_BLOCKS = 8
RNN_BLOCK_DIM = D_MODEL // RNN_BLOCKS
RNN_CONV = 4
LRU_C = 8.0
CONV_KERNEL = 31
NORM_EPS = 1e-6

LANES = 128
SUBLANES = 8
PACKED_ROWS = 16
TQ = 256
VMEM_LIMIT = 56 * 1024 * 1024
LOG2E = 1.4426950408889634

I16_MIN = -(2 ** 15)
KEY_MASKED = -2139095041
NEG_BIG = -(2.0 ** 100)
VT_ROWS = HEAD_DIM + PACKED_ROWS

F32 = jnp.float32
BF16 = jnp.bfloat16


def _rms(x, g):
    return x * lax.rsqrt(jnp.mean(x * x, axis=-1, keepdims=True) + NORM_EPS) * g


def _expm1(y):
    u = jnp.exp(y)
    near = jnp.where(u == 1.0, y, (u - 1.0) * y / jnp.log(jnp.where(u == 1.0, 2.0, u)))
    return jnp.where(y > -0.5, near, u - 1.0)


def _const_spec(shape):
    nd = len(shape)
    return pl.BlockSpec(shape, lambda *_: (0,) * nd, pipeline_mode=pl.Buffered(1))


def _pick_tile(n, candidates):
    for c in candidates:
        if n % c == 0:
            return c
    raise ValueError(f"no tile for {n}")


def _attn_proj_kernel(x_ref, g_ref, w_ref, qg_ref, kg_ref, cosa_ref, sina_ref, cosi_ref, sini_ref,
                      q_ref, k_ref, vt_ref, qi_ref, ki_ref, wi_ref, *, idx_scale):
    n = _rms(x_ref[...], g_ref[...]).astype(BF16)
    cosa, sina = cosa_ref[...], sina_ref[...]
    cosi, sini = cosi_ref[...], sini_ref[...]
    lane = lax.broadcasted_iota(jnp.int32, cosi.shape, 1)
    low_half = (lane % IDX_DIM) < (IDX_DIM // 2)

    def rot_attn(a, g):
        a = _rms(a, g)
        return a * cosa + pltpu.roll(a, HEAD_DIM // 2, axis=1) * sina

    def rot_idx(a):
        partner = jnp.where(low_half, pltpu.roll(a, LANES - IDX_DIM // 2, axis=1),
                            pltpu.roll(a, IDX_DIM // 2, axis=1))
        return a * cosi + partner * sini

    q_w = N_HEADS * HEAD_DIM
    kv_w = N_KV_HEADS * HEAD_DIM
    qi_w = IDX_HEADS * IDX_DIM
    o_k, o_v, o_qi, o_ki = q_w, q_w + kv_w, q_w + 2 * kv_w, q_w + 2 * kv_w + qi_w

    q_scale = (HEAD_DIM ** -0.5) * LOG2E
    pq = jnp.dot(n, w_ref[:, 0:q_w], preferred_element_type=F32)
    for h in range(N_HEADS):
        sl = slice(h * HEAD_DIM, (h + 1) * HEAD_DIM)
        q_ref[:, sl] = (rot_attn(pq[:, sl], qg_ref[...]) * q_scale).astype(BF16)
    pk = jnp.dot(n, w_ref[:, o_k:o_qi], preferred_element_type=F32)
    for h in range(N_KV_HEADS):
        sl = slice(h * HEAD_DIM, (h + 1) * HEAD_DIM)
        k_ref[:, sl] = rot_attn(pk[:, sl], kg_ref[...]).astype(BF16)
        vt_ref[h * VT_ROWS:h * VT_ROWS + HEAD_DIM, :] = (
            pk[:, kv_w + h * HEAD_DIM:kv_w + (h + 1) * HEAD_DIM].T.astype(BF16))
        vt_ref[h * VT_ROWS + HEAD_DIM:(h + 1) * VT_ROWS, :] = jnp.ones((PACKED_ROWS, pk.shape[0]), BF16)
    pi = jnp.dot(n, w_ref[:, o_qi:], preferred_element_type=F32)
    for c in range(qi_w // LANES):
        sl = slice(c * LANES, (c + 1) * LANES)
        qi_ref[:, sl] = rot_idx(pi[:, sl]).astype(BF16)
    last = pi[:, qi_w:]
    ki = rot_idx(last)
    ki_ref[...] = jnp.where(lane < IDX_DIM, ki, pltpu.roll(ki, IDX_DIM, axis=1)).astype(BF16)
    wi_ref[...] = last * idx_scale


def _attn_proj(h2, g, w, qg, kg, tabs, tm, tiles_per_seq):
    rows = h2.shape[0]
    wcols = w.shape[1]
    row = lambda width: pl.BlockSpec((tm, width), lambda i: (i, 0))
    tab = pl.BlockSpec((tm, LANES), lambda i: (i % tiles_per_seq, 0))
    idx_scale = (IDX_HEADS ** -0.5) * (IDX_DIM ** -0.5)
    return pl.pallas_call(
        functools.partial(_attn_proj_kernel, idx_scale=idx_scale),
        grid=(rows // tm,),
        in_specs=[row(D_MODEL), _const_spec((1, D_MODEL)), _const_spec((D_MODEL, wcols)),
                  _const_spec((1, HEAD_DIM)), _const_spec((1, HEAD_DIM)), tab, tab, tab, tab],
        out_specs=[row(N_HEADS * HEAD_DIM), row(N_KV_HEADS * HEAD_DIM),
                   pl.BlockSpec((None, N_KV_HEADS * VT_ROWS, tm), lambda i: (i, 0, 0)),
                   row(IDX_HEADS * IDX_DIM), row(LANES), row(LANES)],
        out_shape=[jax.ShapeDtypeStruct((rows, N_HEADS * HEAD_DIM), BF16),
                   jax.ShapeDtypeStruct((rows, N_KV_HEADS * HEAD_DIM), BF16),
                   jax.ShapeDtypeStruct((rows // tm, N_KV_HEADS * VT_ROWS, tm), BF16),
                   jax.ShapeDtypeStruct((rows, IDX_HEADS * IDX_DIM), BF16),
                   jax.ShapeDtypeStruct((rows, LANES), BF16),
                   jax.ShapeDtypeStruct((rows, LANES), F32)],
        compiler_params=pltpu.CompilerParams(dimension_semantics=("arbitrary",),
                                             vmem_limit_bytes=VMEM_LIMIT),
        name="attn_proj",
    )(h2, g, w, qg, kg, *tabs)


def _dsa_kernel(q_ref, qi_ref, wi_ref, ki_ref, k_ref, vt_ref, out_ref,
                key_ref, hi_ref, lo_ref, bias_ref, qt_ref, qit_ref, acc_ref, m_ref, sa_ref, sb_ref,
                *, n_valid, topk, tk):
    tq = TQ
    gw = KV_GROUP * tq
    t0 = pl.program_id(1) * tq
    tcol = t0 + lax.broadcasted_iota(jnp.int32, (1, tq), 1)
    kb = jnp.where(tcol < N_META, N_META,
                   N_META + CHUNK * (1 + jnp.right_shift(tcol - N_META, 6)))
    kb = jnp.where(tcol < n_valid, jnp.minimum(kb, n_valid), n_valid)
    kext = jnp.minimum(n_valid, N_META + CHUNK * (1 + (t0 + tq - 1 - N_META) // CHUNK))
    n_tiles = (kext + tk - 1) // tk

    for h in range(N_HEADS):
        qh = q_ref[:, h * HEAD_DIM:(h + 1) * HEAD_DIM].astype(F32).T
        qt_ref[h // KV_GROUP, :, (h % KV_GROUP) * tq:(h % KV_GROUP + 1) * tq] = qh.astype(BF16)
    row = lax.broadcasted_iota(jnp.int32, (LANES, tq), 0)
    for hp in range(IDX_HEADS // 2):
        pair = qi_ref[:, hp * LANES:(hp + 1) * LANES].astype(F32).T
        qit_ref[:, (2 * hp) * tq:(2 * hp + 1) * tq] = jnp.where(row < IDX_DIM, pair, 0.0).astype(BF16)
        qit_ref[:, (2 * hp + 1) * tq:(2 * hp + 2) * tq] = jnp.where(row >= IDX_DIM, pair, 0.0).astype(BF16)
    w = wi_ref[...].T[IDX_DIM:IDX_DIM + IDX_HEADS, :]

    def score_tile(i, carry):
        r0 = pl.multiple_of(i * tk, tk)
        kt = ki_ref[pl.ds(r0, tk), :]
        acc = jnp.zeros((tk, tq), F32)
        for hp in range(IDX_HEADS // 2):
            lg = jnp.dot(kt, qit_ref[:, hp * 2 * tq:(hp + 1) * 2 * tq],
                         preferred_element_type=F32)
            for hh in range(2):
                h = 2 * hp + hh
                acc = acc + jnp.maximum(lg[:, hh * tq:(hh + 1) * tq], 0.0) * w[h:h + 1, :]
        acc = jnp.where(acc == 0.0, 0.0, acc)
        pos = r0 + lax.broadcasted_iota(jnp.int32, (tk, tq), 0)
        bits = pltpu.bitcast(acc, jnp.int32)
        key = jnp.where(bits >= 0, bits, bits ^ 0x7FFFFFFF)
        key = jnp.where(pos < kb, key, KEY_MASKED)
        key_ref[pl.ds(r0, tk), :] = key
        hi_ref[pl.ds(r0, tk), :] = lax.shift_right_arithmetic(key, 16).astype(jnp.int16)
        return carry

    lax.fori_loop(0, n_tiles, score_tile, 0)

    def search16(ref, kth):
        def count_ge(cand):
            c16 = cand.astype(jnp.int16)

            def body(i, acc):
                r0 = pl.multiple_of(i * tk, tk)
                ind = jnp.where(ref[pl.ds(r0, tk), :] >= c16, jnp.bfloat16(1), jnp.bfloat16(0))
                parts = [ind[r * PACKED_ROWS:(r + 1) * PACKED_ROWS] for r in range(tk // PACKED_ROWS)]
                while len(parts) > 1:
                    nxt = [parts[r] + parts[r + 1] for r in range(0, len(parts) - 1, 2)]
                    parts = nxt + ([parts[-1]] if len(parts) % 2 else [])
                return acc + parts[0].astype(F32)

            acc = lax.fori_loop(0, n_tiles, body, jnp.zeros((PACKED_ROWS, tq), F32))
            return acc.sum(axis=0, keepdims=True)

        def step(it, lo):
            cand = lo + jnp.left_shift(jnp.int32(1), 15 - it)
            return jnp.where(count_ge(cand) >= kth, cand, lo)

        return lax.fori_loop(0, 16, step, jnp.full((1, tq), I16_MIN, jnp.int32))

    tau_hi = search16(hi_ref, jnp.full((1, tq), float(topk), F32))

    def low_tile(i, acc):
        r0 = pl.multiple_of(i * tk, tk)
        blk = key_ref[pl.ds(r0, tk), :]
        hi = lax.shift_right_arithmetic(blk, 16)
        low = jnp.where(hi == tau_hi, (blk & 0xFFFF) + I16_MIN, I16_MIN)
        lo_ref[pl.ds(r0, tk), :] = low.astype(jnp.int16)
        above = jnp.where(hi > tau_hi, 1, 0).astype(jnp.int32)
        return acc + above.reshape(tk // SUBLANES, SUBLANES, tq).sum(axis=0)

    c_above = lax.fori_loop(0, n_tiles, low_tile, jnp.zeros((SUBLANES, tq), jnp.int32))
    c_above = c_above.sum(axis=0, keepdims=True)
    tau_lo = search16(lo_ref, (topk - c_above).astype(F32))
    tau = tau_hi * 65536 + (tau_lo - I16_MIN)

    def count(pred):
        def body(i, acc):
            r0 = pl.multiple_of(i * tk, tk)
            ind = jnp.where(pred(key_ref[pl.ds(r0, tk), :]), 1, 0).astype(jnp.int32)
            return acc + ind.reshape(tk // SUBLANES, SUBLANES, tq).sum(axis=0)
        acc = lax.fori_loop(0, n_tiles, body, jnp.zeros((SUBLANES, tq), jnp.int32))
        return acc.sum(axis=0, keepdims=True)

    c_gt = count(lambda blk: blk > tau)
    c_ge = count(lambda blk: blk >= tau)
    cut = jnp.max(c_ge - topk) > 0

    @pl.when(jnp.logical_not(cut))
    def _():
        def bias_tile(i, carry):
            r0 = pl.multiple_of(i * tk, tk)
            pos = r0 + lax.broadcasted_iota(jnp.int32, (tk, tq), 0)
            keep = (key_ref[pl.ds(r0, tk), :] >= tau) & (pos < kb)
            bias_ref[pl.ds(r0, tk), :] = jnp.where(keep, 0.0, NEG_BIG).astype(BF16)
            return carry

        lax.fori_loop(0, n_tiles, bias_tile, 0)

    @pl.when(cut)
    def _():
        rank = (topk - c_gt).astype(F32)
        tri = jnp.where(lax.broadcasted_iota(jnp.int32, (tk, tk), 0)
                        >= lax.broadcasted_iota(jnp.int32, (tk, tk), 1), 1.0, 0.0).astype(BF16)

        def bias_tile(i, seen):
            r0 = pl.multiple_of(i * tk, tk)
            blk = key_ref[pl.ds(r0, tk), :]
            pos = r0 + lax.broadcasted_iota(jnp.int32, (tk, tq), 0)
            eq = blk == tau
            nth = seen + jnp.dot(tri, jnp.where(eq, 1.0, 0.0).astype(BF16), preferred_element_type=F32)
            keep = ((blk > tau) | (eq & (nth <= rank))) & (pos < kb)
            bias_ref[pl.ds(r0, tk), :] = jnp.where(keep, 0.0, NEG_BIG).astype(BF16)
            return nth[tk - 1:tk, :]

        lax.fori_loop(0, n_tiles, bias_tile, jnp.zeros((1, tq), F32))

    acc_ref[...] = jnp.zeros_like(acc_ref)
    m_ref[...] = jnp.full(m_ref.shape, NEG_BIG, F32)

    def scores(t, s_ref):
        r0 = pl.multiple_of(t * tk, tk)
        for n in range(N_KV_HEADS):
            kt = k_ref[pl.ds(r0, tk), n * HEAD_DIM:(n + 1) * HEAD_DIM]
            s_ref[n] = jnp.dot(kt, qt_ref[n], preferred_element_type=F32).astype(BF16)

    def softmax_pv(t, s_ref):
        r0 = pl.multiple_of(t * tk, tk)
        bt = bias_ref[pl.ds(r0, tk), :]
        bt = jnp.concatenate([bt] * KV_GROUP, axis=1)
        for n in range(N_KV_HEADS):
            s = s_ref[n] + bt
            m = m_ref[n]
            m_new = jnp.maximum(m, jnp.max(s, axis=0, keepdims=True).astype(F32))
            alpha = jnp.exp2(m - m_new)
            p = jnp.exp2(s - m_new.astype(BF16))
            m_ref[n] = m_new
            vt = vt_ref[t, n * VT_ROWS:(n + 1) * VT_ROWS, :]
            acc_ref[n] = alpha * acc_ref[n] + jnp.dot(vt, p, preferred_element_type=F32)

    last_tile = k_ref.shape[0] // tk - 1
    scores(0, sa_ref)

    def tile_pair(i, carry):
        t = 2 * i
        scores(t + 1, sb_ref)
        softmax_pv(t, sa_ref)
        scores(jnp.minimum(t + 2, last_tile), sa_ref)
        softmax_pv(t + 1, sb_ref)
        return carry

    lax.fori_loop(0, n_tiles // 2, tile_pair, 0)

    @pl.when(n_tiles % 2 == 1)
    def _():
        softmax_pv(n_tiles - 1, sa_ref)

    for h in range(N_HEADS):
        n, g = h // KV_GROUP, h % KV_GROUP
        inv_l = 1.0 / acc_ref[n, HEAD_DIM:HEAD_DIM + 1, g * tq:(g + 1) * tq]
        oh = acc_ref[n, 0:HEAD_DIM, g * tq:(g + 1) * tq] * inv_l
        out_ref[:, h * HEAD_DIM:(h + 1) * HEAD_DIM] = oh.T.astype(BF16)


def _dsa_attention(q, qi, wi, ki, k, vt, n_valid, topk, tk):
    bsz, tp, _ = q.shape
    gw = KV_GROUP * TQ
    qrow = lambda width: pl.BlockSpec((None, TQ, width), lambda b, j: (b, j, 0))
    seq = lambda width: pl.BlockSpec((None, tp, width), lambda b, j: (b, 0, 0))
    return pl.pallas_call(
        functools.partial(_dsa_kernel, n_valid=n_valid, topk=topk, tk=tk),
        grid=(bsz, pl.cdiv(tp, TQ)),
        in_specs=[qrow(N_HEADS * HEAD_DIM), qrow(IDX_HEADS * IDX_DIM), qrow(LANES), seq(LANES),
                  seq(N_KV_HEADS * HEAD_DIM),
                  pl.BlockSpec((None, tp // tk, N_KV_HEADS * VT_ROWS, tk), lambda b, j: (b, 0, 0, 0))],
        out_specs=qrow(N_HEADS * HEAD_DIM),
        out_shape=jax.ShapeDtypeStruct((bsz, tp, N_HEADS * HEAD_DIM), BF16),
        scratch_shapes=[pltpu.VMEM((tp, TQ), jnp.int32), pltpu.VMEM((tp, TQ), jnp.int16),
                        pltpu.VMEM((tp, TQ), jnp.int16), pltpu.VMEM((tp, TQ), BF16),
                        pltpu.VMEM((N_KV_HEADS, HEAD_DIM, gw), BF16),
                        pltpu.VMEM((LANES, IDX_HEADS * TQ), BF16),
                        pltpu.VMEM((N_KV_HEADS, VT_ROWS, gw), F32),
                        pltpu.VMEM((N_KV_HEADS, 1, gw), F32),
                        pltpu.VMEM((N_KV_HEADS, tk, gw), BF16), pltpu.VMEM((N_KV_HEADS, tk, gw), BF16)],
        compiler_params=pltpu.CompilerParams(dimension_semantics=("arbitrary", "arbitrary"),
                                             vmem_limit_bytes=VMEM_LIMIT),
        name="dsa_attention",
    )(q, qi, wi, ki, k, vt)


def _rglru_kernel(h_ref, g_ref, w_ref, cw_ref, cb_ref, wa_ref, ba_ref, wx_ref, bx_ref, lam_ref,
                  out_ref, xext_ref, a_ref, b_ref, state_ref):
    tm = h_ref.shape[0]
    hist = SUBLANES

    @pl.when(pl.program_id(1) == 0)
    def _():
        xext_ref[0:hist, :] = jnp.zeros((hist, D_MODEL), F32)
        state_ref[...] = jnp.zeros_like(state_ref)

    n = _rms(h_ref[...], g_ref[...]).astype(BF16)
    xext_ref[hist:hist + tm, :] = jnp.dot(n, w_ref[:, 0:D_MODEL], preferred_element_type=F32)
    u = cb_ref[...] + cw_ref[0:1, :] * xext_ref[hist - RNN_CONV + 1:hist - RNN_CONV + 1 + tm, :]
    for kk in range(1, RNN_CONV):
        o = hist - RNN_CONV + 1 + kk
        u = u + cw_ref[kk:kk + 1, :] * xext_ref[o:o + tm, :]
    xext_ref[0:hist, :] = xext_ref[tm:tm + hist, :]

    ub = u.astype(BF16)
    rs, is_ = [], []
    for blk in range(RNN_BLOCKS):
        sl = slice(blk * RNN_BLOCK_DIM, (blk + 1) * RNN_BLOCK_DIM)
        rs.append(jnp.dot(ub[:, sl], wa_ref[blk], preferred_element_type=F32))
        is_.append(jnp.dot(ub[:, sl], wx_ref[blk], preferred_element_type=F32))
    r = jax.nn.sigmoid(jnp.concatenate(rs, axis=1) + ba_ref[...])
    ig = jax.nn.sigmoid(jnp.concatenate(is_, axis=1) + bx_ref[...])
    nl = -lam_ref[...]
    softplus = jnp.maximum(nl, 0.0) + jnp.log1p(jnp.exp(-jnp.abs(nl)))
    log_a = -LRU_C * r * softplus
    a_all = jnp.exp(log_a)
    b_all = jnp.sqrt(-_expm1(2.0 * log_a)) * (ig * u)
    nchunk = D_MODEL // LANES
    for c in range(nchunk):
        a_ref[c] = a_all[:, c * LANES:(c + 1) * LANES]
        b_ref[c] = b_all[:, c * LANES:(c + 1) * LANES]

    seg = tm // SCAN_SEGMENTS
    groups = SCAN_SEGMENTS // SUBLANES

    def step(i, carry):
        out = []
        for v in range(groups):
            prod, loc = carry[2 * v], carry[2 * v + 1]
            rows = pl.ds(v * SUBLANES * seg + i, SUBLANES, stride=seg)
            a = a_ref[:, rows, :]
            prod = a * prod
            loc = a * loc + b_ref[:, rows, :]
            a_ref[:, rows, :] = prod
            b_ref[:, rows, :] = loc
            out += [prod, loc]
        return tuple(out)

    init = (jnp.ones((nchunk, SUBLANES, LANES), F32), jnp.zeros((nchunk, SUBLANES, LANES), F32)) * groups
    ends = lax.fori_loop(0, seg, step, init)
    gel = jax.nn.gelu(jnp.dot(n, w_ref[:, D_MODEL:], preferred_element_type=F32))
    hcur = state_ref[...]
    for s in range(SCAN_SEGMENTS):
        rs = slice(s * seg, (s + 1) * seg)
        hs = a_ref[:, rs, :] * hcur + b_ref[:, rs, :]
        for c in range(nchunk):
            cs = slice(c * LANES, (c + 1) * LANES)
            out_ref[rs, cs] = (hs[c] * gel[rs, cs]).astype(BF16)
        v, r = s // SUBLANES, s % SUBLANES
        hcur = ends[2 * v][:, r:r + 1, :] * hcur + ends[2 * v + 1][:, r:r + 1, :]
    state_ref[...] = hcur


def _rglru(h3, g, w, cw, cb, wa, ba, wx, bx, lam, tm):
    bsz, tp, _ = h3.shape
    row = pl.BlockSpec((None, tm, D_MODEL), lambda b, t: (b, t, 0))
    vec = _const_spec((1, D_MODEL))
    blkw = _const_spec((RNN_BLOCKS, RNN_BLOCK_DIM, RNN_BLOCK_DIM))
    return pl.pallas_call(
        _rglru_kernel,
        grid=(bsz, tp // tm),
        in_specs=[row, vec, _const_spec((D_MODEL, 2 * D_MODEL)), _const_spec((RNN_CONV, D_MODEL)), vec,
                  blkw, vec, blkw, vec, vec],
        out_specs=row,
        out_shape=jax.ShapeDtypeStruct((bsz, tp, D_MODEL), BF16),
        scratch_shapes=[pltpu.VMEM((tm + SUBLANES, D_MODEL), F32),
                        pltpu.VMEM((D_MODEL // LANES, tm, LANES), F32),
                        pltpu.VMEM((D_MODEL // LANES, tm, LANES), F32),
                        pltpu.VMEM((D_MODEL // LANES, 1, LANES), F32)],
        compiler_params=pltpu.CompilerParams(dimension_semantics=("arbitrary", "arbitrary"),
                                             vmem_limit_bytes=VMEM_LIMIT),
        name="rglru",
    )(h3, g, w, cw, cb, wa, ba, wx, bx, lam)


SCAN_SEGMENTS = 16
CONV_HIST = 32
CONV_ROWS = 128


def _conformer_kernel(h_ref, g_ref, w_ref, dw_ref, db_ref, lg_ref, lb_ref, out_ref, xs_ref, y_ref):
    tm = h_ref.shape[0]

    @pl.when(pl.program_id(1) == 0)
    def _():
        xs_ref[0, 0:CONV_HIST, :] = jnp.zeros((CONV_HIST, D_MODEL), F32)

    n = _rms(h_ref[...], g_ref[...]).astype(BF16)
    a = jnp.dot(n, w_ref[:, 0:D_MODEL], preferred_element_type=F32)
    gate = jnp.dot(n, w_ref[:, D_MODEL:], preferred_element_type=F32)
    xs_ref[0, CONV_HIST:CONV_HIST + tm, :] = a * jax.nn.sigmoid(gate)
    span = tm + CONV_HIST - SUBLANES
    for s in range(1, SUBLANES):
        xs_ref[s, 0:span, :] = xs_ref[0, s:s + span, :]

    first = CONV_HIST - CONV_KERNEL + 1
    row_chunks = range(0, tm, CONV_ROWS)
    for c0 in range(0, D_MODEL, LANES):
        cs = slice(c0, c0 + LANES)
        accs = [jnp.broadcast_to(db_ref[:, cs], (CONV_ROWS, LANES)) for _ in row_chunks]
        for kk in range(CONV_KERNEL):
            shift = (first + kk) % SUBLANES
            base = first + kk - shift
            wk = dw_ref[kk:kk + 1, cs]
            for ri, r0 in enumerate(row_chunks):
                accs[ri] = accs[ri] + wk * xs_ref[shift, base + r0:base + r0 + CONV_ROWS, cs]
        for ri, r0 in enumerate(row_chunks):
            y_ref[r0:r0 + CONV_ROWS, cs] = accs[ri]
    xs_ref[0, 0:CONV_HIST, :] = xs_ref[0, tm:tm + CONV_HIST, :]

    y = y_ref[...]
    mu = jnp.mean(y, axis=-1, keepdims=True)
    yc = y - mu
    z = yc * lax.rsqrt(jnp.mean(yc * yc, axis=-1, keepdims=True) + NORM_EPS) * lg_ref[...] + lb_ref[...]
    out_ref[...] = jax.nn.silu(z).astype(BF16)


def _conformer(h3, g, w, dw, db, lg, lb, tm):
    bsz, tp, _ = h3.shape
    row = pl.BlockSpec((None, tm, D_MODEL), lambda b, t: (b, t, 0))
    vec = _const_spec((1, D_MODEL))
    return pl.pallas_call(
        _conformer_kernel,
        grid=(bsz, tp // tm),
        in_specs=[row, vec, _const_spec((D_MODEL, 2 * D_MODEL)), _const_spec((CONV_KERNEL, D_MODEL)),
                  vec, vec, vec],
        out_specs=row,
        out_shape=jax.ShapeDtypeStruct((bsz, tp, D_MODEL), BF16),
        scratch_shapes=[pltpu.VMEM((SUBLANES, tm + CONV_HIST, D_MODEL), F32), pltpu.VMEM((tm, D_MODEL), F32)],
        compiler_params=pltpu.CompilerParams(dimension_semantics=("arbitrary", "arbitrary"),
                                             vmem_limit_bytes=VMEM_LIMIT),
        name="conformer",
    )(h3, g, w, dw, db, lg, lb)


def _merge_kernel(h_ref, at_ref, rn_ref, cv_ref, g_ref, wg_ref, wa_ref, wr_ref, wc_ref, wo_ref, out_ref):
    h = h_ref[...]
    n = _rms(h, g_ref[...]).astype(BF16)
    merged = None
    for i, (src, wref) in enumerate(((at_ref, wa_ref), (rn_ref, wr_ref), (cv_ref, wc_ref))):
        gate = jax.nn.sigmoid(jnp.dot(n, wg_ref[:, i * D_MODEL:(i + 1) * D_MODEL],
                                      preferred_element_type=F32))
        term = gate * jnp.dot(src[...], wref[...], preferred_element_type=F32)
        merged = term if merged is None else merged + term
    out_ref[...] = h + jnp.dot(merged.astype(BF16), wo_ref[...], preferred_element_type=F32)


def _merge(h2, attn, rnn, cnv, g, wg, wa, wr, wc, wo, tm):
    rows = h2.shape[0]
    row = pl.BlockSpec((tm, D_MODEL), lambda i: (i, 0))
    sq = _const_spec((D_MODEL, D_MODEL))
    return pl.pallas_call(
        _merge_kernel,
        grid=(rows // tm,),
        in_specs=[row, row, row, row, _const_spec((1, D_MODEL)), _const_spec((D_MODEL, 3 * D_MODEL)),
                  sq, sq, sq, sq],
        out_specs=row,
        out_shape=jax.ShapeDtypeStruct((rows, D_MODEL), F32),
        compiler_params=pltpu.CompilerParams(dimension_semantics=("arbitrary",),
                                             vmem_limit_bytes=VMEM_LIMIT),
        name="merge",
    )(h2, attn, rnn, cnv, g, wg, wa, wr, wc, wo)


def _ffn_kernel(h_ref, g_ref, wg_ref, wu_ref, wd_ref, out_ref):
    h = h_ref[...]
    f = _rms(h, g_ref[...]).astype(BF16)
    gate = jnp.dot(f, wg_ref[...], preferred_element_type=F32)
    up = jnp.dot(f, wu_ref[...], preferred_element_type=F32)
    act = (jax.nn.silu(gate) * up).astype(BF16)
    out_ref[...] = h + jnp.dot(act, wd_ref[...], preferred_element_type=F32)


def _ffn(h2, g, wg, wu, wd, tm):
    rows = h2.shape[0]
    dff = wg.shape[1]
    row = pl.BlockSpec((tm, D_MODEL), lambda i: (i, 0))
    return pl.pallas_call(
        _ffn_kernel,
        grid=(rows // tm,),
        in_specs=[row, _const_spec((1, D_MODEL)), _const_spec((D_MODEL, dff)), _const_spec((D_MODEL, dff)),
                  _const_spec((dff, D_MODEL))],
        out_specs=row,
        out_shape=jax.ShapeDtypeStruct((rows, D_MODEL), F32),
        compiler_params=pltpu.CompilerParams(dimension_semantics=("arbitrary",),
                                             vmem_limit_bytes=VMEM_LIMIT),
        name="ffn",
    )(h2, g, wg, wu, wd)


def _rope_tables(n, dim):
    inv = ROPE_THETA ** (-jnp.arange(0, dim, 2, dtype=F32) / dim)
    ang = jnp.arange(n, dtype=F32)[:, None] * inv[None, :]
    return jnp.cos(ang), jnp.sin(ang)


def kernel(x, meta, mix_norm_g, w_in, q_norm_g, k_norm_g, rnn_conv_w, rnn_conv_b, rnn_wa, rnn_ba, rnn_wx, rnn_bx, rnn_lambda, conv_dw_w, conv_dw_b, conv_ln_g, conv_ln_b, w_o_attn, w_o_rnn, w_o_conv, w_out, ffn_norm_g, w_ffn_gate, w_ffn_up, w_ffn_down):
    bsz, seq, _ = x.shape
    depth = w_in.shape[0]
    t_valid = seq + N_META
    tp = -(-t_valid // LANES) * LANES
    topk = min(TOPK_MAX, seq // 4)
    rows = bsz * tp
    tm_seq = _pick_tile(tp, (384, 256, 128))
    tm_flat = _pick_tile(rows, (512, 384, 256, 128))

    q_w, kv_w, qi_w = N_HEADS * HEAD_DIM, N_KV_HEADS * HEAD_DIM, IDX_HEADS * IDX_DIM
    o = np.cumsum([0, q_w, kv_w, kv_w, qi_w, IDX_DIM, IDX_HEADS, D_MODEL, D_MODEL, 2 * D_MODEL, 3 * D_MODEL])
    pad_w = LANES - IDX_DIM - IDX_HEADS
    w_attn = jnp.concatenate([w_in[:, :, o[0]:o[6]], jnp.zeros((depth, D_MODEL, pad_w), w_in.dtype)],
                             axis=2).astype(BF16)
    w_rnn = w_in[:, :, o[6]:o[8]].astype(BF16)
    w_cnv = w_in[:, :, o[8]:o[9]].astype(BF16)
    w_gate = w_in[:, :, o[9]:o[10]].astype(BF16)
    wa_b, wx_b = rnn_wa.astype(BF16), rnn_wx.astype(BF16)
    w_oa, w_or, w_oc, w_ot = (a.astype(BF16) for a in (w_o_attn, w_o_rnn, w_o_conv, w_out))
    w_fg, w_fu, w_fd = (a.astype(BF16) for a in (w_ffn_gate, w_ffn_up, w_ffn_down))

    cos_a, sin_a = _rope_tables(tp, HEAD_DIM)
    cos_i, sin_i = _rope_tables(tp, IDX_DIM)
    tabs = (jnp.concatenate([cos_a, cos_a], axis=1), jnp.concatenate([-sin_a, sin_a], axis=1),
            jnp.concatenate([cos_i] * 4, axis=1), jnp.concatenate([-sin_i, sin_i] * 2, axis=1))

    h = jnp.concatenate([jnp.broadcast_to(meta[None].astype(x.dtype), (bsz, N_META, D_MODEL)), x,
                         jnp.zeros((bsz, tp - t_valid, D_MODEL), x.dtype)], axis=1)
    vec = lambda a: a.reshape(1, -1)

    for l in range(depth):
        h2 = h.reshape(rows, D_MODEL)
        q, k, vt, qi, ki, wi = _attn_proj(h2, vec(mix_norm_g[l]), w_attn[l], vec(q_norm_g[l]),
                                          vec(k_norm_g[l]), tabs, tm_seq, tp // tm_seq)
        seq3 = lambda a: a.reshape(bsz, tp, a.shape[-1])
        attn = _dsa_attention(seq3(q), seq3(qi), seq3(wi), seq3(ki), seq3(k),
                              vt.reshape(bsz, tp // tm_seq, N_KV_HEADS * VT_ROWS, tm_seq),
                              t_valid, topk, tm_seq)
        attn = attn.reshape(rows, N_HEADS * HEAD_DIM)

        rnn = _rglru(h, vec(mix_norm_g[l]), w_rnn[l], rnn_conv_w[l], vec(rnn_conv_b[l]), wa_b[l],
                     vec(rnn_ba[l]), wx_b[l], vec(rnn_bx[l]), vec(rnn_lambda[l]), tm_seq)
        cnv = _conformer(h, vec(mix_norm_g[l]), w_cnv[l], conv_dw_w[l], vec(conv_dw_b[l]),
                         vec(conv_ln_g[l]), vec(conv_ln_b[l]), tm_seq)
        h2 = _merge(h2, attn, rnn.reshape(rows, D_MODEL), cnv.reshape(rows, D_MODEL), vec(mix_norm_g[l]),
                    w_gate[l], w_oa[l], w_or[l], w_oc[l], w_ot[l], tm_flat)
        h2 = _ffn(h2, vec(ffn_norm_g[l]), w_fg[l], w_fu[l], w_fd[l], tm_flat)
        h = h2.reshape(bsz, tp, D_MODEL)

    return h[:, N_META:t_valid]
```

```python
import functools

import jax
import jax.numpy as jnp
import numpy as np
from jax import lax
from jax.experimental import pallas as pl
from jax.experimental.pallas import tpu as pltpu

D_MODEL = 1024
CHUNK = 64
N_META = 16
N_HEADS = 8
N_KV_HEADS = 2
HEAD_DIM = 128
KV_GROUP = N_HEADS // N_KV_HEADS
IDX_HEADS = 8
IDX_DIM = 64
TOPK_MAX = 256
ROPE_THETA = 10000.0
RNN_BLOCKS = 8
RNN_BLOCK_DIM = D_MODEL // RNN_BLOCKS
RNN_CONV = 4
LRU_C = 8.0
CONV_KERNEL = 31
NORM_EPS = 1e-6

LANES = 128
SUBLANES = 8
PACKED_ROWS = 16
TQ = 256
VMEM_LIMIT = 56 * 1024 * 1024
LOG2E = 1.4426950408889634

HALF_BITS = 16
HALF_MASK = 2 ** HALF_BITS - 1
I16_MIN = -(2 ** (HALF_BITS - 1))
I32_MAX = 2 ** 31 - 1
KEY_MASKED = -2139095041
NEG_BIG = -(2.0 ** 100)
VT_ROWS = HEAD_DIM + PACKED_ROWS

F32 = jnp.float32
BF16 = jnp.bfloat16


def _rms(x, g):
    return x * lax.rsqrt(jnp.mean(x * x, axis=-1, keepdims=True) + NORM_EPS) * g


def _expm1(y, u):
    near = jnp.where(u == 1.0, y, (u - 1.0) * y / jnp.log(jnp.where(u == 1.0, 2.0, u)))
    return jnp.where(y > -0.5, near, u - 1.0)


def _const_spec(shape):
    nd = len(shape)
    return pl.BlockSpec(shape, lambda *_: (0,) * nd, pipeline_mode=pl.Buffered(1))


def _pick_tile(n, candidates):
    for c in candidates:
        if n % c == 0:
            return c
    raise ValueError(f"no tile for {n}")


def _attn_proj_kernel(x_ref, g_ref, w_ref, qg_ref, kg_ref, cosa_ref, sina_ref, cosi_ref, sini_ref,
                      q_ref, k_ref, vt_ref, qi_ref, ki_ref, wi_ref, *, idx_scale):
    n = _rms(x_ref[...], g_ref[...]).astype(BF16)
    cosa, sina = cosa_ref[...], sina_ref[...]
    cosi, sini = cosi_ref[...], sini_ref[...]
    lane = lax.broadcasted_iota(jnp.int32, cosi.shape, 1)
    low_half = (lane % IDX_DIM) < (IDX_DIM // 2)

    def rot_attn(a, g):
        a = _rms(a, g)
        return a * cosa + pltpu.roll(a, HEAD_DIM // 2, axis=1) * sina

    def rot_idx(a):
        partner = jnp.where(low_half, pltpu.roll(a, LANES - IDX_DIM // 2, axis=1),
                            pltpu.roll(a, IDX_DIM // 2, axis=1))
        return a * cosi + partner * sini

    q_w = N_HEADS * HEAD_DIM
    kv_w = N_KV_HEADS * HEAD_DIM
    qi_w = IDX_HEADS * IDX_DIM
    o_k, o_v, o_qi, o_ki = q_w, q_w + kv_w, q_w + 2 * kv_w, q_w + 2 * kv_w + qi_w

    q_scale = (HEAD_DIM ** -0.5) * LOG2E
    pq = jnp.dot(n, w_ref[:, 0:q_w], preferred_element_type=F32)
    for h in range(N_HEADS):
        sl = slice(h * HEAD_DIM, (h + 1) * HEAD_DIM)
        q_ref[:, sl] = (rot_attn(pq[:, sl], qg_ref[...]) * q_scale).astype(BF16)
    pk = jnp.dot(n, w_ref[:, o_k:o_qi], preferred_element_type=F32)
    for h in range(N_KV_HEADS):
        sl = slice(h * HEAD_DIM, (h + 1) * HEAD_DIM)
        k_ref[:, sl] = rot_attn(pk[:, sl], kg_ref[...]).astype(BF16)
        vt_ref[h * VT_ROWS:h * VT_ROWS + HEAD_DIM, :] = (
            pk[:, kv_w + h * HEAD_DIM:kv_w + (h + 1) * HEAD_DIM].T.astype(BF16))
        vt_ref[h * VT_ROWS + HEAD_DIM:(h + 1) * VT_ROWS, :] = jnp.ones((PACKED_ROWS, pk.shape[0]), BF16)
    pi = jnp.dot(n, w_ref[:, o_qi:], preferred_element_type=F32)
    for c in range(qi_w // LANES):
        sl = slice(c * LANES, (c + 1) * LANES)
        qi_ref[:, sl] = rot_idx(pi[:, sl]).astype(BF16)
    last = pi[:, qi_w:]
    ki = rot_idx(last)
    ki_ref[...] = jnp.where(lane < IDX_DIM, ki, pltpu.roll(ki, IDX_DIM, axis=1)).astype(BF16)
    wi_ref[...] = last * idx_scale


def _attn_proj(h2, g, w, qg, kg, tabs, tm, tiles_per_seq):
    rows = h2.shape[0]
    wcols = w.shape[1]
    row = lambda width: pl.BlockSpec((tm, width), lambda i: (i, 0))
    tab = pl.BlockSpec((tm, LANES), lambda i: (i % tiles_per_seq, 0))
    idx_scale = (IDX_HEADS ** -0.5) * (IDX_DIM ** -0.5)
    return pl.pallas_call(
        functools.partial(_attn_proj_kernel, idx_scale=idx_scale),
        grid=(rows // tm,),
        in_specs=[row(D_MODEL), _const_spec((1, D_MODEL)), _const_spec((D_MODEL, wcols)),
                  _const_spec((1, HEAD_DIM)), _const_spec((1, HEAD_DIM)), tab, tab, tab, tab],
        out_specs=[row(N_HEADS * HEAD_DIM), row(N_KV_HEADS * HEAD_DIM),
                   pl.BlockSpec((None, N_KV_HEADS * VT_ROWS, tm), lambda i: (i, 0, 0)),
                   row(IDX_HEADS * IDX_DIM), row(LANES), row(LANES)],
        out_shape=[jax.ShapeDtypeStruct((rows, N_HEADS * HEAD_DIM), BF16),
                   jax.ShapeDtypeStruct((rows, N_KV_HEADS * HEAD_DIM), BF16),
                   jax.ShapeDtypeStruct((rows // tm, N_KV_HEADS * VT_ROWS, tm), BF16),
                   jax.ShapeDtypeStruct((rows, IDX_HEADS * IDX_DIM), BF16),
                   jax.ShapeDtypeStruct((rows, LANES), BF16),
                   jax.ShapeDtypeStruct((rows, LANES), F32)],
        compiler_params=pltpu.CompilerParams(dimension_semantics=("arbitrary",),
                                             vmem_limit_bytes=VMEM_LIMIT),
        name="attn_proj",
    )(h2, g, w, qg, kg, *tabs)


def _dsa_kernel(q_ref, qi_ref, wi_ref, ki_ref, k_ref, vt_ref, out_ref,
                key_ref, hi_ref, lo_ref, bias_ref, qt_ref, qit_ref, acc_ref, m_ref, sa_ref, sb_ref,
                *, n_valid, topk, tk):
    tq = TQ
    gw = KV_GROUP * tq
    t0 = pl.program_id(1) * tq
    tcol = t0 + lax.broadcasted_iota(jnp.int32, (1, tq), 1)
    kb = jnp.where(tcol < N_META, N_META,
                   N_META + CHUNK * (1 + jnp.right_shift(tcol - N_META, CHUNK.bit_length() - 1)))
    kb = jnp.where(tcol < n_valid, jnp.minimum(kb, n_valid), n_valid)
    kext = jnp.minimum(n_valid, N_META + CHUNK * (1 + (t0 + tq - 1 - N_META) // CHUNK))
    n_tiles = (kext + tk - 1) // tk

    for h in range(N_HEADS):
        qh = q_ref[:, h * HEAD_DIM:(h + 1) * HEAD_DIM].astype(F32).T
        qt_ref[h // KV_GROUP, :, (h % KV_GROUP) * tq:(h % KV_GROUP + 1) * tq] = qh.astype(BF16)
    row = lax.broadcasted_iota(jnp.int32, (LANES, tq), 0)
    for hp in range(IDX_HEADS // 2):
        pair = qi_ref[:, hp * LANES:(hp + 1) * LANES].astype(F32).T
        qit_ref[:, (2 * hp) * tq:(2 * hp + 1) * tq] = jnp.where(row < IDX_DIM, pair, 0.0).astype(BF16)
        qit_ref[:, (2 * hp + 1) * tq:(2 * hp + 2) * tq] = jnp.where(row >= IDX_DIM, pair, 0.0).astype(BF16)
    w = wi_ref[...].T[IDX_DIM:IDX_DIM + IDX_HEADS, :]

    def score_tile(i):
        r0 = pl.multiple_of(i * tk, tk)
        kt = ki_ref[pl.ds(r0, tk), :]
        acc = jnp.zeros((tk, tq), F32)
        for hp in range(IDX_HEADS // 2):
            lg = jnp.dot(kt, qit_ref[:, hp * 2 * tq:(hp + 1) * 2 * tq],
                         preferred_element_type=F32)
            for hh in range(2):
                h = 2 * hp + hh
                acc = acc + jnp.maximum(lg[:, hh * tq:(hh + 1) * tq], 0.0) * w[h:h + 1, :]
        acc = jnp.where(acc == 0.0, 0.0, acc)
        pos = r0 + lax.broadcasted_iota(jnp.int32, (tk, tq), 0)
        bits = pltpu.bitcast(acc, jnp.int32)
        key = jnp.where(bits >= 0, bits, bits ^ I32_MAX)
        key = jnp.where(pos < kb, key, KEY_MASKED)
        key_ref[pl.ds(r0, tk), :] = key
        hi_ref[pl.ds(r0, tk), :] = lax.shift_right_arithmetic(key, HALF_BITS).astype(jnp.int16)

    def score_pair(i, carry):
        score_tile(2 * i)
        score_tile(jnp.minimum(2 * i + 1, n_tiles - 1))
        return carry

    lax.fori_loop(0, (n_tiles + 1) // 2, score_pair, 0)

    def search16(ref, kth):
        def count_ge(cand):
            c16 = cand.astype(jnp.int16)

            def body(i, acc):
                r0 = pl.multiple_of(i * tk, tk)
                ind = jnp.where(ref[pl.ds(r0, tk), :] >= c16, jnp.bfloat16(1), jnp.bfloat16(0))
                parts = [ind[r * PACKED_ROWS:(r + 1) * PACKED_ROWS] for r in range(tk // PACKED_ROWS)]
                while len(parts) > 1:
                    nxt = [parts[r] + parts[r + 1] for r in range(0, len(parts) - 1, 2)]
                    parts = nxt + ([parts[-1]] if len(parts) % 2 else [])
                return acc + parts[0].astype(F32)

            acc = lax.fori_loop(0, n_tiles, body, jnp.zeros((PACKED_ROWS, tq), F32))
            return acc.sum(axis=0, keepdims=True)

        def step(it, carry):
            lo, c_lo = carry
            cand = lo + jnp.left_shift(jnp.int32(1), HALF_BITS - 1 - it)
            c = count_ge(cand)
            return jnp.where(c >= kth, cand, lo), jnp.where(c >= kth, c, c_lo)

        start = (jnp.full((1, tq), I16_MIN, jnp.int32), jnp.full((1, tq), n_tiles * tk, jnp.int32).astype(F32))
        return lax.fori_loop(0, HALF_BITS, step, start)

    tau_hi, _ = search16(hi_ref, jnp.full((1, tq), float(topk), F32))

    def low_tile(i, acc):
        r0 = pl.multiple_of(i * tk, tk)
        blk = key_ref[pl.ds(r0, tk), :]
        hi = lax.shift_right_arithmetic(blk, HALF_BITS)
        low = jnp.where(hi == tau_hi, (blk & HALF_MASK) + I16_MIN, I16_MIN)
        lo_ref[pl.ds(r0, tk), :] = low.astype(jnp.int16)
        above = jnp.where(hi > tau_hi, 1, 0).astype(jnp.int32)
        return acc + above.reshape(tk // SUBLANES, SUBLANES, tq).sum(axis=0)

    c_above = lax.fori_loop(0, n_tiles, low_tile, jnp.zeros((SUBLANES, tq), jnp.int32))
    c_above = c_above.sum(axis=0, keepdims=True)
    c_above = c_above.astype(F32)
    tau_lo, c_low = search16(lo_ref, topk - c_above)
    tau = tau_hi * (HALF_MASK + 1) + (tau_lo - I16_MIN)
    cut = jnp.max(c_above + c_low) > topk

    @pl.when(jnp.logical_not(cut))
    def _():
        def bias_tile(i, carry):
            r0 = pl.multiple_of(i * tk, tk)
            pos = r0 + lax.broadcasted_iota(jnp.int32, (tk, tq), 0)
            keep = (key_ref[pl.ds(r0, tk), :] >= tau) & (pos < kb)
            bias_ref[pl.ds(r0, tk), :] = jnp.where(keep, 0.0, NEG_BIG).astype(BF16)
            return carry

        lax.fori_loop(0, n_tiles, bias_tile, 0)

    @pl.when(cut)
    def _():
        def greater_tile(i, acc):
            r0 = pl.multiple_of(i * tk, tk)
            ind = jnp.where(key_ref[pl.ds(r0, tk), :] > tau, 1, 0).astype(jnp.int32)
            return acc + ind.reshape(tk // SUBLANES, SUBLANES, tq).sum(axis=0)

        c_gt = lax.fori_loop(0, n_tiles, greater_tile, jnp.zeros((SUBLANES, tq), jnp.int32))
        rank = (topk - c_gt.sum(axis=0, keepdims=True)).astype(F32)
        tri = jnp.where(lax.broadcasted_iota(jnp.int32, (tk, tk), 0)
                        >= lax.broadcasted_iota(jnp.int32, (tk, tk), 1), 1.0, 0.0).astype(BF16)

        def bias_tile(i, seen):
            r0 = pl.multiple_of(i * tk, tk)
            blk = key_ref[pl.ds(r0, tk), :]
            pos = r0 + lax.broadcasted_iota(jnp.int32, (tk, tq), 0)
            eq = blk == tau
            nth = seen + jnp.dot(tri, jnp.where(eq, 1.0, 0.0).astype(BF16), preferred_element_type=F32)
            keep = ((blk > tau) | (eq & (nth <= rank))) & (pos < kb)
            bias_ref[pl.ds(r0, tk), :] = jnp.where(keep, 0.0, NEG_BIG).astype(BF16)
            return nth[tk - 1:tk, :]

        lax.fori_loop(0, n_tiles, bias_tile, jnp.zeros((1, tq), F32))

    acc_ref[...] = jnp.zeros_like(acc_ref)
    m_ref[...] = jnp.full(m_ref.shape, NEG_BIG, F32)

    def scores(t, s_ref):
        r0 = pl.multiple_of(t * tk, tk)
        for n in range(N_KV_HEADS):
            kt = k_ref[pl.ds(r0, tk), n * HEAD_DIM:(n + 1) * HEAD_DIM]
            s_ref[n] = jnp.dot(kt, qt_ref[n], preferred_element_type=F32).astype(BF16)

    def softmax_pv(t, s_ref):
        r0 = pl.multiple_of(t * tk, tk)
        bt = bias_ref[pl.ds(r0, tk), :]
        bt = jnp.concatenate([bt] * KV_GROUP, axis=1)
        for n in range(N_KV_HEADS):
            s = s_ref[n] + bt
            m = m_ref[n]
            m_new = jnp.maximum(m, jnp.max(s, axis=0, keepdims=True).astype(F32))
            alpha = jnp.exp2(m - m_new)
            p = jnp.exp2(s - m_new.astype(BF16))
            m_ref[n] = m_new
            vt = vt_ref[t, n * VT_ROWS:(n + 1) * VT_ROWS, :]
            acc_ref[n] = alpha * acc_ref[n] + jnp.dot(vt, p, preferred_element_type=F32)

    last_tile = k_ref.shape[0] // tk - 1
    scores(0, sa_ref)

    def tile_pair(i, carry):
        t = 2 * i
        scores(t + 1, sb_ref)
        softmax_pv(t, sa_ref)
        scores(jnp.minimum(t + 2, last_tile), sa_ref)
        softmax_pv(t + 1, sb_ref)
        return carry

    lax.fori_loop(0, n_tiles // 2, tile_pair, 0)

    @pl.when(n_tiles % 2 == 1)
    def _():
        softmax_pv(n_tiles - 1, sa_ref)

    for h in range(N_HEADS):
        n, g = h // KV_GROUP, h % KV_GROUP
        inv_l = 1.0 / acc_ref[n, HEAD_DIM:HEAD_DIM + 1, g * tq:(g + 1) * tq]
        oh = acc_ref[n, 0:HEAD_DIM, g * tq:(g + 1) * tq] * inv_l
        out_ref[:, h * HEAD_DIM:(h + 1) * HEAD_DIM] = oh.T.astype(BF16)


def _dsa_attention(q, qi, wi, ki, k, vt, n_valid, topk, tk):
    bsz, tp, _ = q.shape
    gw = KV_GROUP * TQ
    qrow = lambda width: pl.BlockSpec((None, TQ, width), lambda b, j: (b, j, 0))
    seq = lambda width: pl.BlockSpec((None, tp, width), lambda b, j: (b, 0, 0))
    return pl.pallas_call(
        functools.partial(_dsa_kernel, n_valid=n_valid, topk=topk, tk=tk),
        grid=(bsz, pl.cdiv(tp, TQ)),
        in_specs=[qrow(N_HEADS * HEAD_DIM), qrow(IDX_HEADS * IDX_DIM), qrow(LANES), seq(LANES),
                  seq(N_KV_HEADS * HEAD_DIM),
                  pl.BlockSpec((None, tp // tk, N_KV_HEADS * VT_ROWS, tk), lambda b, j: (b, 0, 0, 0))],
        out_specs=qrow(N_HEADS * HEAD_DIM),
        out_shape=jax.ShapeDtypeStruct((bsz, tp, N_HEADS * HEAD_DIM), BF16),
        scratch_shapes=[pltpu.VMEM((tp, TQ), jnp.int32), pltpu.VMEM((tp, TQ), jnp.int16),
                        pltpu.VMEM((tp, TQ), jnp.int16), pltpu.VMEM((tp, TQ), BF16),
                        pltpu.VMEM((N_KV_HEADS, HEAD_DIM, gw), BF16),
                        pltpu.VMEM((LANES, IDX_HEADS * TQ), BF16),
                        pltpu.VMEM((N_KV_HEADS, VT_ROWS, gw), F32),
                        pltpu.VMEM((N_KV_HEADS, 1, gw), F32),
                        pltpu.VMEM((N_KV_HEADS, tk, gw), BF16), pltpu.VMEM((N_KV_HEADS, tk, gw), BF16)],
        compiler_params=pltpu.CompilerParams(dimension_semantics=("arbitrary", "arbitrary"),
                                             vmem_limit_bytes=VMEM_LIMIT),
        name="dsa_attention",
    )(q, qi, wi, ki, k, vt)


def _rglru_kernel(h_ref, g_ref, w_ref, cw_ref, cb_ref, wa_ref, ba_ref, wx_ref, bx_ref, lam_ref,
                  out_ref, xext_ref, a_ref, b_ref, state_ref):
    tm = h_ref.shape[0]
    hist = SUBLANES

    @pl.when(pl.program_id(1) == 0)
    def _():
        xext_ref[0:hist, :] = jnp.zeros((hist, D_MODEL), F32)
        state_ref[...] = jnp.zeros_like(state_ref)

    n = _rms(h_ref[...], g_ref[...]).astype(BF16)
    xext_ref[hist:hist + tm, :] = jnp.dot(n, w_ref[:, 0:D_MODEL], preferred_element_type=F32)
    u = cb_ref[...] + cw_ref[0:1, :] * xext_ref[hist - RNN_CONV + 1:hist - RNN_CONV + 1 + tm, :]
    for kk in range(1, RNN_CONV):
        o = hist - RNN_CONV + 1 + kk
        u = u + cw_ref[kk:kk + 1, :] * xext_ref[o:o + tm, :]
    xext_ref[0:hist, :] = xext_ref[tm:tm + hist, :]

    ub = u.astype(BF16)
    rs, is_ = [], []
    for blk in range(RNN_BLOCKS):
        sl = slice(blk * RNN_BLOCK_DIM, (blk + 1) * RNN_BLOCK_DIM)
        rs.append(jnp.dot(ub[:, sl], wa_ref[blk], preferred_element_type=F32))
        is_.append(jnp.dot(ub[:, sl], wx_ref[blk], preferred_element_type=F32))
    r = jax.nn.sigmoid(jnp.concatenate(rs, axis=1) + ba_ref[...])
    ig = jax.nn.sigmoid(jnp.concatenate(is_, axis=1) + bx_ref[...])
    nl = -lam_ref[...]
    softplus = jnp.maximum(nl, 0.0) + jnp.log1p(jnp.exp(-jnp.abs(nl)))
    log_a = -LRU_C * r * softplus
    a_all = jnp.exp(log_a)
    b_all = jnp.sqrt(-_expm1(2.0 * log_a, a_all * a_all)) * (ig * u)
    nchunk = D_MODEL // LANES
    for c in range(nchunk):
        a_ref[c] = a_all[:, c * LANES:(c + 1) * LANES]
        b_ref[c] = b_all[:, c * LANES:(c + 1) * LANES]

    seg = tm // SCAN_SEGMENTS
    groups = SCAN_SEGMENTS // SUBLANES

    def step(i, carry):
        out = []
        for v in range(groups):
            prod, loc = carry[2 * v], carry[2 * v + 1]
            rows = pl.ds(v * SUBLANES * seg + i, SUBLANES, stride=seg)
            a = a_ref[:, rows, :]
            prod = a * prod
            loc = a * loc + b_ref[:, rows, :]
            a_ref[:, rows, :] = prod
            b_ref[:, rows, :] = loc
            out += [prod, loc]
        return tuple(out)

    init = (jnp.ones((nchunk, SUBLANES, LANES), F32), jnp.zeros((nchunk, SUBLANES, LANES), F32)) * groups
    ends = lax.fori_loop(0, seg, step, init)
    gel = jax.nn.gelu(jnp.dot(n, w_ref[:, D_MODEL:], preferred_element_type=F32))
    hcur = state_ref[...]
    for s in range(SCAN_SEGMENTS):
        rs = slice(s * seg, (s + 1) * seg)
        hs = a_ref[:, rs, :] * hcur + b_ref[:, rs, :]
        for c in range(nchunk):
            cs = slice(c * LANES, (c + 1) * LANES)
            out_ref[rs, cs] = (hs[c] * gel[rs, cs]).astype(BF16)
        v, r = s // SUBLANES, s % SUBLANES
        hcur = ends[2 * v][:, r:r + 1, :] * hcur + ends[2 * v + 1][:, r:r + 1, :]
    state_ref[...] = hcur


def _rglru(h3, g, w, cw, cb, wa, ba, wx, bx, lam, tm):
    bsz, tp, _ = h3.shape
    row = pl.BlockSpec((None, tm, D_MODEL), lambda b, t: (b, t, 0))
    vec = _const_spec((1, D_MODEL))
    blkw = _const_spec((RNN_BLOCKS, RNN_BLOCK_DIM, RNN_BLOCK_DIM))
    return pl.pallas_call(
        _rglru_kernel,
        grid=(bsz, tp // tm),
        in_specs=[row, vec, _const_spec((D_MODEL, 2 * D_MODEL)), _const_spec((RNN_CONV, D_MODEL)), vec,
                  blkw, vec, blkw, vec, vec],
        out_specs=row,
        out_shape=jax.ShapeDtypeStruct((bsz, tp, D_MODEL), BF16),
        scratch_shapes=[pltpu.VMEM((tm + SUBLANES, D_MODEL), F32),
                        pltpu.VMEM((D_MODEL // LANES, tm, LANES), F32),
                        pltpu.VMEM((D_MODEL // LANES, tm, LANES), F32),
                        pltpu.VMEM((D_MODEL // LANES, 1, LANES), F32)],
        compiler_params=pltpu.CompilerParams(dimension_semantics=("arbitrary", "arbitrary"),
                                             vmem_limit_bytes=VMEM_LIMIT),
        name="rglru",
    )(h3, g, w, cw, cb, wa, ba, wx, bx, lam)


SCAN_SEGMENTS = 16
CONV_HIST = 32
CONV_ROWS = 128


def _conformer_kernel(h_ref, g_ref, w_ref, dw_ref, db_ref, lg_ref, lb_ref, out_ref, xs_ref, y_ref):
    tm = h_ref.shape[0]

    @pl.when(pl.program_id(1) == 0)
    def _():
        xs_ref[0, 0:CONV_HIST, :] = jnp.zeros((CONV_HIST, D_MODEL), F32)

    n = _rms(h_ref[...], g_ref[...]).astype(BF16)
    a = jnp.dot(n, w_ref[:, 0:D_MODEL], preferred_element_type=F32)
    gate = jnp.dot(n, w_ref[:, D_MODEL:], preferred_element_type=F32)
    xs_ref[0, CONV_HIST:CONV_HIST + tm, :] = a * jax.nn.sigmoid(gate)
    span = tm + CONV_HIST - SUBLANES
    for s in range(1, SUBLANES):
        xs_ref[s, 0:span, :] = xs_ref[0, s:s + span, :]

    first = CONV_HIST - CONV_KERNEL + 1
    row_chunks = range(0, tm, CONV_ROWS)
    for c0 in range(0, D_MODEL, LANES):
        cs = slice(c0, c0 + LANES)
        accs = [jnp.broadcast_to(db_ref[:, cs], (CONV_ROWS, LANES)) for _ in row_chunks]
        for kk in range(CONV_KERNEL):
            shift = (first + kk) % SUBLANES
            base = first + kk - shift
            wk = dw_ref[kk:kk + 1, cs]
            for ri, r0 in enumerate(row_chunks):
                accs[ri] = accs[ri] + wk * xs_ref[shift, base + r0:base + r0 + CONV_ROWS, cs]
        for ri, r0 in enumerate(row_chunks):
            y_ref[r0:r0 + CONV_ROWS, cs] = accs[ri]
    xs_ref[0, 0:CONV_HIST, :] = xs_ref[0, tm:tm + CONV_HIST, :]

    y = y_ref[...]
    mu = jnp.mean(y, axis=-1, keepdims=True)
    yc = y - mu
    z = yc * lax.rsqrt(jnp.mean(yc * yc, axis=-1, keepdims=True) + NORM_EPS) * lg_ref[...] + lb_ref[...]
    out_ref[...] = jax.nn.silu(z).astype(BF16)


def _conformer(h3, g, w, dw, db, lg, lb, tm):
    bsz, tp, _ = h3.shape
    row = pl.BlockSpec((None, tm, D_MODEL), lambda b, t: (b, t, 0))
    vec = _const_spec((1, D_MODEL))
    return pl.pallas_call(
        _conformer_kernel,
        grid=(bsz, tp // tm),
        in_specs=[row, vec, _const_spec((D_MODEL, 2 * D_MODEL)), _const_spec((CONV_KERNEL, D_MODEL)),
                  vec, vec, vec],
        out_specs=row,
        out_shape=jax.ShapeDtypeStruct((bsz, tp, D_MODEL), BF16),
        scratch_shapes=[pltpu.VMEM((SUBLANES, tm + CONV_HIST, D_MODEL), F32), pltpu.VMEM((tm, D_MODEL), F32)],
        compiler_params=pltpu.CompilerParams(dimension_semantics=("arbitrary", "arbitrary"),
                                             vmem_limit_bytes=VMEM_LIMIT),
        name="conformer",
    )(h3, g, w, dw, db, lg, lb)


def _merge_kernel(h_ref, at_ref, rn_ref, cv_ref, g_ref, wg_ref, wa_ref, wr_ref, wc_ref, wo_ref, out_ref):
    h = h_ref[...]
    n = _rms(h, g_ref[...]).astype(BF16)
    merged = None
    for i, (src, wref) in enumerate(((at_ref, wa_ref), (rn_ref, wr_ref), (cv_ref, wc_ref))):
        gate = jax.nn.sigmoid(jnp.dot(n, wg_ref[:, i * D_MODEL:(i + 1) * D_MODEL],
                                      preferred_element_type=F32))
        term = gate * jnp.dot(src[...], wref[...], preferred_element_type=F32)
        merged = term if merged is None else merged + term
    out_ref[...] = h + jnp.dot(merged.astype(BF16), wo_ref[...], preferred_element_type=F32)


def _merge(h2, attn, rnn, cnv, g, wg, wa, wr, wc, wo, tm):
    rows = h2.shape[0]
    row = pl.BlockSpec((tm, D_MODEL), lambda i: (i, 0))
    sq = _const_spec((D_MODEL, D_MODEL))
    return pl.pallas_call(
        _merge_kernel,
        grid=(rows // tm,),
        in_specs=[row, row, row, row, _const_spec((1, D_MODEL)), _const_spec((D_MODEL, 3 * D_MODEL)),
                  sq, sq, sq, sq],
        out_specs=row,
        out_shape=jax.ShapeDtypeStruct((rows, D_MODEL), F32),
        compiler_params=pltpu.CompilerParams(dimension_semantics=("arbitrary",),
                                             vmem_limit_bytes=VMEM_LIMIT),
        name="merge",
    )(h2, attn, rnn, cnv, g, wg, wa, wr, wc, wo)


def _ffn_kernel(h_ref, g_ref, wg_ref, wu_ref, wd_ref, out_ref):
    h = h_ref[...]
    f = _rms(h, g_ref[...]).astype(BF16)
    gate = jnp.dot(f, wg_ref[...], preferred_element_type=F32)
    up = jnp.dot(f, wu_ref[...], preferred_element_type=F32)
    act = (jax.nn.silu(gate) * up).astype(BF16)
    out_ref[...] = h + jnp.dot(act, wd_ref[...], preferred_element_type=F32)


def _ffn(h2, g, wg, wu, wd, tm):
    rows = h2.shape[0]
    dff = wg.shape[1]
    row = pl.BlockSpec((tm, D_MODEL), lambda i: (i, 0))
    return pl.pallas_call(
        _ffn_kernel,
        grid=(rows // tm,),
        in_specs=[row, _const_spec((1, D_MODEL)), _const_spec((D_MODEL, dff)), _const_spec((D_MODEL, dff)),
                  _const_spec((dff, D_MODEL))],
        out_specs=row,
        out_shape=jax.ShapeDtypeStruct((rows, D_MODEL), F32),
        compiler_params=pltpu.CompilerParams(dimension_semantics=("arbitrary",),
                                             vmem_limit_bytes=VMEM_LIMIT),
        name="ffn",
    )(h2, g, wg, wu, wd)


def _rope_tables(n, dim):
    inv = ROPE_THETA ** (-jnp.arange(0, dim, 2, dtype=F32) / dim)
    ang = jnp.arange(n, dtype=F32)[:, None] * inv[None, :]
    return jnp.cos(ang), jnp.sin(ang)


def kernel(x, meta, mix_norm_g, w_in, q_norm_g, k_norm_g, rnn_conv_w, rnn_conv_b, rnn_wa, rnn_ba, rnn_wx, rnn_bx, rnn_lambda, conv_dw_w, conv_dw_b, conv_ln_g, conv_ln_b, w_o_attn, w_o_rnn, w_o_conv, w_out, ffn_norm_g, w_ffn_gate, w_ffn_up, w_ffn_down):
    bsz, seq, _ = x.shape
    depth = w_in.shape[0]
    t_valid = seq + N_META
    tp = -(-t_valid // LANES) * LANES
    topk = min(TOPK_MAX, seq // 4)
    rows = bsz * tp
    tm_seq = _pick_tile(tp, (384, 256, 128))
    tm_flat = _pick_tile(rows, (512, 384, 256, 128))

    q_w, kv_w, qi_w = N_HEADS * HEAD_DIM, N_KV_HEADS * HEAD_DIM, IDX_HEADS * IDX_DIM
    o = np.cumsum([0, q_w, kv_w, kv_w, qi_w, IDX_DIM, IDX_HEADS, D_MODEL, D_MODEL, 2 * D_MODEL, 3 * D_MODEL])
    pad_w = LANES - IDX_DIM - IDX_HEADS
    w_attn = jnp.concatenate([w_in[:, :, o[0]:o[6]], jnp.zeros((depth, D_MODEL, pad_w), w_in.dtype)],
                             axis=2).astype(BF16)
    w_rnn = w_in[:, :, o[6]:o[8]].astype(BF16)
    w_cnv = w_in[:, :, o[8]:o[9]].astype(BF16)
    w_gate = w_in[:, :, o[9]:o[10]].astype(BF16)
    wa_b, wx_b = rnn_wa.astype(BF16), rnn_wx.astype(BF16)
    w_oa, w_or, w_oc, w_ot = (a.astype(BF16) for a in (w_o_attn, w_o_rnn, w_o_conv, w_out))
    w_fg, w_fu, w_fd = (a.astype(BF16) for a in (w_ffn_gate, w_ffn_up, w_ffn_down))

    cos_a, sin_a = _rope_tables(tp, HEAD_DIM)
    cos_i, sin_i = _rope_tables(tp, IDX_DIM)
    tabs = (jnp.concatenate([cos_a, cos_a], axis=1), jnp.concatenate([-sin_a, sin_a], axis=1),
            jnp.concatenate([cos_i] * 4, axis=1), jnp.concatenate([-sin_i, sin_i] * 2, axis=1))

    h = jnp.concatenate([jnp.broadcast_to(meta[None].astype(x.dtype), (bsz, N_META, D_MODEL)), x,
                         jnp.zeros((bsz, tp - t_valid, D_MODEL), x.dtype)], axis=1)
    vec = lambda a: a.reshape(1, -1)

    for l in range(depth):
        h2 = h.reshape(rows, D_MODEL)
        q, k, vt, qi, ki, wi = _attn_proj(h2, vec(mix_norm_g[l]), w_attn[l], vec(q_norm_g[l]),
                                          vec(k_norm_g[l]), tabs, tm_seq, tp // tm_seq)
        seq3 = lambda a: a.reshape(bsz, tp, a.shape[-1])
        attn = _dsa_attention(seq3(q), seq3(qi), seq3(wi), seq3(ki), seq3(k),
                              vt.reshape(bsz, tp // tm_seq, N_KV_HEADS * VT_ROWS, tm_seq),
                              t_valid, topk, tm_seq)
        attn = attn.reshape(rows, N_HEADS * HEAD_DIM)

        rnn = _rglru(h, vec(mix_norm_g[l]), w_rnn[l], rnn_conv_w[l], vec(rnn_conv_b[l]), wa_b[l],
                     vec(rnn_ba[l]), wx_b[l], vec(rnn_bx[l]), vec(rnn_lambda[l]), tm_seq)
        cnv = _conformer(h, vec(mix_norm_g[l]), w_cnv[l], conv_dw_w[l], vec(conv_dw_b[l]),
                         vec(conv_ln_g[l]), vec(conv_ln_b[l]), tm_seq)
        h2 = _merge(h2, attn, rnn.reshape(rows, D_MODEL), cnv.reshape(rows, D_MODEL), vec(mix_norm_g[l]),
                    w_gate[l], w_oa[l], w_or[l], w_oc[l], w_ot[l], tm_flat)
        h2 = _ffn(h2, vec(ffn_norm_g[l]), w_fg[l], w_fu[l], w_fd[l], tm_flat)
        h = h2.reshape(bsz, tp, D_MODEL)

    return h[:, N_META:t_valid]
```

```python
import functools

import jax
import jax.numpy as jnp
import numpy as np
from jax import lax
from jax.experimental import pallas as pl
from jax.experimental.pallas import tpu as pltpu

D_MODEL = 1024
CHUNK = 64
N_META = 16
N_HEADS = 8
N_KV_HEADS = 2
HEAD_DIM = 128
KV_GROUP = N_HEADS // N_KV_HEADS
IDX_HEADS = 8
IDX_DIM = 64
TOPK_MAX = 256
ROPE_THETA = 10000.0
RNN_BLOCKS = 8
RNN_BLOCK_DIM = D_MODEL // RNN_BLOCKS
RNN_CONV = 4
LRU_C = 8.0
CONV_KERNEL = 31
NORM_EPS = 1e-6

LANES = 128
SUBLANES = 8
PACKED_ROWS = 16
TQ = 384
VMEM_LIMIT = 56 * 1024 * 1024
LOG2E = 1.4426950408889634

HALF_BITS = 16
HALF_MASK = 2 ** HALF_BITS - 1
I16_MIN = -(2 ** (HALF_BITS - 1))
I32_MAX = 2 ** 31 - 1
KEY_MASKED = -2139095041
NEG_BIG = -(2.0 ** 100)
VT_ROWS = HEAD_DIM + PACKED_ROWS

F32 = jnp.float32
BF16 = jnp.bfloat16


def _rms(x, g):
    return x * lax.rsqrt(jnp.mean(x * x, axis=-1, keepdims=True) + NORM_EPS) * g


def _expm1(y, u):
    near = jnp.where(u == 1.0, y, (u - 1.0) * y / jnp.log(jnp.where(u == 1.0, 2.0, u)))
    return jnp.where(y > -0.5, near, u - 1.0)


def _const_spec(shape):
    nd = len(shape)
    return pl.BlockSpec(shape, lambda *_: (0,) * nd, pipeline_mode=pl.Buffered(1))


def _pick_tile(n, candidates):
    for c in candidates:
        if n % c == 0:
            return c
    raise ValueError(f"no tile for {n}")


def _attn_proj_kernel(x_ref, g_ref, w_ref, qg_ref, kg_ref, cosa_ref, sina_ref, cosi_ref, sini_ref,
                      q_ref, k_ref, vt_ref, qi_ref, ki_ref, wi_ref, *, idx_scale):
    n = _rms(x_ref[...], g_ref[...]).astype(BF16)
    cosa, sina = cosa_ref[...], sina_ref[...]
    cosi, sini = cosi_ref[...], sini_ref[...]
    lane = lax.broadcasted_iota(jnp.int32, cosi.shape, 1)
    low_half = (lane % IDX_DIM) < (IDX_DIM // 2)

    def rot_attn(a, g):
        a = _rms(a, g)
        return a * cosa + pltpu.roll(a, HEAD_DIM // 2, axis=1) * sina

    def rot_idx(a):
        partner = jnp.where(low_half, pltpu.roll(a, LANES - IDX_DIM // 2, axis=1),
                            pltpu.roll(a, IDX_DIM // 2, axis=1))
        return a * cosi + partner * sini

    q_w = N_HEADS * HEAD_DIM
    kv_w = N_KV_HEADS * HEAD_DIM
    qi_w = IDX_HEADS * IDX_DIM
    o_k, o_v, o_qi, o_ki = q_w, q_w + kv_w, q_w + 2 * kv_w, q_w + 2 * kv_w + qi_w

    q_scale = (HEAD_DIM ** -0.5) * LOG2E
    pq = jnp.dot(n, w_ref[:, 0:q_w], preferred_element_type=F32)
    for h in range(N_HEADS):
        sl = slice(h * HEAD_DIM, (h + 1) * HEAD_DIM)
        q_ref[:, sl] = (rot_attn(pq[:, sl], qg_ref[...]) * q_scale).astype(BF16)
    pk = jnp.dot(n, w_ref[:, o_k:o_qi], preferred_element_type=F32)
    for h in range(N_KV_HEADS):
        sl = slice(h * HEAD_DIM, (h + 1) * HEAD_DIM)
        k_ref[:, sl] = rot_attn(pk[:, sl], kg_ref[...]).astype(BF16)
        vt_ref[h * VT_ROWS:h * VT_ROWS + HEAD_DIM, :] = (
            pk[:, kv_w + h * HEAD_DIM:kv_w + (h + 1) * HEAD_DIM].T.astype(BF16))
        vt_ref[h * VT_ROWS + HEAD_DIM:(h + 1) * VT_ROWS, :] = jnp.ones((PACKED_ROWS, pk.shape[0]), BF16)
    pi = jnp.dot(n, w_ref[:, o_qi:], preferred_element_type=F32)
    for c in range(qi_w // LANES):
        sl = slice(c * LANES, (c + 1) * LANES)
        qi_ref[:, sl] = rot_idx(pi[:, sl]).astype(BF16)
    last = pi[:, qi_w:]
    ki = rot_idx(last)
    ki_ref[...] = jnp.where(lane < IDX_DIM, ki, pltpu.roll(ki, IDX_DIM, axis=1)).astype(BF16)
    wi_ref[...] = last * idx_scale


def _attn_proj(h2, g, w, qg, kg, tabs, tm, tiles_per_seq):
    rows = h2.shape[0]
    wcols = w.shape[1]
    row = lambda width: pl.BlockSpec((tm, width), lambda i: (i, 0))
    tab = pl.BlockSpec((tm, LANES), lambda i: (i % tiles_per_seq, 0))
    idx_scale = (IDX_HEADS ** -0.5) * (IDX_DIM ** -0.5)
    return pl.pallas_call(
        functools.partial(_attn_proj_kernel, idx_scale=idx_scale),
        grid=(rows // tm,),
        in_specs=[row(D_MODEL), _const_spec((1, D_MODEL)), _const_spec((D_MODEL, wcols)),
                  _const_spec((1, HEAD_DIM)), _const_spec((1, HEAD_DIM)), tab, tab, tab, tab],
        out_specs=[row(N_HEADS * HEAD_DIM), row(N_KV_HEADS * HEAD_DIM),
                   pl.BlockSpec((None, N_KV_HEADS * VT_ROWS, tm), lambda i: (i, 0, 0)),
                   row(IDX_HEADS * IDX_DIM), row(LANES), row(LANES)],
        out_shape=[jax.ShapeDtypeStruct((rows, N_HEADS * HEAD_DIM), BF16),
                   jax.ShapeDtypeStruct((rows, N_KV_HEADS * HEAD_DIM), BF16),
                   jax.ShapeDtypeStruct((rows // tm, N_KV_HEADS * VT_ROWS, tm), BF16),
                   jax.ShapeDtypeStruct((rows, IDX_HEADS * IDX_DIM), BF16),
                   jax.ShapeDtypeStruct((rows, LANES), BF16),
                   jax.ShapeDtypeStruct((rows, LANES), F32)],
        compiler_params=pltpu.CompilerParams(dimension_semantics=("arbitrary",),
                                             vmem_limit_bytes=VMEM_LIMIT),
        name="attn_proj",
    )(h2, g, w, qg, kg, *tabs)


def _dsa_kernel(q_ref, qi_ref, wi_ref, ki_ref, k_ref, vt_ref, out_ref,
                key_ref, hi_ref, lo_ref, bias_ref, qt_ref, qit_ref, acc_ref, m_ref, sa_ref, sb_ref,
                *, n_valid, topk, tk):
    tq = TQ
    gw = KV_GROUP * tq
    t0 = pl.program_id(1) * tq
    tcol = t0 + lax.broadcasted_iota(jnp.int32, (1, tq), 1)
    kb = jnp.where(tcol < N_META, N_META,
                   N_META + CHUNK * (1 + jnp.right_shift(tcol - N_META, CHUNK.bit_length() - 1)))
    kb = jnp.where(tcol < n_valid, jnp.minimum(kb, n_valid), n_valid)
    kext = jnp.minimum(n_valid, N_META + CHUNK * (1 + (t0 + tq - 1 - N_META) // CHUNK))
    n_tiles = (kext + tk - 1) // tk

    for h in range(N_HEADS):
        qh = q_ref[:, h * HEAD_DIM:(h + 1) * HEAD_DIM].astype(F32).T
        qt_ref[h // KV_GROUP, :, (h % KV_GROUP) * tq:(h % KV_GROUP + 1) * tq] = qh.astype(BF16)
    row = lax.broadcasted_iota(jnp.int32, (LANES, tq), 0)
    for hp in range(IDX_HEADS // 2):
        pair = qi_ref[:, hp * LANES:(hp + 1) * LANES].astype(F32).T
        qit_ref[:, (2 * hp) * tq:(2 * hp + 1) * tq] = jnp.where(row < IDX_DIM, pair, 0.0).astype(BF16)
        qit_ref[:, (2 * hp + 1) * tq:(2 * hp + 2) * tq] = jnp.where(row >= IDX_DIM, pair, 0.0).astype(BF16)
    w = wi_ref[...].T[IDX_DIM:IDX_DIM + IDX_HEADS, :]

    def score_tile(i, carry):
        r0 = pl.multiple_of(i * tk, tk)
        kt = ki_ref[pl.ds(r0, tk), :]
        acc = jnp.zeros((tk, tq), F32)
        for hp in range(IDX_HEADS // 2):
            lg = jnp.dot(kt, qit_ref[:, hp * 2 * tq:(hp + 1) * 2 * tq],
                         preferred_element_type=F32)
            for hh in range(2):
                h = 2 * hp + hh
                acc = acc + jnp.maximum(lg[:, hh * tq:(hh + 1) * tq], 0.0) * w[h:h + 1, :]
        acc = jnp.where(acc == 0.0, 0.0, acc)
        pos = r0 + lax.broadcasted_iota(jnp.int32, (tk, tq), 0)
        bits = pltpu.bitcast(acc, jnp.int32)
        key = jnp.where(bits >= 0, bits, bits ^ I32_MAX)
        key = jnp.where(pos < kb, key, KEY_MASKED)
        key_ref[pl.ds(r0, tk), :] = key
        hi_ref[pl.ds(r0, tk), :] = lax.shift_right_arithmetic(key, HALF_BITS).astype(jnp.int16)
        return carry

    lax.fori_loop(0, n_tiles, score_tile, 0)

    def search16(ref, kth):
        def count_ge(cand):
            c16 = cand.astype(jnp.int16)

            def body(i, acc):
                r0 = pl.multiple_of(i * tk, tk)
                ind = jnp.where(ref[pl.ds(r0, tk), :] >= c16, jnp.bfloat16(1), jnp.bfloat16(0))
                parts = [ind[r * PACKED_ROWS:(r + 1) * PACKED_ROWS] for r in range(tk // PACKED_ROWS)]
                while len(parts) > 1:
                    nxt = [parts[r] + parts[r + 1] for r in range(0, len(parts) - 1, 2)]
                    parts = nxt + ([parts[-1]] if len(parts) % 2 else [])
                return acc + parts[0].astype(F32)

            acc = lax.fori_loop(0, n_tiles, body, jnp.zeros((PACKED_ROWS, tq), F32))
            return acc.sum(axis=0, keepdims=True)

        def step(it, carry):
            lo, c_lo = carry
            cand = lo + jnp.left_shift(jnp.int32(1), HALF_BITS - 1 - it)
            c = count_ge(cand)
            return jnp.where(c >= kth, cand, lo), jnp.where(c >= kth, c, c_lo)

        start = (jnp.full((1, tq), I16_MIN, jnp.int32), jnp.full((1, tq), n_tiles * tk, jnp.int32).astype(F32))
        return lax.fori_loop(0, HALF_BITS, step, start)

    tau_hi, _ = search16(hi_ref, jnp.full((1, tq), float(topk), F32))

    def low_tile(i, acc):
        r0 = pl.multiple_of(i * tk, tk)
        blk = key_ref[pl.ds(r0, tk), :]
        hi = lax.shift_right_arithmetic(blk, HALF_BITS)
        low = jnp.where(hi == tau_hi, (blk & HALF_MASK) + I16_MIN, I16_MIN)
        lo_ref[pl.ds(r0, tk), :] = low.astype(jnp.int16)
        above = jnp.where(hi > tau_hi, 1, 0).astype(jnp.int32)
        return acc + above.reshape(tk // SUBLANES, SUBLANES, tq).sum(axis=0)

    c_above = lax.fori_loop(0, n_tiles, low_tile, jnp.zeros((SUBLANES, tq), jnp.int32))
    c_above = c_above.sum(axis=0, keepdims=True)
    c_above = c_above.astype(F32)
    tau_lo, c_low = search16(lo_ref, topk - c_above)
    tau = tau_hi * (HALF_MASK + 1) + (tau_lo - I16_MIN)
    cut = jnp.max(c_above + c_low) > topk

    @pl.when(jnp.logical_not(cut))
    def _():
        def bias_tile(i, carry):
            r0 = pl.multiple_of(i * tk, tk)
            pos = r0 + lax.broadcasted_iota(jnp.int32, (tk, tq), 0)
            keep = (key_ref[pl.ds(r0, tk), :] >= tau) & (pos < kb)
            bias_ref[pl.ds(r0, tk), :] = jnp.where(keep, 0.0, NEG_BIG).astype(BF16)
            return carry

        lax.fori_loop(0, n_tiles, bias_tile, 0)

    @pl.when(cut)
    def _():
        def greater_tile(i, acc):
            r0 = pl.multiple_of(i * tk, tk)
            ind = jnp.where(key_ref[pl.ds(r0, tk), :] > tau, 1, 0).astype(jnp.int32)
            return acc + ind.reshape(tk // SUBLANES, SUBLANES, tq).sum(axis=0)

        c_gt = lax.fori_loop(0, n_tiles, greater_tile, jnp.zeros((SUBLANES, tq), jnp.int32))
        rank = (topk - c_gt.sum(axis=0, keepdims=True)).astype(F32)
        tri = jnp.where(lax.broadcasted_iota(jnp.int32, (tk, tk), 0)
                        >= lax.broadcasted_iota(jnp.int32, (tk, tk), 1), 1.0, 0.0).astype(BF16)

        def bias_tile(i, seen):
            r0 = pl.multiple_of(i * tk, tk)
            blk = key_ref[pl.ds(r0, tk), :]
            pos = r0 + lax.broadcasted_iota(jnp.int32, (tk, tq), 0)
            eq = blk == tau
            nth = seen + jnp.dot(tri, jnp.where(eq, 1.0, 0.0).astype(BF16), preferred_element_type=F32)
            keep = ((blk > tau) | (eq & (nth <= rank))) & (pos < kb)
            bias_ref[pl.ds(r0, tk), :] = jnp.where(keep, 0.0, NEG_BIG).astype(BF16)
            return nth[tk - 1:tk, :]

        lax.fori_loop(0, n_tiles, bias_tile, jnp.zeros((1, tq), F32))

    acc_ref[...] = jnp.zeros_like(acc_ref)
    m_ref[...] = jnp.full(m_ref.shape, NEG_BIG, F32)

    def scores(t, s_ref):
        r0 = pl.multiple_of(t * tk, tk)
        for n in range(N_KV_HEADS):
            kt = k_ref[pl.ds(r0, tk), n * HEAD_DIM:(n + 1) * HEAD_DIM]
            s_ref[n] = jnp.dot(kt, qt_ref[n], preferred_element_type=F32).astype(BF16)

    def softmax_pv(t, s_ref):
        r0 = pl.multiple_of(t * tk, tk)
        bt = bias_ref[pl.ds(r0, tk), :]
        bt = jnp.concatenate([bt] * KV_GROUP, axis=1)
        for n in range(N_KV_HEADS):
            s = s_ref[n] + bt
            m = m_ref[n]
            m_new = jnp.maximum(m, jnp.max(s, axis=0, keepdims=True).astype(F32))
            alpha = jnp.exp2(m - m_new)
            p = jnp.exp2(s - m_new.astype(BF16))
            m_ref[n] = m_new
            vt = vt_ref[t, n * VT_ROWS:(n + 1) * VT_ROWS, :]
            acc_ref[n] = alpha * acc_ref[n] + jnp.dot(vt, p, preferred_element_type=F32)

    last_tile = k_ref.shape[0] // tk - 1
    scores(0, sa_ref)

    def tile_pair(i, carry):
        t = 2 * i
        scores(t + 1, sb_ref)
        softmax_pv(t, sa_ref)
        scores(jnp.minimum(t + 2, last_tile), sa_ref)
        softmax_pv(t + 1, sb_ref)
        return carry

    lax.fori_loop(0, n_tiles // 2, tile_pair, 0)

    @pl.when(n_tiles % 2 == 1)
    def _():
        softmax_pv(n_tiles - 1, sa_ref)

    for h in range(N_HEADS):
        n, g = h // KV_GROUP, h % KV_GROUP
        inv_l = 1.0 / acc_ref[n, HEAD_DIM:HEAD_DIM + 1, g * tq:(g + 1) * tq]
        oh = acc_ref[n, 0:HEAD_DIM, g * tq:(g + 1) * tq] * inv_l
        out_ref[:, h * HEAD_DIM:(h + 1) * HEAD_DIM] = oh.T.astype(BF16)


def _dsa_attention(q, qi, wi, ki, k, vt, n_valid, topk, tk):
    bsz, tp, _ = q.shape
    gw = KV_GROUP * TQ
    qrow = lambda width: pl.BlockSpec((None, TQ, width), lambda b, j: (b, j, 0))
    seq = lambda width: pl.BlockSpec((None, tp, width), lambda b, j: (b, 0, 0))
    return pl.pallas_call(
        functools.partial(_dsa_kernel, n_valid=n_valid, topk=topk, tk=tk),
        grid=(bsz, pl.cdiv(tp, TQ)),
        in_specs=[qrow(N_HEADS * HEAD_DIM), qrow(IDX_HEADS * IDX_DIM), qrow(LANES), seq(LANES),
                  seq(N_KV_HEADS * HEAD_DIM),
                  pl.BlockSpec((None, tp // tk, N_KV_HEADS * VT_ROWS, tk), lambda b, j: (b, 0, 0, 0))],
        out_specs=qrow(N_HEADS * HEAD_DIM),
        out_shape=jax.ShapeDtypeStruct((bsz, tp, N_HEADS * HEAD_DIM), BF16),
        scratch_shapes=[pltpu.VMEM((tp, TQ), jnp.int32), pltpu.VMEM((tp, TQ), jnp.int16),
                        pltpu.VMEM((tp, TQ), jnp.int16), pltpu.VMEM((tp, TQ), BF16),
                        pltpu.VMEM((N_KV_HEADS, HEAD_DIM, gw), BF16),
                        pltpu.VMEM((LANES, IDX_HEADS * TQ), BF16),
                        pltpu.VMEM((N_KV_HEADS, VT_ROWS, gw), F32),
                        pltpu.VMEM((N_KV_HEADS, 1, gw), F32),
                        pltpu.VMEM((N_KV_HEADS, tk, gw), BF16), pltpu.VMEM((N_KV_HEADS, tk, gw), BF16)],
        compiler_params=pltpu.CompilerParams(dimension_semantics=("arbitrary", "arbitrary"),
                                             vmem_limit_bytes=VMEM_LIMIT),
        name="dsa_attention",
    )(q, qi, wi, ki, k, vt)


def _rglru_kernel(h_ref, g_ref, w_ref, cw_ref, cb_ref, wa_ref, ba_ref, wx_ref, bx_ref, lam_ref,
                  out_ref, xext_ref, a_ref, b_ref, state_ref):
    tm = h_ref.shape[0]
    hist = SUBLANES

    @pl.when(pl.program_id(1) == 0)
    def _():
        xext_ref[0:hist, :] = jnp.zeros((hist, D_MODEL), F32)
        state_ref[...] = jnp.zeros_like(state_ref)

    n = _rms(h_ref[...], g_ref[...]).astype(BF16)
    xext_ref[hist:hist + tm, :] = jnp.dot(n, w_ref[:, 0:D_MODEL], preferred_element_type=F32)
    u = cb_ref[...] + cw_ref[0:1, :] * xext_ref[hist - RNN_CONV + 1:hist - RNN_CONV + 1 + tm, :]
    for kk in range(1, RNN_CONV):
        o = hist - RNN_CONV + 1 + kk
        u = u + cw_ref[kk:kk + 1, :] * xext_ref[o:o + tm, :]
    xext_ref[0:hist, :] = xext_ref[tm:tm + hist, :]

    ub = u.astype(BF16)
    rs, is_ = [], []
    for blk in range(RNN_BLOCKS):
        sl = slice(blk * RNN_BLOCK_DIM, (blk + 1) * RNN_BLOCK_DIM)
        rs.append(jnp.dot(ub[:, sl], wa_ref[blk], preferred_element_type=F32))
        is_.append(jnp.dot(ub[:, sl], wx_ref[blk], preferred_element_type=F32))
    r = jax.nn.sigmoid(jnp.concatenate(rs, axis=1) + ba_ref[...])
    ig = jax.nn.sigmoid(jnp.concatenate(is_, axis=1) + bx_ref[...])
    nl = -lam_ref[...]
    softplus = jnp.maximum(nl, 0.0) + jnp.log1p(jnp.exp(-jnp.abs(nl)))
    log_a = -LRU_C * r * softplus
    a_all = jnp.exp(log_a)
    b_all = jnp.sqrt(-_expm1(2.0 * log_a, a_all * a_all)) * (ig * u)
    nchunk = D_MODEL // LANES
    for c in range(nchunk):
        a_ref[c] = a_all[:, c * LANES:(c + 1) * LANES]
        b_ref[c] = b_all[:, c * LANES:(c + 1) * LANES]

    seg = tm // SCAN_SEGMENTS
    groups = SCAN_SEGMENTS // SUBLANES

    def step(i, carry):
        out = []
        for v in range(groups):
            prod, loc = carry[2 * v], carry[2 * v + 1]
            rows = pl.ds(v * SUBLANES * seg + i, SUBLANES, stride=seg)
            a = a_ref[:, rows, :]
            prod = a * prod
            loc = a * loc + b_ref[:, rows, :]
            a_ref[:, rows, :] = prod
            b_ref[:, rows, :] = loc
            out += [prod, loc]
        return tuple(out)

    init = (jnp.ones((nchunk, SUBLANES, LANES), F32), jnp.zeros((nchunk, SUBLANES, LANES), F32)) * groups
    ends = lax.fori_loop(0, seg, step, init)
    gel = jax.nn.gelu(jnp.dot(n, w_ref[:, D_MODEL:], preferred_element_type=F32))
    hcur = state_ref[...]
    for s in range(SCAN_SEGMENTS):
        rs = slice(s * seg, (s + 1) * seg)
        hs = a_ref[:, rs, :] * hcur + b_ref[:, rs, :]
        for c in range(nchunk):
            cs = slice(c * LANES, (c + 1) * LANES)
            out_ref[rs, cs] = (hs[c] * gel[rs, cs]).astype(BF16)
        v, r = s // SUBLANES, s % SUBLANES
        hcur = ends[2 * v][:, r:r + 1, :] * hcur + ends[2 * v + 1][:, r:r + 1, :]
    state_ref[...] = hcur


def _rglru(h3, g, w, cw, cb, wa, ba, wx, bx, lam, tm):
    bsz, tp, _ = h3.shape
    row = pl.BlockSpec((None, tm, D_MODEL), lambda b, t: (b, t, 0))
    vec = _const_spec((1, D_MODEL))
    blkw = _const_spec((RNN_BLOCKS, RNN_BLOCK_DIM, RNN_BLOCK_DIM))
    return pl.pallas_call(
        _rglru_kernel,
        grid=(bsz, tp // tm),
        in_specs=[row, vec, _const_spec((D_MODEL, 2 * D_MODEL)), _const_spec((RNN_CONV, D_MODEL)), vec,
                  blkw, vec, blkw, vec, vec],
        out_specs=row,
        out_shape=jax.ShapeDtypeStruct((bsz, tp, D_MODEL), BF16),
        scratch_shapes=[pltpu.VMEM((tm + SUBLANES, D_MODEL), F32),
                        pltpu.VMEM((D_MODEL // LANES, tm, LANES), F32),
                        pltpu.VMEM((D_MODEL // LANES, tm, LANES), F32),
                        pltpu.VMEM((D_MODEL // LANES, 1, LANES), F32)],
        compiler_params=pltpu.CompilerParams(dimension_semantics=("arbitrary", "arbitrary"),
                                             vmem_limit_bytes=VMEM_LIMIT),
        name="rglru",
    )(h3, g, w, cw, cb, wa, ba, wx, bx, lam)


SCAN_SEGMENTS = 16
CONV_HIST = 32
CONV_ROWS = 128


def _conformer_kernel(h_ref, g_ref, w_ref, dw_ref, db_ref, lg_ref, lb_ref, out_ref, xs_ref, y_ref):
    tm = h_ref.shape[0]

    @pl.when(pl.program_id(1) == 0)
    def _():
        xs_ref[0, 0:CONV_HIST, :] = jnp.zeros((CONV_HIST, D_MODEL), F32)

    n = _rms(h_ref[...], g_ref[...]).astype(BF16)
    a = jnp.dot(n, w_ref[:, 0:D_MODEL], preferred_element_type=F32)
    gate = jnp.dot(n, w_ref[:, D_MODEL:], preferred_element_type=F32)
    xs_ref[0, CONV_HIST:CONV_HIST + tm, :] = a * jax.nn.sigmoid(gate)
    span = tm + CONV_HIST - SUBLANES
    for s in range(1, SUBLANES):
        xs_ref[s, 0:span, :] = xs_ref[0, s:s + span, :]

    first = CONV_HIST - CONV_KERNEL + 1
    row_chunks = range(0, tm, CONV_ROWS)
    for c0 in range(0, D_MODEL, LANES):
        cs = slice(c0, c0 + LANES)
        accs = [jnp.broadcast_to(db_ref[:, cs], (CONV_ROWS, LANES)) for _ in row_chunks]
        for kk in range(CONV_KERNEL):
            shift = (first + kk) % SUBLANES
            base = first + kk - shift
            wk = dw_ref[kk:kk + 1, cs]
            for ri, r0 in enumerate(row_chunks):
                accs[ri] = accs[ri] + wk * xs_ref[shift, base + r0:base + r0 + CONV_ROWS, cs]
        for ri, r0 in enumerate(row_chunks):
            y_ref[r0:r0 + CONV_ROWS, cs] = accs[ri]
    xs_ref[0, 0:CONV_HIST, :] = xs_ref[0, tm:tm + CONV_HIST, :]

    y = y_ref[...]
    mu = jnp.mean(y, axis=-1, keepdims=True)
    yc = y - mu
    z = yc * lax.rsqrt(jnp.mean(yc * yc, axis=-1, keepdims=True) + NORM_EPS) * lg_ref[...] + lb_ref[...]
    out_ref[...] = jax.nn.silu(z).astype(BF16)


def _conformer(h3, g, w, dw, db, lg, lb, tm):
    bsz, tp, _ = h3.shape
    row = pl.BlockSpec((None, tm, D_MODEL), lambda b, t: (b, t, 0))
    vec = _const_spec((1, D_MODEL))
    return pl.pallas_call(
        _conformer_kernel,
        grid=(bsz, tp // tm),
        in_specs=[row, vec, _const_spec((D_MODEL, 2 * D_MODEL)), _const_spec((CONV_KERNEL, D_MODEL)),
                  vec, vec, vec],
        out_specs=row,
        out_shape=jax.ShapeDtypeStruct((bsz, tp, D_MODEL), BF16),
        scratch_shapes=[pltpu.VMEM((SUBLANES, tm + CONV_HIST, D_MODEL), F32), pltpu.VMEM((tm, D_MODEL), F32)],
        compiler_params=pltpu.CompilerParams(dimension_semantics=("arbitrary", "arbitrary"),
                                             vmem_limit_bytes=VMEM_LIMIT),
        name="conformer",
    )(h3, g, w, dw, db, lg, lb)


def _merge_kernel(h_ref, at_ref, rn_ref, cv_ref, g_ref, wg_ref, wa_ref, wr_ref, wc_ref, wo_ref, out_ref):
    h = h_ref[...]
    n = _rms(h, g_ref[...]).astype(BF16)
    merged = None
    for i, (src, wref) in enumerate(((at_ref, wa_ref), (rn_ref, wr_ref), (cv_ref, wc_ref))):
        gate = jax.nn.sigmoid(jnp.dot(n, wg_ref[:, i * D_MODEL:(i + 1) * D_MODEL],
                                      preferred_element_type=F32))
        term = gate * jnp.dot(src[...], wref[...], preferred_element_type=F32)
        merged = term if merged is None else merged + term
    out_ref[...] = h + jnp.dot(merged.astype(BF16), wo_ref[...], preferred_element_type=F32)


def _merge(h2, attn, rnn, cnv, g, wg, wa, wr, wc, wo, tm):
    rows = h2.shape[0]
    row = pl.BlockSpec((tm, D_MODEL), lambda i: (i, 0))
    sq = _const_spec((D_MODEL, D_MODEL))
    return pl.pallas_call(
        _merge_kernel,
        grid=(rows // tm,),
        in_specs=[row, row, row, row, _const_spec((1, D_MODEL)), _const_spec((D_MODEL, 3 * D_MODEL)),
                  sq, sq, sq, sq],
        out_specs=row,
        out_shape=jax.ShapeDtypeStruct((rows, D_MODEL), F32),
        compiler_params=pltpu.CompilerParams(dimension_semantics=("arbitrary",),
                                             vmem_limit_bytes=VMEM_LIMIT),
        name="merge",
    )(h2, attn, rnn, cnv, g, wg, wa, wr, wc, wo)


def _ffn_kernel(h_ref, g_ref, wg_ref, wu_ref, wd_ref, out_ref):
    h = h_ref[...]
    f = _rms(h, g_ref[...]).astype(BF16)
    gate = jnp.dot(f, wg_ref[...], preferred_element_type=F32)
    up = jnp.dot(f, wu_ref[...], preferred_element_type=F32)
    act = (jax.nn.silu(gate) * up).astype(BF16)
    out_ref[...] = h + jnp.dot(act, wd_ref[...], preferred_element_type=F32)


def _ffn(h2, g, wg, wu, wd, tm):
    rows = h2.shape[0]
    dff = wg.shape[1]
    row = pl.BlockSpec((tm, D_MODEL), lambda i: (i, 0))
    return pl.pallas_call(
        _ffn_kernel,
        grid=(rows // tm,),
        in_specs=[row, _const_spec((1, D_MODEL)), _const_spec((D_MODEL, dff)), _const_spec((D_MODEL, dff)),
                  _const_spec((dff, D_MODEL))],
        out_specs=row,
        out_shape=jax.ShapeDtypeStruct((rows, D_MODEL), F32),
        compiler_params=pltpu.CompilerParams(dimension_semantics=("arbitrary",),
                                             vmem_limit_bytes=VMEM_LIMIT),
        name="ffn",
    )(h2, g, wg, wu, wd)


def _rope_tables(n, dim):
    inv = ROPE_THETA ** (-jnp.arange(0, dim, 2, dtype=F32) / dim)
    ang = jnp.arange(n, dtype=F32)[:, None] * inv[None, :]
    return jnp.cos(ang), jnp.sin(ang)


def kernel(x, meta, mix_norm_g, w_in, q_norm_g, k_norm_g, rnn_conv_w, rnn_conv_b, rnn_wa, rnn_ba, rnn_wx, rnn_bx, rnn_lambda, conv_dw_w, conv_dw_b, conv_ln_g, conv_ln_b, w_o_attn, w_o_rnn, w_o_conv, w_out, ffn_norm_g, w_ffn_gate, w_ffn_up, w_ffn_down):
    bsz, seq, _ = x.shape
    depth = w_in.shape[0]
    t_valid = seq + N_META
    tp = -(-t_valid // LANES) * LANES
    topk = min(TOPK_MAX, seq // 4)
    rows = bsz * tp
    tm_seq = _pick_tile(tp, (384, 256, 128))
    tm_flat = _pick_tile(rows, (512, 384, 256, 128))

    q_w, kv_w, qi_w = N_HEADS * HEAD_DIM, N_KV_HEADS * HEAD_DIM, IDX_HEADS * IDX_DIM
    o = np.cumsum([0, q_w, kv_w, kv_w, qi_w, IDX_DIM, IDX_HEADS, D_MODEL, D_MODEL, 2 * D_MODEL, 3 * D_MODEL])
    pad_w = LANES - IDX_DIM - IDX_HEADS
    w_attn = jnp.concatenate([w_in[:, :, o[0]:o[6]], jnp.zeros((depth, D_MODEL, pad_w), w_in.dtype)],
                             axis=2).astype(BF16)
    w_rnn = w_in[:, :, o[6]:o[8]].astype(BF16)
    w_cnv = w_in[:, :, o[8]:o[9]].astype(BF16)
    w_gate = w_in[:, :, o[9]:o[10]].astype(BF16)
    wa_b, wx_b = rnn_wa.astype(BF16), rnn_wx.astype(BF16)
    w_oa, w_or, w_oc, w_ot = (a.astype(BF16) for a in (w_o_attn, w_o_rnn, w_o_conv, w_out))
    w_fg, w_fu, w_fd = (a.astype(BF16) for a in (w_ffn_gate, w_ffn_up, w_ffn_down))

    cos_a, sin_a = _rope_tables(tp, HEAD_DIM)
    cos_i, sin_i = _rope_tables(tp, IDX_DIM)
    tabs = (jnp.concatenate([cos_a, cos_a], axis=1), jnp.concatenate([-sin_a, sin_a], axis=1),
            jnp.concatenate([cos_i] * 4, axis=1), jnp.concatenate([-sin_i, sin_i] * 2, axis=1))

    h = jnp.concatenate([jnp.broadcast_to(meta[None].astype(x.dtype), (bsz, N_META, D_MODEL)), x,
                         jnp.zeros((bsz, tp - t_valid, D_MODEL), x.dtype)], axis=1)
    vec = lambda a: a.reshape(1, -1)

    for l in range(depth):
        h2 = h.reshape(rows, D_MODEL)
        q, k, vt, qi, ki, wi = _attn_proj(h2, vec(mix_norm_g[l]), w_attn[l], vec(q_norm_g[l]),
                                          vec(k_norm_g[l]), tabs, tm_seq, tp // tm_seq)
        seq3 = lambda a: a.reshape(bsz, tp, a.shape[-1])
        attn = _dsa_attention(seq3(q), seq3(qi), seq3(wi), seq3(ki), seq3(k),
                              vt.reshape(bsz, tp // tm_seq, N_KV_HEADS * VT_ROWS, tm_seq),
                              t_valid, topk, tm_seq)
        attn = attn.reshape(rows, N_HEADS * HEAD_DIM)

        rnn = _rglru(h, vec(mix_norm_g[l]), w_rnn[l], rnn_conv_w[l], vec(rnn_conv_b[l]), wa_b[l],
                     vec(rnn_ba[l]), wx_b[l], vec(rnn_bx[l]), vec(rnn_lambda[l]), tm_seq)
        cnv = _conformer(h, vec(mix_norm_g[l]), w_cnv[l], conv_dw_w[l], vec(conv_dw_b[l]),
                         vec(conv_ln_g[l]), vec(conv_ln_b[l]), tm_seq)
        h2 = _merge(h2, attn, rnn.reshape(rows, D_MODEL), cnv.reshape(rows, D_MODEL), vec(mix_norm_g[l]),
                    w_gate[l], w_oa[l], w_or[l], w_oc[l], w_ot[l], tm_flat)
        h2 = _ffn(h2, vec(ffn_norm_g[l]), w_fg[l], w_fu[l], w_fd[l], tm_flat)
        h = h2.reshape(bsz, tp, D_MODEL)

    return h[:, N_META:t_valid]
```

```python
import functools

import jax
import jax.numpy as jnp
import numpy as np
from jax import lax
from jax.experimental import pallas as pl
from jax.experimental.pallas import tpu as pltpu

D_MODEL = 1024
CHUNK = 64
N_META = 16
N_HEADS = 8
N_KV_HEADS = 2
HEAD_DIM = 128
KV_GROUP = N_HEADS // N_KV_HEADS
IDX_HEADS = 8
IDX_DIM = 64
TOPK_MAX = 256
ROPE_THETA = 10000.0
RNN_BLOCKS = 8
RNN_BLOCK_DIM = D_MODEL // RNN_BLOCKS
RNN_CONV = 4
LRU_C = 8.0
CONV_KERNEL = 31
NORM_EPS = 1e-6

LANES = 128
SUBLANES = 8
PACKED_ROWS = 16
TQ = 256
VMEM_LIMIT = 56 * 1024 * 1024
LOG2E = 1.4426950408889634

HALF_BITS = 16
HALF_MASK = 2 ** HALF_BITS - 1
LOW_FIRST_STEPS = 10
I16_MIN = -(2 ** (HALF_BITS - 1))
I32_MAX = 2 ** 31 - 1
KEY_MASKED = -2139095041
NEG_BIG = -(2.0 ** 100)
VT_ROWS = HEAD_DIM + PACKED_ROWS

F32 = jnp.float32
BF16 = jnp.bfloat16


def _rms(x, g):
    return x * lax.rsqrt(jnp.mean(x * x, axis=-1, keepdims=True) + NORM_EPS) * g


def _expm1(y, u):
    near = jnp.where(u == 1.0, y, (u - 1.0) * y / jnp.log(jnp.where(u == 1.0, 2.0, u)))
    return jnp.where(y > -0.5, near, u - 1.0)


def _const_spec(shape):
    nd = len(shape)
    return pl.BlockSpec(shape, lambda *_: (0,) * nd, pipeline_mode=pl.Buffered(1))


def _pick_tile(n, candidates):
    for c in candidates:
        if n % c == 0:
            return c
    raise ValueError(f"no tile for {n}")


def _attn_proj_kernel(x_ref, g_ref, w_ref, qg_ref, kg_ref, cosa_ref, sina_ref, cosi_ref, sini_ref,
                      q_ref, k_ref, vt_ref, qi_ref, ki_ref, wi_ref, *, idx_scale):
    n = _rms(x_ref[...], g_ref[...]).astype(BF16)
    cosa, sina = cosa_ref[...], sina_ref[...]
    cosi, sini = cosi_ref[...], sini_ref[...]
    lane = lax.broadcasted_iota(jnp.int32, cosi.shape, 1)
    low_half = (lane % IDX_DIM) < (IDX_DIM // 2)

    def rot_attn(a, g):
        a = _rms(a, g)
        return a * cosa + pltpu.roll(a, HEAD_DIM // 2, axis=1) * sina

    def rot_idx(a):
        partner = jnp.where(low_half, pltpu.roll(a, LANES - IDX_DIM // 2, axis=1),
                            pltpu.roll(a, IDX_DIM // 2, axis=1))
        return a * cosi + partner * sini

    q_w = N_HEADS * HEAD_DIM
    kv_w = N_KV_HEADS * HEAD_DIM
    qi_w = IDX_HEADS * IDX_DIM
    o_k, o_v, o_qi, o_ki = q_w, q_w + kv_w, q_w + 2 * kv_w, q_w + 2 * kv_w + qi_w

    q_scale = (HEAD_DIM ** -0.5) * LOG2E
    pq = jnp.dot(n, w_ref[:, 0:q_w], preferred_element_type=F32)
    for h in range(N_HEADS):
        sl = slice(h * HEAD_DIM, (h + 1) * HEAD_DIM)
        q_ref[:, sl] = (rot_attn(pq[:, sl], qg_ref[...]) * q_scale).astype(BF16)
    pk = jnp.dot(n, w_ref[:, o_k:o_qi], preferred_element_type=F32)
    for h in range(N_KV_HEADS):
        sl = slice(h * HEAD_DIM, (h + 1) * HEAD_DIM)
        k_ref[:, sl] = rot_attn(pk[:, sl], kg_ref[...]).astype(BF16)
        vt_ref[h * VT_ROWS:h * VT_ROWS + HEAD_DIM, :] = (
            pk[:, kv_w + h * HEAD_DIM:kv_w + (h + 1) * HEAD_DIM].T.astype(BF16))
        vt_ref[h * VT_ROWS + HEAD_DIM:(h + 1) * VT_ROWS, :] = jnp.ones((PACKED_ROWS, pk.shape[0]), BF16)
    pi = jnp.dot(n, w_ref[:, o_qi:], preferred_element_type=F32)
    for c in range(qi_w // LANES):
        sl = slice(c * LANES, (c + 1) * LANES)
        qi_ref[:, sl] = rot_idx(pi[:, sl]).astype(BF16)
    last = pi[:, qi_w:]
    ki = rot_idx(last)
    ki_ref[...] = jnp.where(lane < IDX_DIM, ki, pltpu.roll(ki, IDX_DIM, axis=1)).astype(BF16)
    wi_ref[...] = last * idx_scale


def _attn_proj(h2, g, w, qg, kg, tabs, tm, tiles_per_seq):
    rows = h2.shape[0]
    wcols = w.shape[1]
    row = lambda width: pl.BlockSpec((tm, width), lambda i: (i, 0))
    tab = pl.BlockSpec((tm, LANES), lambda i: (i % tiles_per_seq, 0))
    idx_scale = (IDX_HEADS ** -0.5) * (IDX_DIM ** -0.5)
    return pl.pallas_call(
        functools.partial(_attn_proj_kernel, idx_scale=idx_scale),
        grid=(rows // tm,),
        in_specs=[row(D_MODEL), _const_spec((1, D_MODEL)), _const_spec((D_MODEL, wcols)),
                  _const_spec((1, HEAD_DIM)), _const_spec((1, HEAD_DIM)), tab, tab, tab, tab],
        out_specs=[row(N_HEADS * HEAD_DIM), row(N_KV_HEADS * HEAD_DIM),
                   pl.BlockSpec((None, N_KV_HEADS * VT_ROWS, tm), lambda i: (i, 0, 0)),
                   row(IDX_HEADS * IDX_DIM), row(LANES), row(LANES)],
        out_shape=[jax.ShapeDtypeStruct((rows, N_HEADS * HEAD_DIM), BF16),
                   jax.ShapeDtypeStruct((rows, N_KV_HEADS * HEAD_DIM), BF16),
                   jax.ShapeDtypeStruct((rows // tm, N_KV_HEADS * VT_ROWS, tm), BF16),
                   jax.ShapeDtypeStruct((rows, IDX_HEADS * IDX_DIM), BF16),
                   jax.ShapeDtypeStruct((rows, LANES), BF16),
                   jax.ShapeDtypeStruct((rows, LANES), F32)],
        compiler_params=pltpu.CompilerParams(dimension_semantics=("arbitrary",),
                                             vmem_limit_bytes=VMEM_LIMIT),
        name="attn_proj",
    )(h2, g, w, qg, kg, *tabs)


def _dsa_kernel(q_ref, qi_ref, wi_ref, ki_ref, k_ref, vt_ref, out_ref,
                key_ref, hi_ref, lo_ref, bias_ref, qt_ref, qit_ref, acc_ref, m_ref, sa_ref, sb_ref, fin_ref,
                *, n_valid, topk, tk):
    tq = TQ
    gw = KV_GROUP * tq
    t0 = pl.program_id(1) * tq
    tcol = t0 + lax.broadcasted_iota(jnp.int32, (1, tq), 1)
    kb = jnp.where(tcol < N_META, N_META,
                   N_META + CHUNK * (1 + jnp.right_shift(tcol - N_META, CHUNK.bit_length() - 1)))
    kb = jnp.where(tcol < n_valid, jnp.minimum(kb, n_valid), n_valid)
    kext = jnp.minimum(n_valid, N_META + CHUNK * (1 + (t0 + tq - 1 - N_META) // CHUNK))
    n_tiles = (kext + tk - 1) // tk

    for h in range(N_HEADS):
        qh = q_ref[:, h * HEAD_DIM:(h + 1) * HEAD_DIM].astype(F32).T
        qt_ref[h // KV_GROUP, :, (h % KV_GROUP) * tq:(h % KV_GROUP + 1) * tq] = qh.astype(BF16)
    row = lax.broadcasted_iota(jnp.int32, (LANES, tq), 0)
    for hp in range(IDX_HEADS // 2):
        pair = qi_ref[:, hp * LANES:(hp + 1) * LANES].astype(F32).T
        qit_ref[:, (2 * hp) * tq:(2 * hp + 1) * tq] = jnp.where(row < IDX_DIM, pair, 0.0).astype(BF16)
        qit_ref[:, (2 * hp + 1) * tq:(2 * hp + 2) * tq] = jnp.where(row >= IDX_DIM, pair, 0.0).astype(BF16)
    w = wi_ref[...].T[IDX_DIM:IDX_DIM + IDX_HEADS, :]

    def score_tile(i, carry):
        r0 = pl.multiple_of(i * tk, tk)
        kt = ki_ref[pl.ds(r0, tk), :]
        acc = jnp.zeros((tk, tq), F32)
        for hp in range(IDX_HEADS // 2):
            lg = jnp.dot(kt, qit_ref[:, hp * 2 * tq:(hp + 1) * 2 * tq],
                         preferred_element_type=F32)
            for hh in range(2):
                h = 2 * hp + hh
                acc = acc + jnp.maximum(lg[:, hh * tq:(hh + 1) * tq], 0.0) * w[h:h + 1, :]
        acc = jnp.where(acc == 0.0, 0.0, acc)
        pos = r0 + lax.broadcasted_iota(jnp.int32, (tk, tq), 0)
        bits = pltpu.bitcast(acc, jnp.int32)
        key = jnp.where(bits >= 0, bits, bits ^ I32_MAX)
        key = jnp.where(pos < kb, key, KEY_MASKED)
        key_ref[pl.ds(r0, tk), :] = key
        hi_ref[pl.ds(r0, tk), :] = lax.shift_right_arithmetic(key, HALF_BITS).astype(jnp.int16)
        return carry

    lax.fori_loop(0, n_tiles, score_tile, 0)

    def search16(ref, kth, first=0, last=HALF_BITS, start=None):
        def count_ge(cand):
            c16 = cand.astype(jnp.int16)

            def body(i, acc):
                r0 = pl.multiple_of(i * tk, tk)
                ind = jnp.where(ref[pl.ds(r0, tk), :] >= c16, jnp.bfloat16(1), jnp.bfloat16(0))
                parts = [ind[r * PACKED_ROWS:(r + 1) * PACKED_ROWS] for r in range(tk // PACKED_ROWS)]
                while len(parts) > 1:
                    nxt = [parts[r] + parts[r + 1] for r in range(0, len(parts) - 1, 2)]
                    parts = nxt + ([parts[-1]] if len(parts) % 2 else [])
                return acc + parts[0].astype(F32)

            acc = lax.fori_loop(0, n_tiles, body, jnp.zeros((PACKED_ROWS, tq), F32))
            return acc.sum(axis=0, keepdims=True)

        def step(it, carry):
            lo, c_lo = carry
            cand = lo + jnp.left_shift(jnp.int32(1), HALF_BITS - 1 - it)
            c = count_ge(cand)
            return jnp.where(c >= kth, cand, lo), jnp.where(c >= kth, c, c_lo)

        if start is None:
            start = (jnp.full((1, tq), I16_MIN, jnp.int32),
                     jnp.full((1, tq), n_tiles * tk, jnp.int32).astype(F32))
        return lax.fori_loop(first, last, step, start)

    tau_hi, _ = search16(hi_ref, jnp.full((1, tq), float(topk), F32))

    def low_tile(i, acc):
        r0 = pl.multiple_of(i * tk, tk)
        blk = key_ref[pl.ds(r0, tk), :]
        hi = lax.shift_right_arithmetic(blk, HALF_BITS)
        low = jnp.where(hi == tau_hi, (blk & HALF_MASK) + I16_MIN, I16_MIN)
        lo_ref[pl.ds(r0, tk), :] = low.astype(jnp.int16)
        above = jnp.where(hi > tau_hi, 1, 0).astype(jnp.int32)
        return acc + above.reshape(tk // SUBLANES, SUBLANES, tq).sum(axis=0)

    c_above = lax.fori_loop(0, n_tiles, low_tile, jnp.zeros((SUBLANES, tq), jnp.int32))
    c_above = c_above.sum(axis=0, keepdims=True)
    c_above = c_above.astype(F32)
    k_low = topk - c_above
    tau_lo, c_low = search16(lo_ref, k_low, 0, LOW_FIRST_STEPS)
    fin_ref[0] = tau_lo
    fin_ref[1] = c_low.astype(jnp.int32)

    @pl.when(jnp.max(jnp.abs(c_low - k_low)) > 0)
    def _():
        lo2, c2 = search16(lo_ref, k_low, LOW_FIRST_STEPS, HALF_BITS, (tau_lo, c_low))
        fin_ref[0] = lo2
        fin_ref[1] = c2.astype(jnp.int32)

    tau_lo, c_low = fin_ref[0], fin_ref[1].astype(F32)
    tau = tau_hi * (HALF_MASK + 1) + (tau_lo - I16_MIN)
    cut = jnp.max(c_above + c_low) > topk

    @pl.when(jnp.logical_not(cut))
    def _():
        def bias_tile(i, carry):
            r0 = pl.multiple_of(i * tk, tk)
            pos = r0 + lax.broadcasted_iota(jnp.int32, (tk, tq), 0)
            keep = (key_ref[pl.ds(r0, tk), :] >= tau) & (pos < kb)
            bias_ref[pl.ds(r0, tk), :] = jnp.where(keep, 0.0, NEG_BIG).astype(BF16)
            return carry

        lax.fori_loop(0, n_tiles, bias_tile, 0)

    @pl.when(cut)
    def _():
        def greater_tile(i, acc):
            r0 = pl.multiple_of(i * tk, tk)
            ind = jnp.where(key_ref[pl.ds(r0, tk), :] > tau, 1, 0).astype(jnp.int32)
            return acc + ind.reshape(tk // SUBLANES, SUBLANES, tq).sum(axis=0)

        c_gt = lax.fori_loop(0, n_tiles, greater_tile, jnp.zeros((SUBLANES, tq), jnp.int32))
        rank = (topk - c_gt.sum(axis=0, keepdims=True)).astype(F32)
        tri = jnp.where(lax.broadcasted_iota(jnp.int32, (tk, tk), 0)
                        >= lax.broadcasted_iota(jnp.int32, (tk, tk), 1), 1.0, 0.0).astype(BF16)

        def bias_tile(i, seen):
            r0 = pl.multiple_of(i * tk, tk)
            blk = key_ref[pl.ds(r0, tk), :]
            pos = r0 + lax.broadcasted_iota(jnp.int32, (tk, tq), 0)
            eq = blk == tau
            nth = seen + jnp.dot(tri, jnp.where(eq, 1.0, 0.0).astype(BF16), preferred_element_type=F32)
            keep = ((blk > tau) | (eq & (nth <= rank))) & (pos < kb)
            bias_ref[pl.ds(r0, tk), :] = jnp.where(keep, 0.0, NEG_BIG).astype(BF16)
            return nth[tk - 1:tk, :]

        lax.fori_loop(0, n_tiles, bias_tile, jnp.zeros((1, tq), F32))

    acc_ref[...] = jnp.zeros_like(acc_ref)
    m_ref[...] = jnp.full(m_ref.shape, NEG_BIG, F32)

    def scores(t, s_ref):
        r0 = pl.multiple_of(t * tk, tk)
        for n in range(N_KV_HEADS):
            kt = k_ref[pl.ds(r0, tk), n * HEAD_DIM:(n + 1) * HEAD_DIM]
            s_ref[n] = jnp.dot(kt, qt_ref[n], preferred_element_type=F32).astype(BF16)

    def softmax_pv(t, s_ref):
        r0 = pl.multiple_of(t * tk, tk)
        bt = bias_ref[pl.ds(r0, tk), :]
        bt = jnp.concatenate([bt] * KV_GROUP, axis=1)
        for n in range(N_KV_HEADS):
            s = s_ref[n] + bt
            m = m_ref[n]
            m_new = jnp.maximum(m, jnp.max(s, axis=0, keepdims=True).astype(F32))
            alpha = jnp.exp2(m - m_new)
            p = jnp.exp2(s - m_new.astype(BF16))
            m_ref[n] = m_new
            vt = vt_ref[t, n * VT_ROWS:(n + 1) * VT_ROWS, :]
            acc_ref[n] = alpha * acc_ref[n] + jnp.dot(vt, p, preferred_element_type=F32)

    last_tile = k_ref.shape[0] // tk - 1
    scores(0, sa_ref)

    def tile_pair(i, carry):
        t = 2 * i
        scores(t + 1, sb_ref)
        softmax_pv(t, sa_ref)
        scores(jnp.minimum(t + 2, last_tile), sa_ref)
        softmax_pv(t + 1, sb_ref)
        return carry

    lax.fori_loop(0, n_tiles // 2, tile_pair, 0)

    @pl.when(n_tiles % 2 == 1)
    def _():
        softmax_pv(n_tiles - 1, sa_ref)

    for h in range(N_HEADS):
        n, g = h // KV_GROUP, h % KV_GROUP
        inv_l = 1.0 / acc_ref[n, HEAD_DIM:HEAD_DIM + 1, g * tq:(g + 1) * tq]
        oh = acc_ref[n, 0:HEAD_DIM, g * tq:(g + 1) * tq] * inv_l
        out_ref[:, h * HEAD_DIM:(h + 1) * HEAD_DIM] = oh.T.astype(BF16)


def _dsa_attention(q, qi, wi, ki, k, vt, n_valid, topk, tk):
    bsz, tp, _ = q.shape
    gw = KV_GROUP * TQ
    qrow = lambda width: pl.BlockSpec((None, TQ, width), lambda b, j: (b, j, 0))
    seq = lambda width: pl.BlockSpec((None, tp, width), lambda b, j: (b, 0, 0))
    return pl.pallas_call(
        functools.partial(_dsa_kernel, n_valid=n_valid, topk=topk, tk=tk),
        grid=(bsz, pl.cdiv(tp, TQ)),
        in_specs=[qrow(N_HEADS * HEAD_DIM), qrow(IDX_HEADS * IDX_DIM), qrow(LANES), seq(LANES),
                  seq(N_KV_HEADS * HEAD_DIM),
                  pl.BlockSpec((None, tp // tk, N_KV_HEADS * VT_ROWS, tk), lambda b, j: (b, 0, 0, 0))],
        out_specs=qrow(N_HEADS * HEAD_DIM),
        out_shape=jax.ShapeDtypeStruct((bsz, tp, N_HEADS * HEAD_DIM), BF16),
        scratch_shapes=[pltpu.VMEM((tp, TQ), jnp.int32), pltpu.VMEM((tp, TQ), jnp.int16),
                        pltpu.VMEM((tp, TQ), jnp.int16), pltpu.VMEM((tp, TQ), BF16),
                        pltpu.VMEM((N_KV_HEADS, HEAD_DIM, gw), BF16),
                        pltpu.VMEM((LANES, IDX_HEADS * TQ), BF16),
                        pltpu.VMEM((N_KV_HEADS, VT_ROWS, gw), F32),
                        pltpu.VMEM((N_KV_HEADS, 1, gw), F32),
                        pltpu.VMEM((N_KV_HEADS, tk, gw), BF16), pltpu.VMEM((N_KV_HEADS, tk, gw), BF16),
                        pltpu.VMEM((2, 1, TQ), jnp.int32)],
        compiler_params=pltpu.CompilerParams(dimension_semantics=("arbitrary", "arbitrary"),
                                             vmem_limit_bytes=VMEM_LIMIT),
        name="dsa_attention",
    )(q, qi, wi, ki, k, vt)


def _rglru_kernel(h_ref, g_ref, w_ref, cw_ref, cb_ref, wa_ref, ba_ref, wx_ref, bx_ref, lam_ref,
                  out_ref, xext_ref, a_ref, b_ref, state_ref):
    tm = h_ref.shape[0]
    hist = SUBLANES

    @pl.when(pl.program_id(1) == 0)
    def _():
        xext_ref[0:hist, :] = jnp.zeros((hist, D_MODEL), F32)
        state_ref[...] = jnp.zeros_like(state_ref)

    n = _rms(h_ref[...], g_ref[...]).astype(BF16)
    xext_ref[hist:hist + tm, :] = jnp.dot(n, w_ref[:, 0:D_MODEL], preferred_element_type=F32)
    u = cb_ref[...] + cw_ref[0:1, :] * xext_ref[hist - RNN_CONV + 1:hist - RNN_CONV + 1 + tm, :]
    for kk in range(1, RNN_CONV):
        o = hist - RNN_CONV + 1 + kk
        u = u + cw_ref[kk:kk + 1, :] * xext_ref[o:o + tm, :]
    xext_ref[0:hist, :] = xext_ref[tm:tm + hist, :]

    ub = u.astype(BF16)
    rs, is_ = [], []
    for blk in range(RNN_BLOCKS):
        sl = slice(blk * RNN_BLOCK_DIM, (blk + 1) * RNN_BLOCK_DIM)
        rs.append(jnp.dot(ub[:, sl], wa_ref[blk], preferred_element_type=F32))
        is_.append(jnp.dot(ub[:, sl], wx_ref[blk], preferred_element_type=F32))
    r = jax.nn.sigmoid(jnp.concatenate(rs, axis=1) + ba_ref[...])
    ig = jax.nn.sigmoid(jnp.concatenate(is_, axis=1) + bx_ref[...])
    nl = -lam_ref[...]
    softplus = jnp.maximum(nl, 0.0) + jnp.log1p(jnp.exp(-jnp.abs(nl)))
    log_a = -LRU_C * r * softplus
    a_all = jnp.exp(log_a)
    b_all = jnp.sqrt(-_expm1(2.0 * log_a, a_all * a_all)) * (ig * u)
    nchunk = D_MODEL // LANES
    for c in range(nchunk):
        a_ref[c] = a_all[:, c * LANES:(c + 1) * LANES]
        b_ref[c] = b_all[:, c * LANES:(c + 1) * LANES]

    seg = tm // SCAN_SEGMENTS
    groups = SCAN_SEGMENTS // SUBLANES

    def step(i, carry):
        out = []
        for v in range(groups):
            prod, loc = carry[2 * v], carry[2 * v + 1]
            rows = pl.ds(v * SUBLANES * seg + i, SUBLANES, stride=seg)
            a = a_ref[:, rows, :]
            prod = a * prod
            loc = a * loc + b_ref[:, rows, :]
            a_ref[:, rows, :] = prod
            b_ref[:, rows, :] = loc
            out += [prod, loc]
        return tuple(out)

    init = (jnp.ones((nchunk, SUBLANES, LANES), F32), jnp.zeros((nchunk, SUBLANES, LANES), F32)) * groups
    ends = lax.fori_loop(0, seg, step, init)
    gel = jax.nn.gelu(jnp.dot(n, w_ref[:, D_MODEL:], preferred_element_type=F32))
    hcur = state_ref[...]
    for s in range(SCAN_SEGMENTS):
        rs = slice(s * seg, (s + 1) * seg)
        hs = a_ref[:, rs, :] * hcur + b_ref[:, rs, :]
        for c in range(nchunk):
            cs = slice(c * LANES, (c + 1) * LANES)
            out_ref[rs, cs] = (hs[c] * gel[rs, cs]).astype(BF16)
        v, r = s // SUBLANES, s % SUBLANES
        hcur = ends[2 * v][:, r:r + 1, :] * hcur + ends[2 * v + 1][:, r:r + 1, :]
    state_ref[...] = hcur


def _rglru(h3, g, w, cw, cb, wa, ba, wx, bx, lam, tm):
    bsz, tp, _ = h3.shape
    row = pl.BlockSpec((None, tm, D_MODEL), lambda b, t: (b, t, 0))
    vec = _const_spec((1, D_MODEL))
    blkw = _const_spec((RNN_BLOCKS, RNN_BLOCK_DIM, RNN_BLOCK_DIM))
    return pl.pallas_call(
        _rglru_kernel,
        grid=(bsz, tp // tm),
        in_specs=[row, vec, _const_spec((D_MODEL, 2 * D_MODEL)), _const_spec((RNN_CONV, D_MODEL)), vec,
                  blkw, vec, blkw, vec, vec],
        out_specs=row,
        out_shape=jax.ShapeDtypeStruct((bsz, tp, D_MODEL), BF16),
        scratch_shapes=[pltpu.VMEM((tm + SUBLANES, D_MODEL), F32),
                        pltpu.VMEM((D_MODEL // LANES, tm, LANES), F32),
                        pltpu.VMEM((D_MODEL // LANES, tm, LANES), F32),
                        pltpu.VMEM((D_MODEL // LANES, 1, LANES), F32)],
        compiler_params=pltpu.CompilerParams(dimension_semantics=("arbitrary", "arbitrary"),
                                             vmem_limit_bytes=VMEM_LIMIT),
        name="rglru",
    )(h3, g, w, cw, cb, wa, ba, wx, bx, lam)


SCAN_SEGMENTS = 16
CONV_HIST = 32
CONV_ROWS = 128


def _conformer_kernel(h_ref, g_ref, w_ref, dw_ref, db_ref, lg_ref, lb_ref, out_ref, xs_ref, y_ref):
    tm = h_ref.shape[0]

    @pl.when(pl.program_id(1) == 0)
    def _():
        xs_ref[0, 0:CONV_HIST, :] = jnp.zeros((CONV_HIST, D_MODEL), F32)

    n = _rms(h_ref[...], g_ref[...]).astype(BF16)
    a = jnp.dot(n, w_ref[:, 0:D_MODEL], preferred_element_type=F32)
    gate = jnp.dot(n, w_ref[:, D_MODEL:], preferred_element_type=F32)
    xs_ref[0, CONV_HIST:CONV_HIST + tm, :] = a * jax.nn.sigmoid(gate)
    span = tm + CONV_HIST - SUBLANES
    for s in range(1, SUBLANES):
        xs_ref[s, 0:span, :] = xs_ref[0, s:s + span, :]

    first = CONV_HIST - CONV_KERNEL + 1
    row_chunks = range(0, tm, CONV_ROWS)
    for c0 in range(0, D_MODEL, LANES):
        cs = slice(c0, c0 + LANES)
        accs = [jnp.broadcast_to(db_ref[:, cs], (CONV_ROWS, LANES)) for _ in row_chunks]
        for kk in range(CONV_KERNEL):
            shift = (first + kk) % SUBLANES
            base = first + kk - shift
            wk = dw_ref[kk:kk + 1, cs]
            for ri, r0 in enumerate(row_chunks):
                accs[ri] = accs[ri] + wk * xs_ref[shift, base + r0:base + r0 + CONV_ROWS, cs]
        for ri, r0 in enumerate(row_chunks):
            y_ref[r0:r0 + CONV_ROWS, cs] = accs[ri]
    xs_ref[0, 0:CONV_HIST, :] = xs_ref[0, tm:tm + CONV_HIST, :]

    y = y_ref[...]
    mu = jnp.mean(y, axis=-1, keepdims=True)
    yc = y - mu
    z = yc * lax.rsqrt(jnp.mean(yc * yc, axis=-1, keepdims=True) + NORM_EPS) * lg_ref[...] + lb_ref[...]
    out_ref[...] = jax.nn.silu(z).astype(BF16)


def _conformer(h3, g, w, dw, db, lg, lb, tm):
    bsz, tp, _ = h3.shape
    row = pl.BlockSpec((None, tm, D_MODEL), lambda b, t: (b, t, 0))
    vec = _const_spec((1, D_MODEL))
    return pl.pallas_call(
        _conformer_kernel,
        grid=(bsz, tp // tm),
        in_specs=[row, vec, _const_spec((D_MODEL, 2 * D_MODEL)), _const_spec((CONV_KERNEL, D_MODEL)),
                  vec, vec, vec],
        out_specs=row,
        out_shape=jax.ShapeDtypeStruct((bsz, tp, D_MODEL), BF16),
        scratch_shapes=[pltpu.VMEM((SUBLANES, tm + CONV_HIST, D_MODEL), F32), pltpu.VMEM((tm, D_MODEL), F32)],
        compiler_params=pltpu.CompilerParams(dimension_semantics=("arbitrary", "arbitrary"),
                                             vmem_limit_bytes=VMEM_LIMIT),
        name="conformer",
    )(h3, g, w, dw, db, lg, lb)


def _merge_kernel(h_ref, at_ref, rn_ref, cv_ref, g_ref, wg_ref, wa_ref, wr_ref, wc_ref, wo_ref, out_ref):
    h = h_ref[...]
    n = _rms(h, g_ref[...]).astype(BF16)
    merged = None
    for i, (src, wref) in enumerate(((at_ref, wa_ref), (rn_ref, wr_ref), (cv_ref, wc_ref))):
        gate = jax.nn.sigmoid(jnp.dot(n, wg_ref[:, i * D_MODEL:(i + 1) * D_MODEL],
                                      preferred_element_type=F32))
        term = gate * jnp.dot(src[...], wref[...], preferred_element_type=F32)
        merged = term if merged is None else merged + term
    out_ref[...] = h + jnp.dot(merged.astype(BF16), wo_ref[...], preferred_element_type=F32)


def _merge(h2, attn, rnn, cnv, g, wg, wa, wr, wc, wo, tm):
    rows = h2.shape[0]
    row = pl.BlockSpec((tm, D_MODEL), lambda i: (i, 0))
    sq = _const_spec((D_MODEL, D_MODEL))
    return pl.pallas_call(
        _merge_kernel,
        grid=(rows // tm,),
        in_specs=[row, row, row, row, _const_spec((1, D_MODEL)), _const_spec((D_MODEL, 3 * D_MODEL)),
                  sq, sq, sq, sq],
        out_specs=row,
        out_shape=jax.ShapeDtypeStruct((rows, D_MODEL), F32),
        compiler_params=pltpu.CompilerParams(dimension_semantics=("arbitrary",),
                                             vmem_limit_bytes=VMEM_LIMIT),
        name="merge",
    )(h2, attn, rnn, cnv, g, wg, wa, wr, wc, wo)


def _ffn_kernel(h_ref, g_ref, wg_ref, wu_ref, wd_ref, out_ref):
    h = h_ref[...]
    f = _rms(h, g_ref[...]).astype(BF16)
    gate = jnp.dot(f, wg_ref[...], preferred_element_type=F32)
    up = jnp.dot(f, wu_ref[...], preferred_element_type=F32)
    act = (jax.nn.silu(gate) * up).astype(BF16)
    out_ref[...] = h + jnp.dot(act, wd_ref[...], preferred_element_type=F32)


def _ffn(h2, g, wg, wu, wd, tm):
    rows = h2.shape[0]
    dff = wg.shape[1]
    row = pl.BlockSpec((tm, D_MODEL), lambda i: (i, 0))
    return pl.pallas_call(
        _ffn_kernel,
        grid=(rows // tm,),
        in_specs=[row, _const_spec((1, D_MODEL)), _const_spec((D_MODEL, dff)), _const_spec((D_MODEL, dff)),
                  _const_spec((dff, D_MODEL))],
        out_specs=row,
        out_shape=jax.ShapeDtypeStruct((rows, D_MODEL), F32),
        compiler_params=pltpu.CompilerParams(dimension_semantics=("arbitrary",),
                                             vmem_limit_bytes=VMEM_LIMIT),
        name="ffn",
    )(h2, g, wg, wu, wd)


def _rope_tables(n, dim):
    inv = ROPE_THETA ** (-jnp.arange(0, dim, 2, dtype=F32) / dim)
    ang = jnp.arange(n, dtype=F32)[:, None] * inv[None, :]
    return jnp.cos(ang), jnp.sin(ang)


def kernel(x, meta, mix_norm_g, w_in, q_norm_g, k_norm_g, rnn_conv_w, rnn_conv_b, rnn_wa, rnn_ba, rnn_wx, rnn_bx, rnn_lambda, conv_dw_w, conv_dw_b, conv_ln_g, conv_ln_b, w_o_attn, w_o_rnn, w_o_conv, w_out, ffn_norm_g, w_ffn_gate, w_ffn_up, w_ffn_down):
    bsz, seq, _ = x.shape
    depth = w_in.shape[0]
    t_valid = seq + N_META
    tp = -(-t_valid // LANES) * LANES
    topk = min(TOPK_MAX, seq // 4)
    rows = bsz * tp
    tm_seq = _pick_tile(tp, (384, 256, 128))
    tm_flat = _pick_tile(rows, (512, 384, 256, 128))

    q_w, kv_w, qi_w = N_HEADS * HEAD_DIM, N_KV_HEADS * HEAD_DIM, IDX_HEADS * IDX_DIM
    o = np.cumsum([0, q_w, kv_w, kv_w, qi_w, IDX_DIM, IDX_HEADS, D_MODEL, D_MODEL, 2 * D_MODEL, 3 * D_MODEL])
    pad_w = LANES - IDX_DIM - IDX_HEADS
    w_attn = jnp.concatenate([w_in[:, :, o[0]:o[6]], jnp.zeros((depth, D_MODEL, pad_w), w_in.dtype)],
                             axis=2).astype(BF16)
    w_rnn = w_in[:, :, o[6]:o[8]].astype(BF16)
    w_cnv = w_in[:, :, o[8]:o[9]].astype(BF16)
    w_gate = w_in[:, :, o[9]:o[10]].astype(BF16)
    wa_b, wx_b = rnn_wa.astype(BF16), rnn_wx.astype(BF16)
    w_oa, w_or, w_oc, w_ot = (a.astype(BF16) for a in (w_o_attn, w_o_rnn, w_o_conv, w_out))
    w_fg, w_fu, w_fd = (a.astype(BF16) for a in (w_ffn_gate, w_ffn_up, w_ffn_down))

    cos_a, sin_a = _rope_tables(tp, HEAD_DIM)
    cos_i, sin_i = _rope_tables(tp, IDX_DIM)
    tabs = (jnp.concatenate([cos_a, cos_a], axis=1), jnp.concatenate([-sin_a, sin_a], axis=1),
            jnp.concatenate([cos_i] * 4, axis=1), jnp.concatenate([-sin_i, sin_i] * 2, axis=1))

    h = jnp.concatenate([jnp.broadcast_to(meta[None].astype(x.dtype), (bsz, N_META, D_MODEL)), x,
                         jnp.zeros((bsz, tp - t_valid, D_MODEL), x.dtype)], axis=1)
    vec = lambda a: a.reshape(1, -1)

    for l in range(depth):
        h2 = h.reshape(rows, D_MODEL)
        q, k, vt, qi, ki, wi = _attn_proj(h2, vec(mix_norm_g[l]), w_attn[l], vec(q_norm_g[l]),
                                          vec(k_norm_g[l]), tabs, tm_seq, tp // tm_seq)
        seq3 = lambda a: a.reshape(bsz, tp, a.shape[-1])
        attn = _dsa_attention(seq3(q), seq3(qi), seq3(wi), seq3(ki), seq3(k),
                              vt.reshape(bsz, tp // tm_seq, N_KV_HEADS * VT_ROWS, tm_seq),
                              t_valid, topk, tm_seq)
        attn = attn.reshape(rows, N_HEADS * HEAD_DIM)

        rnn = _rglru(h, vec(mix_norm_g[l]), w_rnn[l], rnn_conv_w[l], vec(rnn_conv_b[l]), wa_b[l],
                     vec(rnn_ba[l]), wx_b[l], vec(rnn_bx[l]), vec(rnn_lambda[l]), tm_seq)
        cnv = _conformer(h, vec(mix_norm_g[l]), w_cnv[l], conv_dw_w[l], vec(conv_dw_b[l]),
                         vec(conv_ln_g[l]), vec(conv_ln_b[l]), tm_seq)
        h2 = _merge(h2, attn, rnn.reshape(rows, D_MODEL), cnv.reshape(rows, D_MODEL), vec(mix_norm_g[l]),
                    w_gate[l], w_oa[l], w_or[l], w_oc[l], w_ot[l], tm_flat)
        h2 = _ffn(h2, vec(ffn_norm_g[l]), w_fg[l], w_fu[l], w_fd[l], tm_flat)
        h = h2.reshape(bsz, tp, D_MODEL)

    return h[:, N_META:t_valid]
```

```python
import functools

import jax
import jax.numpy as jnp
import numpy as np
from jax import lax
from jax.experimental import pallas as pl
from jax.experimental.pallas import tpu as pltpu

D_MODEL = 1024
CHUNK = 64
N_META = 16
N_HEADS = 8
N_KV_HEADS = 2
HEAD_DIM = 128
KV_GROUP = N_HEADS // N_KV_HEADS
IDX_HEADS = 8
IDX_DIM = 64
TOPK_MAX = 256
ROPE_THETA = 10000.0
RNN_BLOCKS = 8
RNN_BLOCK_DIM = D_MODEL // RNN_BLOCKS
RNN_CONV = 4
LRU_C = 8.0
CONV_KERNEL = 31
NORM_EPS = 1e-6

LANES = 128
SUBLANES = 8
PACKED_ROWS = 16
TQ = 256
VMEM_LIMIT = 56 * 1024 * 1024
LOG2E = 1.4426950408889634

HALF_BITS = 16
HALF_MASK = 2 ** HALF_BITS - 1
I16_MIN = -(2 ** (HALF_BITS - 1))
I32_MAX = 2 ** 31 - 1
KEY_MASKED = -2139095041
NEG_BIG = -(2.0 ** 100)
VT_ROWS = HEAD_DIM + PACKED_ROWS

F32 = jnp.float32
BF16 = jnp.bfloat16


def _rms(x, g):
    return x * lax.rsqrt(jnp.mean(x * x, axis=-1, keepdims=True) + NORM_EPS) * g


def _expm1(y, u):
    near = jnp.where(u == 1.0, y, (u - 1.0) * y / jnp.log(jnp.where(u == 1.0, 2.0, u)))
    return jnp.where(y > -0.5, near, u - 1.0)


def _const_spec(shape):
    nd = len(shape)
    return pl.BlockSpec(shape, lambda *_: (0,) * nd, pipeline_mode=pl.Buffered(1))


def _pick_tile(n, candidates):
    for c in candidates:
        if n % c == 0:
            return c
    raise ValueError(f"no tile for {n}")


def _attn_proj_kernel(x_ref, g_ref, w_ref, qg_ref, kg_ref, cosa_ref, sina_ref, cosi_ref, sini_ref,
                      q_ref, k_ref, vt_ref, qi_ref, ki_ref, wi_ref, *, idx_scale):
    n = _rms(x_ref[...], g_ref[...]).astype(BF16)
    cosa, sina = cosa_ref[...], sina_ref[...]
    cosi, sini = cosi_ref[...], sini_ref[...]
    lane = lax.broadcasted_iota(jnp.int32, cosi.shape, 1)
    low_half = (lane % IDX_DIM) < (IDX_DIM // 2)

    def rot_attn(a, g):
        a = _rms(a, g)
        return a * cosa + pltpu.roll(a, HEAD_DIM // 2, axis=1) * sina

    def rot_idx(a):
        partner = jnp.where(low_half, pltpu.roll(a, LANES - IDX_DIM // 2, axis=1),
                            pltpu.roll(a, IDX_DIM // 2, axis=1))
        return a * cosi + partner * sini

    q_w = N_HEADS * HEAD_DIM
    kv_w = N_KV_HEADS * HEAD_DIM
    qi_w = IDX_HEADS * IDX_DIM
    o_k, o_v, o_qi, o_ki = q_w, q_w + kv_w, q_w + 2 * kv_w, q_w + 2 * kv_w + qi_w

    q_scale = (HEAD_DIM ** -0.5) * LOG2E
    pq = jnp.dot(n, w_ref[:, 0:q_w], preferred_element_type=F32)
    for h in range(N_HEADS):
        sl = slice(h * HEAD_DIM, (h + 1) * HEAD_DIM)
        q_ref[:, sl] = (rot_attn(pq[:, sl], qg_ref[...]) * q_scale).astype(BF16)
    pk = jnp.dot(n, w_ref[:, o_k:o_qi], preferred_element_type=F32)
    for h in range(N_KV_HEADS):
        sl = slice(h * HEAD_DIM, (h + 1) * HEAD_DIM)
        k_ref[:, sl] = rot_attn(pk[:, sl], kg_ref[...]).astype(BF16)
        vt_ref[h * VT_ROWS:h * VT_ROWS + HEAD_DIM, :] = (
            pk[:, kv_w + h * HEAD_DIM:kv_w + (h + 1) * HEAD_DIM].T.astype(BF16))
        vt_ref[h * VT_ROWS + HEAD_DIM:(h + 1) * VT_ROWS, :] = jnp.ones((PACKED_ROWS, pk.shape[0]), BF16)
    pi = jnp.dot(n, w_ref[:, o_qi:], preferred_element_type=F32)
    for c in range(qi_w // LANES):
        sl = slice(c * LANES, (c + 1) * LANES)
        qi_ref[:, sl] = rot_idx(pi[:, sl]).astype(BF16)
    last = pi[:, qi_w:]
    ki = rot_idx(last)
    ki_ref[...] = jnp.where(lane < IDX_DIM, ki, pltpu.roll(ki, IDX_DIM, axis=1)).astype(BF16)
    wi_ref[...] = last * idx_scale


def _attn_proj(h2, g, w, qg, kg, tabs, tm, tiles_per_seq):
    rows = h2.shape[0]
    wcols = w.shape[1]
    row = lambda width: pl.BlockSpec((tm, width), lambda i: (i, 0))
    tab = pl.BlockSpec((tm, LANES), lambda i: (i % tiles_per_seq, 0))
    idx_scale = (IDX_HEADS ** -0.5) * (IDX_DIM ** -0.5)
    return pl.pallas_call(
        functools.partial(_attn_proj_kernel, idx_scale=idx_scale),
        grid=(rows // tm,),
        in_specs=[row(D_MODEL), _const_spec((1, D_MODEL)), _const_spec((D_MODEL, wcols)),
                  _const_spec((1, HEAD_DIM)), _const_spec((1, HEAD_DIM)), tab, tab, tab, tab],
        out_specs=[row(N_HEADS * HEAD_DIM), row(N_KV_HEADS * HEAD_DIM),
                   pl.BlockSpec((None, N_KV_HEADS * VT_ROWS, tm), lambda i: (i, 0, 0)),
                   row(IDX_HEADS * IDX_DIM), row(LANES), row(LANES)],
        out_shape=[jax.ShapeDtypeStruct((rows, N_HEADS * HEAD_DIM), BF16),
                   jax.ShapeDtypeStruct((rows, N_KV_HEADS * HEAD_DIM), BF16),
                   jax.ShapeDtypeStruct((rows // tm, N_KV_HEADS * VT_ROWS, tm), BF16),
                   jax.ShapeDtypeStruct((rows, IDX_HEADS * IDX_DIM), BF16),
                   jax.ShapeDtypeStruct((rows, LANES), BF16),
                   jax.ShapeDtypeStruct((rows, LANES), F32)],
        compiler_params=pltpu.CompilerParams(dimension_semantics=("arbitrary",),
                                             vmem_limit_bytes=VMEM_LIMIT),
        name="attn_proj",
    )(h2, g, w, qg, kg, *tabs)


def _dsa_kernel(q_ref, qi_ref, wi_ref, ki_ref, k_ref, vt_ref, out_ref,
                key_ref, hi_ref, lo_ref, bias_ref, qt_ref, qit_ref, acc_ref, m_ref, sa_ref, sb_ref, tri_ref,
                *, n_valid, topk, tk):
    tq = TQ
    gw = KV_GROUP * tq
    t0 = pl.program_id(1) * tq
    tcol = t0 + lax.broadcasted_iota(jnp.int32, (1, tq), 1)
    kb = jnp.where(tcol < N_META, N_META,
                   N_META + CHUNK * (1 + jnp.right_shift(tcol - N_META, CHUNK.bit_length() - 1)))
    kb = jnp.where(tcol < n_valid, jnp.minimum(kb, n_valid), n_valid)
    kext = jnp.minimum(n_valid, N_META + CHUNK * (1 + (t0 + tq - 1 - N_META) // CHUNK))
    n_tiles = (kext + tk - 1) // tk

    @pl.when((pl.program_id(0) == 0) & (pl.program_id(1) == 0))
    def _():
        tri_ref[...] = jnp.where(lax.broadcasted_iota(jnp.int32, (tk, tk), 0)
                                 <= lax.broadcasted_iota(jnp.int32, (tk, tk), 1), 1.0, 0.0).astype(BF16)

    for h in range(N_HEADS):
        qh = q_ref[:, h * HEAD_DIM:(h + 1) * HEAD_DIM].astype(F32).T
        qt_ref[h // KV_GROUP, :, (h % KV_GROUP) * tq:(h % KV_GROUP + 1) * tq] = qh.astype(BF16)
    row = lax.broadcasted_iota(jnp.int32, (LANES, tq), 0)
    for hp in range(IDX_HEADS // 2):
        pair = qi_ref[:, hp * LANES:(hp + 1) * LANES].astype(F32).T
        qit_ref[:, (2 * hp) * tq:(2 * hp + 1) * tq] = jnp.where(row < IDX_DIM, pair, 0.0).astype(BF16)
        qit_ref[:, (2 * hp + 1) * tq:(2 * hp + 2) * tq] = jnp.where(row >= IDX_DIM, pair, 0.0).astype(BF16)
    w = wi_ref[...].T[IDX_DIM:IDX_DIM + IDX_HEADS, :]

    def score_tile(i, carry):
        r0 = pl.multiple_of(i * tk, tk)
        kt = ki_ref[pl.ds(r0, tk), :]
        acc = jnp.zeros((tk, tq), F32)
        for hp in range(IDX_HEADS // 2):
            lg = jnp.dot(kt, qit_ref[:, hp * 2 * tq:(hp + 1) * 2 * tq],
                         preferred_element_type=F32)
            for hh in range(2):
                h = 2 * hp + hh
                acc = acc + jnp.maximum(lg[:, hh * tq:(hh + 1) * tq], 0.0) * w[h:h + 1, :]
        acc = jnp.where(acc == 0.0, 0.0, acc)
        pos = r0 + lax.broadcasted_iota(jnp.int32, (tk, tq), 0)
        bits = pltpu.bitcast(acc, jnp.int32)
        key = jnp.where(bits >= 0, bits, bits ^ I32_MAX)
        key = jnp.where(pos < kb, key, KEY_MASKED)
        key_ref[pl.ds(r0, tk), :] = key
        hi_ref[pl.ds(r0, tk), :] = lax.shift_right_arithmetic(key, HALF_BITS).astype(jnp.int16)
        return carry

    lax.fori_loop(0, n_tiles, score_tile, 0)

    def search16(ref, kth):
        def count_ge(cand):
            c16 = cand.astype(jnp.int16)

            def body(i, acc):
                r0 = pl.multiple_of(i * tk, tk)
                ind = jnp.where(ref[pl.ds(r0, tk), :] >= c16, jnp.bfloat16(1), jnp.bfloat16(0))
                parts = [ind[r * PACKED_ROWS:(r + 1) * PACKED_ROWS] for r in range(tk // PACKED_ROWS)]
                while len(parts) > 1:
                    nxt = [parts[r] + parts[r + 1] for r in range(0, len(parts) - 1, 2)]
                    parts = nxt + ([parts[-1]] if len(parts) % 2 else [])
                return acc + parts[0].astype(F32)

            acc = lax.fori_loop(0, n_tiles, body, jnp.zeros((PACKED_ROWS, tq), F32))
            return acc.sum(axis=0, keepdims=True)

        def step(it, carry):
            lo, c_lo = carry
            cand = lo + jnp.left_shift(jnp.int32(1), HALF_BITS - 1 - it)
            c = count_ge(cand)
            return jnp.where(c >= kth, cand, lo), jnp.where(c >= kth, c, c_lo)

        start = (jnp.full((1, tq), I16_MIN, jnp.int32), jnp.full((1, tq), n_tiles * tk, jnp.int32).astype(F32))
        return lax.fori_loop(0, HALF_BITS, step, start)

    tau_hi, c_high = search16(hi_ref, jnp.full((1, tq), float(topk), F32))

    def low_tile(i, acc):
        r0 = pl.multiple_of(i * tk, tk)
        blk = key_ref[pl.ds(r0, tk), :]
        hi = lax.shift_right_arithmetic(blk, HALF_BITS)
        low = jnp.where(hi == tau_hi, (blk & HALF_MASK) + I16_MIN, I16_MIN)
        lo_ref[pl.ds(r0, tk), :] = low.astype(jnp.int16)
        above = jnp.where(hi > tau_hi, 1, 0).astype(jnp.int32)
        return acc + above.reshape(tk // SUBLANES, SUBLANES, tq).sum(axis=0)

    c_above = lax.fori_loop(0, n_tiles, low_tile, jnp.zeros((SUBLANES, tq), jnp.int32))
    c_above = c_above.sum(axis=0, keepdims=True)
    c_above = c_above.astype(F32)
    tau_lo, c_low = search16(lo_ref, topk - c_above)
    tau = tau_hi * (HALF_MASK + 1) + (tau_lo - I16_MIN)
    c_ge = jnp.where(tau_lo == I16_MIN, c_high, c_above + c_low)
    excess = c_ge - topk
    cut = jnp.max(excess) > 0

    @pl.when(jnp.logical_not(cut))
    def _():
        def bias_tile(i, carry):
            r0 = pl.multiple_of(i * tk, tk)
            pos = r0 + lax.broadcasted_iota(jnp.int32, (tk, tq), 0)
            keep = (key_ref[pl.ds(r0, tk), :] >= tau) & (pos < kb)
            bias_ref[pl.ds(r0, tk), :] = jnp.where(keep, 0.0, NEG_BIG).astype(BF16)
            return carry

        lax.fori_loop(0, n_tiles, bias_tile, 0)

    @pl.when(cut)
    def _():
        def bias_tile(i, later):
            r0 = pl.multiple_of((n_tiles - 1 - i) * tk, tk)
            blk = key_ref[pl.ds(r0, tk), :]
            pos = r0 + lax.broadcasted_iota(jnp.int32, (tk, tq), 0)
            eq = blk == tau
            from_here = later + jnp.dot(tri_ref[...], jnp.where(eq, 1.0, 0.0).astype(BF16),
                                        preferred_element_type=F32)
            keep = ((blk > tau) | (eq & (from_here > excess))) & (pos < kb)
            bias_ref[pl.ds(r0, tk), :] = jnp.where(keep, 0.0, NEG_BIG).astype(BF16)
            return from_here[0:1, :]

        lax.fori_loop(0, n_tiles, bias_tile, jnp.zeros((1, tq), F32))

    acc_ref[...] = jnp.zeros_like(acc_ref)
    m_ref[...] = jnp.full(m_ref.shape, NEG_BIG, F32)

    def scores(t, s_ref):
        r0 = pl.multiple_of(t * tk, tk)
        for n in range(N_KV_HEADS):
            kt = k_ref[pl.ds(r0, tk), n * HEAD_DIM:(n + 1) * HEAD_DIM]
            s_ref[n] = jnp.dot(kt, qt_ref[n], preferred_element_type=F32).astype(BF16)

    def softmax_pv(t, s_ref):
        r0 = pl.multiple_of(t * tk, tk)
        bt = bias_ref[pl.ds(r0, tk), :]
        bt = jnp.concatenate([bt] * KV_GROUP, axis=1)
        for n in range(N_KV_HEADS):
            s = s_ref[n] + bt
            m = m_ref[n]
            m_new = jnp.maximum(m, jnp.max(s, axis=0, keepdims=True).astype(F32))
            alpha = jnp.exp2(m - m_new)
            p = jnp.exp2(s - m_new.astype(BF16))
            m_ref[n] = m_new
            vt = vt_ref[t, n * VT_ROWS:(n + 1) * VT_ROWS, :]
            acc_ref[n] = alpha * acc_ref[n] + jnp.dot(vt, p, preferred_element_type=F32)

    last_tile = k_ref.shape[0] // tk - 1
    scores(0, sa_ref)

    def tile_pair(i, carry):
        t = 2 * i
        scores(t + 1, sb_ref)
        softmax_pv(t, sa_ref)
        scores(jnp.minimum(t + 2, last_tile), sa_ref)
        softmax_pv(t + 1, sb_ref)
        return carry

    lax.fori_loop(0, n_tiles // 2, tile_pair, 0)

    @pl.when(n_tiles % 2 == 1)
    def _():
        softmax_pv(n_tiles - 1, sa_ref)

    for h in range(N_HEADS):
        n, g = h // KV_GROUP, h % KV_GROUP
        inv_l = 1.0 / acc_ref[n, HEAD_DIM:HEAD_DIM + 1, g * tq:(g + 1) * tq]
        oh = acc_ref[n, 0:HEAD_DIM, g * tq:(g + 1) * tq] * inv_l
        out_ref[:, h * HEAD_DIM:(h + 1) * HEAD_DIM] = oh.T.astype(BF16)


def _dsa_attention(q, qi, wi, ki, k, vt, n_valid, topk, tk):
    bsz, tp, _ = q.shape
    gw = KV_GROUP * TQ
    qrow = lambda width: pl.BlockSpec((None, TQ, width), lambda b, j: (b, j, 0))
    seq = lambda width: pl.BlockSpec((None, tp, width), lambda b, j: (b, 0, 0))
    return pl.pallas_call(
        functools.partial(_dsa_kernel, n_valid=n_valid, topk=topk, tk=tk),
        grid=(bsz, pl.cdiv(tp, TQ)),
        in_specs=[qrow(N_HEADS * HEAD_DIM), qrow(IDX_HEADS * IDX_DIM), qrow(LANES), seq(LANES),
                  seq(N_KV_HEADS * HEAD_DIM),
                  pl.BlockSpec((None, tp // tk, N_KV_HEADS * VT_ROWS, tk), lambda b, j: (b, 0, 0, 0))],
        out_specs=qrow(N_HEADS * HEAD_DIM),
        out_shape=jax.ShapeDtypeStruct((bsz, tp, N_HEADS * HEAD_DIM), BF16),
        scratch_shapes=[pltpu.VMEM((tp, TQ), jnp.int32), pltpu.VMEM((tp, TQ), jnp.int16),
                        pltpu.VMEM((tp, TQ), jnp.int16), pltpu.VMEM((tp, TQ), BF16),
                        pltpu.VMEM((N_KV_HEADS, HEAD_DIM, gw), BF16),
                        pltpu.VMEM((LANES, IDX_HEADS * TQ), BF16),
                        pltpu.VMEM((N_KV_HEADS, VT_ROWS, gw), F32),
                        pltpu.VMEM((N_KV_HEADS, 1, gw), F32),
                        pltpu.VMEM((N_KV_HEADS, tk, gw), BF16), pltpu.VMEM((N_KV_HEADS, tk, gw), BF16),
                        pltpu.VMEM((tk, tk), BF16)],
        compiler_params=pltpu.CompilerParams(dimension_semantics=("arbitrary", "arbitrary"),
                                             vmem_limit_bytes=VMEM_LIMIT),
        name="dsa_attention",
    )(q, qi, wi, ki, k, vt)


def _rglru_kernel(h_ref, g_ref, w_ref, cw_ref, cb_ref, wa_ref, ba_ref, wx_ref, bx_ref, lam_ref,
                  out_ref, xext_ref, a_ref, b_ref, state_ref):
    tm = h_ref.shape[0]
    hist = SUBLANES

    @pl.when(pl.program_id(1) == 0)
    def _():
        xext_ref[0:hist, :] = jnp.zeros((hist, D_MODEL), F32)
        state_ref[...] = jnp.zeros_like(state_ref)

    n = _rms(h_ref[...], g_ref[...]).astype(BF16)
    xext_ref[hist:hist + tm, :] = jnp.dot(n, w_ref[:, 0:D_MODEL], preferred_element_type=F32)
    u = cb_ref[...] + cw_ref[0:1, :] * xext_ref[hist - RNN_CONV + 1:hist - RNN_CONV + 1 + tm, :]
    for kk in range(1, RNN_CONV):
        o = hist - RNN_CONV + 1 + kk
        u = u + cw_ref[kk:kk + 1, :] * xext_ref[o:o + tm, :]
    xext_ref[0:hist, :] = xext_ref[tm:tm + hist, :]

    ub = u.astype(BF16)
    rs, is_ = [], []
    for blk in range(RNN_BLOCKS):
        sl = slice(blk * RNN_BLOCK_DIM, (blk + 1) * RNN_BLOCK_DIM)
        rs.append(jnp.dot(ub[:, sl], wa_ref[blk], preferred_element_type=F32))
        is_.append(jnp.dot(ub[:, sl], wx_ref[blk], preferred_element_type=F32))
    r = jax.nn.sigmoid(jnp.concatenate(rs, axis=1) + ba_ref[...])
    ig = jax.nn.sigmoid(jnp.concatenate(is_, axis=1) + bx_ref[...])
    nl = -lam_ref[...]
    softplus = jnp.maximum(nl, 0.0) + jnp.log1p(jnp.exp(-jnp.abs(nl)))
    log_a = -LRU_C * r * softplus
    a_all = jnp.exp(log_a)
    b_all = jnp.sqrt(-_expm1(2.0 * log_a, a_all * a_all)) * (ig * u)
    nchunk = D_MODEL // LANES
    for c in range(nchunk):
        a_ref[c] = a_all[:, c * LANES:(c + 1) * LANES]
        b_ref[c] = b_all[:, c * LANES:(c + 1) * LANES]

    seg = tm // SCAN_SEGMENTS
    groups = SCAN_SEGMENTS // SUBLANES

    def step(i, carry):
        out = []
        for v in range(groups):
            prod, loc = carry[2 * v], carry[2 * v + 1]
            rows = pl.ds(v * SUBLANES * seg + i, SUBLANES, stride=seg)
            a = a_ref[:, rows, :]
            prod = a * prod
            loc = a * loc + b_ref[:, rows, :]
            a_ref[:, rows, :] = prod
            b_ref[:, rows, :] = loc
            out += [prod, loc]
        return tuple(out)

    init = (jnp.ones((nchunk, SUBLANES, LANES), F32), jnp.zeros((nchunk, SUBLANES, LANES), F32)) * groups
    ends = lax.fori_loop(0, seg, step, init)
    gel = jax.nn.gelu(jnp.dot(n, w_ref[:, D_MODEL:], preferred_element_type=F32))
    hcur = state_ref[...]
    for s in range(SCAN_SEGMENTS):
        rs = slice(s * seg, (s + 1) * seg)
        hs = a_ref[:, rs, :] * hcur + b_ref[:, rs, :]
        for c in range(nchunk):
            cs = slice(c * LANES, (c + 1) * LANES)
            out_ref[rs, cs] = (hs[c] * gel[rs, cs]).astype(BF16)
        v, r = s // SUBLANES, s % SUBLANES
        hcur = ends[2 * v][:, r:r + 1, :] * hcur + ends[2 * v + 1][:, r:r + 1, :]
    state_ref[...] = hcur


def _rglru(h3, g, w, cw, cb, wa, ba, wx, bx, lam, tm):
    bsz, tp, _ = h3.shape
    row = pl.BlockSpec((None, tm, D_MODEL), lambda b, t: (b, t, 0))
    vec = _const_spec((1, D_MODEL))
    blkw = _const_spec((RNN_BLOCKS, RNN_BLOCK_DIM, RNN_BLOCK_DIM))
    return pl.pallas_call(
        _rglru_kernel,
        grid=(bsz, tp // tm),
        in_specs=[row, vec, _const_spec((D_MODEL, 2 * D_MODEL)), _const_spec((RNN_CONV, D_MODEL)), vec,
                  blkw, vec, blkw, vec, vec],
        out_specs=row,
        out_shape=jax.ShapeDtypeStruct((bsz, tp, D_MODEL), BF16),
        scratch_shapes=[pltpu.VMEM((tm + SUBLANES, D_MODEL), F32),
                        pltpu.VMEM((D_MODEL // LANES, tm, LANES), F32),
                        pltpu.VMEM((D_MODEL // LANES, tm, LANES), F32),
                        pltpu.VMEM((D_MODEL // LANES, 1, LANES), F32)],
        compiler_params=pltpu.CompilerParams(dimension_semantics=("arbitrary", "arbitrary"),
                                             vmem_limit_bytes=VMEM_LIMIT),
        name="rglru",
    )(h3, g, w, cw, cb, wa, ba, wx, bx, lam)


SCAN_SEGMENTS = 16
CONV_HIST = 32
CONV_ROWS = 128


def _conformer_kernel(h_ref, g_ref, w_ref, dw_ref, db_ref, lg_ref, lb_ref, out_ref, xs_ref, y_ref):
    tm = h_ref.shape[0]

    @pl.when(pl.program_id(1) == 0)
    def _():
        xs_ref[0, 0:CONV_HIST, :] = jnp.zeros((CONV_HIST, D_MODEL), F32)

    n = _rms(h_ref[...], g_ref[...]).astype(BF16)
    a = jnp.dot(n, w_ref[:, 0:D_MODEL], preferred_element_type=F32)
    gate = jnp.dot(n, w_ref[:, D_MODEL:], preferred_element_type=F32)
    xs_ref[0, CONV_HIST:CONV_HIST + tm, :] = a * jax.nn.sigmoid(gate)
    span = tm + CONV_HIST - SUBLANES
    for s in range(1, SUBLANES):
        xs_ref[s, 0:span, :] = xs_ref[0, s:s + span, :]

    first = CONV_HIST - CONV_KERNEL + 1
    row_chunks = range(0, tm, CONV_ROWS)
    for c0 in range(0, D_MODEL, LANES):
        cs = slice(c0, c0 + LANES)
        accs = [jnp.broadcast_to(db_ref[:, cs], (CONV_ROWS, LANES)) for _ in row_chunks]
        for kk in range(CONV_KERNEL):
            shift = (first + kk) % SUBLANES
            base = first + kk - shift
            wk = dw_ref[kk:kk + 1, cs]
            for ri, r0 in enumerate(row_chunks):
                accs[ri] = accs[ri] + wk * xs_ref[shift, base + r0:base + r0 + CONV_ROWS, cs]
        for ri, r0 in enumerate(row_chunks):
            y_ref[r0:r0 + CONV_ROWS, cs] = accs[ri]
    xs_ref[0, 0:CONV_HIST, :] = xs_ref[0, tm:tm + CONV_HIST, :]

    y = y_ref[...]
    mu = jnp.mean(y, axis=-1, keepdims=True)
    yc = y - mu
    z = yc * lax.rsqrt(jnp.mean(yc * yc, axis=-1, keepdims=True) + NORM_EPS) * lg_ref[...] + lb_ref[...]
    out_ref[...] = jax.nn.silu(z).astype(BF16)


def _conformer(h3, g, w, dw, db, lg, lb, tm):
    bsz, tp, _ = h3.shape
    row = pl.BlockSpec((None, tm, D_MODEL), lambda b, t: (b, t, 0))
    vec = _const_spec((1, D_MODEL))
    return pl.pallas_call(
        _conformer_kernel,
        grid=(bsz, tp // tm),
        in_specs=[row, vec, _const_spec((D_MODEL, 2 * D_MODEL)), _const_spec((CONV_KERNEL, D_MODEL)),
                  vec, vec, vec],
        out_specs=row,
        out_shape=jax.ShapeDtypeStruct((bsz, tp, D_MODEL), BF16),
        scratch_shapes=[pltpu.VMEM((SUBLANES, tm + CONV_HIST, D_MODEL), F32), pltpu.VMEM((tm, D_MODEL), F32)],
        compiler_params=pltpu.CompilerParams(dimension_semantics=("arbitrary", "arbitrary"),
                                             vmem_limit_bytes=VMEM_LIMIT),
        name="conformer",
    )(h3, g, w, dw, db, lg, lb)


def _merge_kernel(h_ref, at_ref, rn_ref, cv_ref, g_ref, wg_ref, wa_ref, wr_ref, wc_ref, wo_ref, out_ref):
    h = h_ref[...]
    n = _rms(h, g_ref[...]).astype(BF16)
    merged = None
    for i, (src, wref) in enumerate(((at_ref, wa_ref), (rn_ref, wr_ref), (cv_ref, wc_ref))):
        gate = jax.nn.sigmoid(jnp.dot(n, wg_ref[:, i * D_MODEL:(i + 1) * D_MODEL],
                                      preferred_element_type=F32))
        term = gate * jnp.dot(src[...], wref[...], preferred_element_type=F32)
        merged = term if merged is None else merged + term
    out_ref[...] = h + jnp.dot(merged.astype(BF16), wo_ref[...], preferred_element_type=F32)


def _merge(h2, attn, rnn, cnv, g, wg, wa, wr, wc, wo, tm):
    rows = h2.shape[0]
    row = pl.BlockSpec((tm, D_MODEL), lambda i: (i, 0))
    sq = _const_spec((D_MODEL, D_MODEL))
    return pl.pallas_call(
        _merge_kernel,
        grid=(rows // tm,),
        in_specs=[row, row, row, row, _const_spec((1, D_MODEL)), _const_spec((D_MODEL, 3 * D_MODEL)),
                  sq, sq, sq, sq],
        out_specs=row,
        out_shape=jax.ShapeDtypeStruct((rows, D_MODEL), F32),
        compiler_params=pltpu.CompilerParams(dimension_semantics=("arbitrary",),
                                             vmem_limit_bytes=VMEM_LIMIT),
        name="merge",
    )(h2, attn, rnn, cnv, g, wg, wa, wr, wc, wo)


def _ffn_kernel(h_ref, g_ref, wg_ref, wu_ref, wd_ref, out_ref):
    h = h_ref[...]
    f = _rms(h, g_ref[...]).astype(BF16)
    gate = jnp.dot(f, wg_ref[...], preferred_element_type=F32)
    up = jnp.dot(f, wu_ref[...], preferred_element_type=F32)
    act = (jax.nn.silu(gate) * up).astype(BF16)
    out_ref[...] = h + jnp.dot(act, wd_ref[...], preferred_element_type=F32)


def _ffn(h2, g, wg, wu, wd, tm):
    rows = h2.shape[0]
    dff = wg.shape[1]
    row = pl.BlockSpec((tm, D_MODEL), lambda i: (i, 0))
    return pl.pallas_call(
        _ffn_kernel,
        grid=(rows // tm,),
        in_specs=[row, _const_spec((1, D_MODEL)), _const_spec((D_MODEL, dff)), _const_spec((D_MODEL, dff)),
                  _const_spec((dff, D_MODEL))],
        out_specs=row,
        out_shape=jax.ShapeDtypeStruct((rows, D_MODEL), F32),
        compiler_params=pltpu.CompilerParams(dimension_semantics=("arbitrary",),
                                             vmem_limit_bytes=VMEM_LIMIT),
        name="ffn",
    )(h2, g, wg, wu, wd)


def _rope_tables(n, dim):
    inv = ROPE_THETA ** (-jnp.arange(0, dim, 2, dtype=F32) / dim)
    ang = jnp.arange(n, dtype=F32)[:, None] * inv[None, :]
    return jnp.cos(ang), jnp.sin(ang)


def kernel(x, meta, mix_norm_g, w_in, q_norm_g, k_norm_g, rnn_conv_w, rnn_conv_b, rnn_wa, rnn_ba, rnn_wx, rnn_bx, rnn_lambda, conv_dw_w, conv_dw_b, conv_ln_g, conv_ln_b, w_o_attn, w_o_rnn, w_o_conv, w_out, ffn_norm_g, w_ffn_gate, w_ffn_up, w_ffn_down):
    bsz, seq, _ = x.shape
    depth = w_in.shape[0]
    t_valid = seq + N_META
    tp = -(-t_valid // LANES) * LANES
    topk = min(TOPK_MAX, seq // 4)
    rows = bsz * tp
    tm_seq = _pick_tile(tp, (384, 256, 128))
    tm_flat = _pick_tile(rows, (512, 384, 256, 128))

    q_w, kv_w, qi_w = N_HEADS * HEAD_DIM, N_KV_HEADS * HEAD_DIM, IDX_HEADS * IDX_DIM
    o = np.cumsum([0, q_w, kv_w, kv_w, qi_w, IDX_DIM, IDX_HEADS, D_MODEL, D_MODEL, 2 * D_MODEL, 3 * D_MODEL])
    pad_w = LANES - IDX_DIM - IDX_HEADS
    w_attn = jnp.concatenate([w_in[:, :, o[0]:o[6]], jnp.zeros((depth, D_MODEL, pad_w), w_in.dtype)],
                             axis=2).astype(BF16)
    w_rnn = w_in[:, :, o[6]:o[8]].astype(BF16)
    w_cnv = w_in[:, :, o[8]:o[9]].astype(BF16)
    w_gate = w_in[:, :, o[9]:o[10]].astype(BF16)
    wa_b, wx_b = rnn_wa.astype(BF16), rnn_wx.astype(BF16)
    w_oa, w_or, w_oc, w_ot = (a.astype(BF16) for a in (w_o_attn, w_o_rnn, w_o_conv, w_out))
    w_fg, w_fu, w_fd = (a.astype(BF16) for a in (w_ffn_gate, w_ffn_up, w_ffn_down))

    cos_a, sin_a = _rope_tables(tp, HEAD_DIM)
    cos_i, sin_i = _rope_tables(tp, IDX_DIM)
    tabs = (jnp.concatenate([cos_a, cos_a], axis=1), jnp.concatenate([-sin_a, sin_a], axis=1),
            jnp.concatenate([cos_i] * 4, axis=1), jnp.concatenate([-sin_i, sin_i] * 2, axis=1))

    h = jnp.concatenate([jnp.broadcast_to(meta[None].astype(x.dtype), (bsz, N_META, D_MODEL)), x,
                         jnp.zeros((bsz, tp - t_valid, D_MODEL), x.dtype)], axis=1)
    vec = lambda a: a.reshape(1, -1)

    for l in range(depth):
        h2 = h.reshape(rows, D_MODEL)
        q, k, vt, qi, ki, wi = _attn_proj(h2, vec(mix_norm_g[l]), w_attn[l], vec(q_norm_g[l]),
                                          vec(k_norm_g[l]), tabs, tm_seq, tp // tm_seq)
        seq3 = lambda a: a.reshape(bsz, tp, a.shape[-1])
        attn = _dsa_attention(seq3(q), seq3(qi), seq3(wi), seq3(ki), seq3(k),
                              vt.reshape(bsz, tp // tm_seq, N_KV_HEADS * VT_ROWS, tm_seq),
                              t_valid, topk, tm_seq)
        attn = attn.reshape(rows, N_HEADS * HEAD_DIM)

        rnn = _rglru(h, vec(mix_norm_g[l]), w_rnn[l], rnn_conv_w[l], vec(rnn_conv_b[l]), wa_b[l],
                     vec(rnn_ba[l]), wx_b[l], vec(rnn_bx[l]), vec(rnn_lambda[l]), tm_seq)
        cnv = _conformer(h, vec(mix_norm_g[l]), w_cnv[l], conv_dw_w[l], vec(conv_dw_b[l]),
                         vec(conv_ln_g[l]), vec(conv_ln_b[l]), tm_seq)
        h2 = _merge(h2, attn, rnn.reshape(rows, D_MODEL), cnv.reshape(rows, D_MODEL), vec(mix_norm_g[l]),
                    w_gate[l], w_oa[l], w_or[l], w_oc[l], w_ot[l], tm_flat)
        h2 = _ffn(h2, vec(ffn_norm_g[l]), w_fg[l], w_fu[l], w_fd[l], tm_flat)
        h = h2.reshape(bsz, tp, D_MODEL)

    return h[:, N_META:t_valid]
```

```python
import functools

import jax
import jax.numpy as jnp
import numpy as np
from jax import lax
from jax.experimental import pallas as pl
from jax.experimental.pallas import tpu as pltpu

D_MODEL = 1024
CHUNK = 64
N_META = 16
N_HEADS = 8
N_KV_HEADS = 2
HEAD_DIM = 128
KV_GROUP = N_HEADS // N_KV_HEADS
IDX_HEADS = 8
IDX_DIM = 64
TOPK_MAX = 256
ROPE_THETA = 10000.0
RNN_BLOCKS = 8
RNN_BLOCK_DIM = D_MODEL // RNN_BLOCKS
RNN_CONV = 4
LRU_C = 8.0
CONV_KERNEL = 31
NORM_EPS = 1e-6

LANES = 128
SUBLANES = 8
PACKED_ROWS = 16
TQ = 256
VMEM_LIMIT = 56 * 1024 * 1024
LOG2E = 1.4426950408889634

HALF_BITS = 16
HALF_MASK = 2 ** HALF_BITS - 1
I16_MIN = -(2 ** (HALF_BITS - 1))
I32_MAX = 2 ** 31 - 1
KEY_MASKED = -2139095041
NEG_BIG = -(2.0 ** 100)
VT_ROWS = HEAD_DIM + PACKED_ROWS

F32 = jnp.float32
BF16 = jnp.bfloat16


def _rms(x, g):
    return x * lax.rsqrt(jnp.mean(x * x, axis=-1, keepdims=True) + NORM_EPS) * g


def _expm1(y, u):
    near = jnp.where(u == 1.0, y, (u - 1.0) * y / jnp.log(jnp.where(u == 1.0, 2.0, u)))
    return jnp.where(y > -0.5, near, u - 1.0)


def _const_spec(shape):
    nd = len(shape)
    return pl.BlockSpec(shape, lambda *_: (0,) * nd, pipeline_mode=pl.Buffered(1))


def _pick_tile(n, candidates):
    for c in candidates:
        if n % c == 0:
            return c
    raise ValueError(f"no tile for {n}")


def _attn_proj_kernel(x_ref, g_ref, w_ref, qg_ref, kg_ref, cosa_ref, sina_ref, cosi_ref, sini_ref,
                      q_ref, k_ref, vt_ref, qi_ref, ki_ref, wi_ref, *, idx_scale):
    n = _rms(x_ref[...], g_ref[...]).astype(BF16)
    cosa, sina = cosa_ref[...], sina_ref[...]
    cosi, sini = cosi_ref[...], sini_ref[...]
    lane = lax.broadcasted_iota(jnp.int32, cosi.shape, 1)
    low_half = (lane % IDX_DIM) < (IDX_DIM // 2)

    def rot_attn(a, g):
        a = _rms(a, g)
        return a * cosa + pltpu.roll(a, HEAD_DIM // 2, axis=1) * sina

    def rot_idx(a):
        partner = jnp.where(low_half, pltpu.roll(a, LANES - IDX_DIM // 2, axis=1),
                            pltpu.roll(a, IDX_DIM // 2, axis=1))
        return a * cosi + partner * sini

    q_w = N_HEADS * HEAD_DIM
    kv_w = N_KV_HEADS * HEAD_DIM
    qi_w = IDX_HEADS * IDX_DIM
    o_k, o_v, o_qi, o_ki = q_w, q_w + kv_w, q_w + 2 * kv_w, q_w + 2 * kv_w + qi_w

    q_scale = (HEAD_DIM ** -0.5) * LOG2E
    pq = jnp.dot(n, w_ref[:, 0:q_w], preferred_element_type=F32)
    for h in range(N_HEADS):
        sl = slice(h * HEAD_DIM, (h + 1) * HEAD_DIM)
        q_ref[:, sl] = (rot_attn(pq[:, sl], qg_ref[...]) * q_scale).astype(BF16)
    pk = jnp.dot(n, w_ref[:, o_k:o_qi], preferred_element_type=F32)
    for h in range(N_KV_HEADS):
        sl = slice(h * HEAD_DIM, (h + 1) * HEAD_DIM)
        k_ref[:, sl] = rot_attn(pk[:, sl], kg_ref[...]).astype(BF16)
        vt_ref[h * VT_ROWS:h * VT_ROWS + HEAD_DIM, :] = (
            pk[:, kv_w + h * HEAD_DIM:kv_w + (h + 1) * HEAD_DIM].T.astype(BF16))
        vt_ref[h * VT_ROWS + HEAD_DIM:(h + 1) * VT_ROWS, :] = jnp.ones((PACKED_ROWS, pk.shape[0]), BF16)
    pi = jnp.dot(n, w_ref[:, o_qi:], preferred_element_type=F32)
    for c in range(qi_w // LANES):
        sl = slice(c * LANES, (c + 1) * LANES)
        qi_ref[:, sl] = rot_idx(pi[:, sl]).astype(BF16)
    last = pi[:, qi_w:]
    ki = rot_idx(last)
    ki_ref[...] = jnp.where(lane < IDX_DIM, ki, pltpu.roll(ki, IDX_DIM, axis=1)).astype(BF16)
    wi_ref[...] = last * idx_scale


def _attn_proj(h2, g, w, qg, kg, tabs, tm, tiles_per_seq):
    rows = h2.shape[0]
    wcols = w.shape[1]
    row = lambda width: pl.BlockSpec((tm, width), lambda i: (i, 0))
    tab = pl.BlockSpec((tm, LANES), lambda i: (i % tiles_per_seq, 0))
    idx_scale = (IDX_HEADS ** -0.5) * (IDX_DIM ** -0.5)
    return pl.pallas_call(
        functools.partial(_attn_proj_kernel, idx_scale=idx_scale),
        grid=(rows // tm,),
        in_specs=[row(D_MODEL), _const_spec((1, D_MODEL)), _const_spec((D_MODEL, wcols)),
                  _const_spec((1, HEAD_DIM)), _const_spec((1, HEAD_DIM)), tab, tab, tab, tab],
        out_specs=[row(N_HEADS * HEAD_DIM), row(N_KV_HEADS * HEAD_DIM),
                   pl.BlockSpec((None, N_KV_HEADS * VT_ROWS, tm), lambda i: (i, 0, 0)),
                   row(IDX_HEADS * IDX_DIM), row(LANES), row(LANES)],
        out_shape=[jax.ShapeDtypeStruct((rows, N_HEADS * HEAD_DIM), BF16),
                   jax.ShapeDtypeStruct((rows, N_KV_HEADS * HEAD_DIM), BF16),
                   jax.ShapeDtypeStruct((rows // tm, N_KV_HEADS * VT_ROWS, tm), BF16),
                   jax.ShapeDtypeStruct((rows, IDX_HEADS * IDX_DIM), BF16),
                   jax.ShapeDtypeStruct((rows, LANES), BF16),
                   jax.ShapeDtypeStruct((rows, LANES), F32)],
        compiler_params=pltpu.CompilerParams(dimension_semantics=("arbitrary",),
                                             vmem_limit_bytes=VMEM_LIMIT),
        name="attn_proj",
    )(h2, g, w, qg, kg, *tabs)


def _dsa_kernel(q_ref, qi_ref, wi_ref, ki_ref, k_ref, vt_ref, out_ref,
                key_ref, hi_ref, lo_ref, bias_ref, qt_ref, qit_ref, acc_ref, m_ref, sa_ref, sb_ref, tri_ref,
                *, n_valid, topk, tk):
    tq = TQ
    gw = KV_GROUP * tq
    t0 = pl.program_id(1) * tq
    tcol = t0 + lax.broadcasted_iota(jnp.int32, (1, tq), 1)
    kb = jnp.where(tcol < N_META, N_META,
                   N_META + CHUNK * (1 + jnp.right_shift(tcol - N_META, CHUNK.bit_length() - 1)))
    kb = jnp.where(tcol < n_valid, jnp.minimum(kb, n_valid), n_valid)
    kext = jnp.minimum(n_valid, N_META + CHUNK * (1 + (t0 + tq - 1 - N_META) // CHUNK))
    n_tiles = (kext + tk - 1) // tk

    @pl.when((pl.program_id(0) == 0) & (pl.program_id(1) == 0))
    def _():
        tri_ref[...] = jnp.where(lax.broadcasted_iota(jnp.int32, (tk, tk), 0)
                                 <= lax.broadcasted_iota(jnp.int32, (tk, tk), 1), 1.0, 0.0).astype(BF16)

    def scores(t, s_ref):
        r0 = pl.multiple_of(t * tk, tk)
        for n in range(N_KV_HEADS):
            kt = k_ref[pl.ds(r0, tk), n * HEAD_DIM:(n + 1) * HEAD_DIM]
            s_ref[n] = jnp.dot(kt, qt_ref[n], preferred_element_type=F32).astype(BF16)

    for h in range(N_HEADS):
        qh = q_ref[:, h * HEAD_DIM:(h + 1) * HEAD_DIM].astype(F32).T
        qt_ref[h // KV_GROUP, :, (h % KV_GROUP) * tq:(h % KV_GROUP + 1) * tq] = qh.astype(BF16)
    scores(0, sa_ref)
    row = lax.broadcasted_iota(jnp.int32, (LANES, tq), 0)
    for hp in range(IDX_HEADS // 2):
        pair = qi_ref[:, hp * LANES:(hp + 1) * LANES].astype(F32).T
        qit_ref[:, (2 * hp) * tq:(2 * hp + 1) * tq] = jnp.where(row < IDX_DIM, pair, 0.0).astype(BF16)
        qit_ref[:, (2 * hp + 1) * tq:(2 * hp + 2) * tq] = jnp.where(row >= IDX_DIM, pair, 0.0).astype(BF16)
    w = wi_ref[...].T[IDX_DIM:IDX_DIM + IDX_HEADS, :]

    def score_tile(i, carry):
        r0 = pl.multiple_of(i * tk, tk)
        kt = ki_ref[pl.ds(r0, tk), :]
        acc = jnp.zeros((tk, tq), F32)
        for hp in range(IDX_HEADS // 2):
            lg = jnp.dot(kt, qit_ref[:, hp * 2 * tq:(hp + 1) * 2 * tq],
                         preferred_element_type=F32)
            for hh in range(2):
                h = 2 * hp + hh
                acc = acc + jnp.maximum(lg[:, hh * tq:(hh + 1) * tq], 0.0) * w[h:h + 1, :]
        acc = jnp.where(acc == 0.0, 0.0, acc)
        pos = r0 + lax.broadcasted_iota(jnp.int32, (tk, tq), 0)
        bits = pltpu.bitcast(acc, jnp.int32)
        key = jnp.where(bits >= 0, bits, bits ^ I32_MAX)
        key = jnp.where(pos < kb, key, KEY_MASKED)
        key_ref[pl.ds(r0, tk), :] = key
        hi_ref[pl.ds(r0, tk), :] = lax.shift_right_arithmetic(key, HALF_BITS).astype(jnp.int16)
        return carry

    lax.fori_loop(0, n_tiles, score_tile, 0)

    def search16(ref, kth):
        def count_ge(cand):
            c16 = cand.astype(jnp.int16)

            def body(i, acc):
                r0 = pl.multiple_of(i * tk, tk)
                ind = jnp.where(ref[pl.ds(r0, tk), :] >= c16, jnp.bfloat16(1), jnp.bfloat16(0))
                parts = [ind[r * PACKED_ROWS:(r + 1) * PACKED_ROWS] for r in range(tk // PACKED_ROWS)]
                while len(parts) > 1:
                    nxt = [parts[r] + parts[r + 1] for r in range(0, len(parts) - 1, 2)]
                    parts = nxt + ([parts[-1]] if len(parts) % 2 else [])
                return acc + parts[0].astype(F32)

            acc = lax.fori_loop(0, n_tiles, body, jnp.zeros((PACKED_ROWS, tq), F32))
            return acc.sum(axis=0, keepdims=True)

        def step(it, carry):
            lo, c_lo = carry
            cand = lo + jnp.left_shift(jnp.int32(1), HALF_BITS - 1 - it)
            c = count_ge(cand)
            return jnp.where(c >= kth, cand, lo), jnp.where(c >= kth, c, c_lo)

        start = (jnp.full((1, tq), I16_MIN, jnp.int32), jnp.full((1, tq), n_tiles * tk, jnp.int32).astype(F32))
        return lax.fori_loop(0, HALF_BITS, step, start)

    tau_hi, c_high = search16(hi_ref, jnp.full((1, tq), float(topk), F32))

    def low_tile(i, acc):
        r0 = pl.multiple_of(i * tk, tk)
        blk = key_ref[pl.ds(r0, tk), :]
        hi = lax.shift_right_arithmetic(blk, HALF_BITS)
        low = jnp.where(hi == tau_hi, (blk & HALF_MASK) + I16_MIN, I16_MIN)
        lo_ref[pl.ds(r0, tk), :] = low.astype(jnp.int16)
        above = jnp.where(hi > tau_hi, 1, 0).astype(jnp.int32)
        return acc + above.reshape(tk // SUBLANES, SUBLANES, tq).sum(axis=0)

    c_above = lax.fori_loop(0, n_tiles, low_tile, jnp.zeros((SUBLANES, tq), jnp.int32))
    c_above = c_above.sum(axis=0, keepdims=True)
    c_above = c_above.astype(F32)
    tau_lo, c_low = search16(lo_ref, topk - c_above)
    tau = tau_hi * (HALF_MASK + 1) + (tau_lo - I16_MIN)
    c_ge = jnp.where(tau_lo == I16_MIN, c_high, c_above + c_low)
    excess = c_ge - topk
    cut = jnp.max(excess) > 0

    @pl.when(jnp.logical_not(cut))
    def _():
        def bias_tile(i, carry):
            r0 = pl.multiple_of(i * tk, tk)
            pos = r0 + lax.broadcasted_iota(jnp.int32, (tk, tq), 0)
            keep = (key_ref[pl.ds(r0, tk), :] >= tau) & (pos < kb)
            bias_ref[pl.ds(r0, tk), :] = jnp.where(keep, 0.0, NEG_BIG).astype(BF16)
            return carry

        lax.fori_loop(0, n_tiles, bias_tile, 0)

    @pl.when(cut)
    def _():
        def tile_bias(t, later):
            r0 = pl.multiple_of(t * tk, tk)
            blk = key_ref[pl.ds(r0, tk), :]
            pos = r0 + lax.broadcasted_iota(jnp.int32, (tk, tq), 0)
            eq = blk == tau
            within = jnp.dot(tri_ref[...], jnp.where(eq, 1.0, 0.0).astype(BF16), preferred_element_type=F32)
            from_here = later + within
            keep = ((blk > tau) | (eq & (from_here > excess))) & (pos < kb)
            return r0, jnp.where(keep, 0.0, NEG_BIG).astype(BF16), from_here[0:1, :]

        def bias_pair(i, later):
            t = n_tiles - 1 - 2 * i
            ra, bias_a, later_a = tile_bias(t, later)
            rb, bias_b, later_b = tile_bias(jnp.maximum(t - 1, 0), later_a)
            bias_ref[pl.ds(rb, tk), :] = bias_b
            bias_ref[pl.ds(ra, tk), :] = bias_a
            return later_b

        lax.fori_loop(0, (n_tiles + 1) // 2, bias_pair, jnp.zeros((1, tq), F32))

    acc_ref[...] = jnp.zeros_like(acc_ref)
    m_ref[...] = jnp.full(m_ref.shape, NEG_BIG, F32)

    def softmax_pv(t, s_ref):
        r0 = pl.multiple_of(t * tk, tk)
        bt = bias_ref[pl.ds(r0, tk), :]
        bt = jnp.concatenate([bt] * KV_GROUP, axis=1)
        for n in range(N_KV_HEADS):
            s = s_ref[n] + bt
            m = m_ref[n]
            m_new = jnp.maximum(m, jnp.max(s, axis=0, keepdims=True).astype(F32))
            alpha = jnp.exp2(m - m_new)
            p = jnp.exp2(s - m_new.astype(BF16))
            m_ref[n] = m_new
            vt = vt_ref[t, n * VT_ROWS:(n + 1) * VT_ROWS, :]
            acc_ref[n] = alpha * acc_ref[n] + jnp.dot(vt, p, preferred_element_type=F32)

    last_tile = k_ref.shape[0] // tk - 1

    def tile_pair(i, carry):
        t = 2 * i
        scores(t + 1, sb_ref)
        softmax_pv(t, sa_ref)
        scores(jnp.minimum(t + 2, last_tile), sa_ref)
        softmax_pv(t + 1, sb_ref)
        return carry

    lax.fori_loop(0, n_tiles // 2, tile_pair, 0)

    @pl.when(n_tiles % 2 == 1)
    def _():
        softmax_pv(n_tiles - 1, sa_ref)

    for h in range(N_HEADS):
        n, g = h // KV_GROUP, h % KV_GROUP
        inv_l = 1.0 / acc_ref[n, HEAD_DIM:HEAD_DIM + 1, g * tq:(g + 1) * tq]
        oh = acc_ref[n, 0:HEAD_DIM, g * tq:(g + 1) * tq] * inv_l
        out_ref[:, h * HEAD_DIM:(h + 1) * HEAD_DIM] = oh.T.astype(BF16)


def _dsa_attention(q, qi, wi, ki, k, vt, n_valid, topk, tk):
    bsz, tp, _ = q.shape
    gw = KV_GROUP * TQ
    qrow = lambda width: pl.BlockSpec((None, TQ, width), lambda b, j: (b, j, 0))
    seq = lambda width: pl.BlockSpec((None, tp, width), lambda b, j: (b, 0, 0))
    return pl.pallas_call(
        functools.partial(_dsa_kernel, n_valid=n_valid, topk=topk, tk=tk),
        grid=(bsz, pl.cdiv(tp, TQ)),
        in_specs=[qrow(N_HEADS * HEAD_DIM), qrow(IDX_HEADS * IDX_DIM), qrow(LANES), seq(LANES),
                  seq(N_KV_HEADS * HEAD_DIM),
                  pl.BlockSpec((None, tp // tk, N_KV_HEADS * VT_ROWS, tk), lambda b, j: (b, 0, 0, 0))],
        out_specs=qrow(N_HEADS * HEAD_DIM),
        out_shape=jax.ShapeDtypeStruct((bsz, tp, N_HEADS * HEAD_DIM), BF16),
        scratch_shapes=[pltpu.VMEM((tp, TQ), jnp.int32), pltpu.VMEM((tp, TQ), jnp.int16),
                        pltpu.VMEM((tp, TQ), jnp.int16), pltpu.VMEM((tp, TQ), BF16),
                        pltpu.VMEM((N_KV_HEADS, HEAD_DIM, gw), BF16),
                        pltpu.VMEM((LANES, IDX_HEADS * TQ), BF16),
                        pltpu.VMEM((N_KV_HEADS, VT_ROWS, gw), F32),
                        pltpu.VMEM((N_KV_HEADS, 1, gw), F32),
                        pltpu.VMEM((N_KV_HEADS, tk, gw), BF16), pltpu.VMEM((N_KV_HEADS, tk, gw), BF16),
                        pltpu.VMEM((tk, tk), BF16)],
        compiler_params=pltpu.CompilerParams(dimension_semantics=("arbitrary", "arbitrary"),
                                             vmem_limit_bytes=VMEM_LIMIT),
        name="dsa_attention",
    )(q, qi, wi, ki, k, vt)


def _rglru_kernel(h_ref, g_ref, w_ref, cw_ref, cb_ref, wa_ref, ba_ref, wx_ref, bx_ref, lam_ref,
                  out_ref, xext_ref, a_ref, b_ref, state_ref):
    tm = h_ref.shape[0]
    hist = SUBLANES

    @pl.when(pl.program_id(1) == 0)
    def _():
        xext_ref[0:hist, :] = jnp.zeros((hist, D_MODEL), F32)
        state_ref[...] = jnp.zeros_like(state_ref)

    n = _rms(h_ref[...], g_ref[...]).astype(BF16)
    xext_ref[hist:hist + tm, :] = jnp.dot(n, w_ref[:, 0:D_MODEL], preferred_element_type=F32)
    u = cb_ref[...] + cw_ref[0:1, :] * xext_ref[hist - RNN_CONV + 1:hist - RNN_CONV + 1 + tm, :]
    for kk in range(1, RNN_CONV):
        o = hist - RNN_CONV + 1 + kk
        u = u + cw_ref[kk:kk + 1, :] * xext_ref[o:o + tm, :]
    xext_ref[0:hist, :] = xext_ref[tm:tm + hist, :]

    ub = u.astype(BF16)
    rs, is_ = [], []
    for blk in range(RNN_BLOCKS):
        sl = slice(blk * RNN_BLOCK_DIM, (blk + 1) * RNN_BLOCK_DIM)
        rs.append(jnp.dot(ub[:, sl], wa_ref[blk], preferred_element_type=F32))
        is_.append(jnp.dot(ub[:, sl], wx_ref[blk], preferred_element_type=F32))
    r = jax.nn.sigmoid(jnp.concatenate(rs, axis=1) + ba_ref[...])
    ig = jax.nn.sigmoid(jnp.concatenate(is_, axis=1) + bx_ref[...])
    nl = -lam_ref[...]
    softplus = jnp.maximum(nl, 0.0) + jnp.log1p(jnp.exp(-jnp.abs(nl)))
    log_a = -LRU_C * r * softplus
    a_all = jnp.exp(log_a)
    b_all = jnp.sqrt(-_expm1(2.0 * log_a, a_all * a_all)) * (ig * u)
    nchunk = D_MODEL // LANES
    for c in range(nchunk):
        a_ref[c] = a_all[:, c * LANES:(c + 1) * LANES]
        b_ref[c] = b_all[:, c * LANES:(c + 1) * LANES]

    seg = tm // SCAN_SEGMENTS
    groups = SCAN_SEGMENTS // SUBLANES

    def step(i, carry):
        out = []
        for v in range(groups):
            prod, loc = carry[2 * v], carry[2 * v + 1]
            rows = pl.ds(v * SUBLANES * seg + i, SUBLANES, stride=seg)
            a = a_ref[:, rows, :]
            prod = a * prod
            loc = a * loc + b_ref[:, rows, :]
            a_ref[:, rows, :] = prod
            b_ref[:, rows, :] = loc
            out += [prod, loc]
        return tuple(out)

    init = (jnp.ones((nchunk, SUBLANES, LANES), F32), jnp.zeros((nchunk, SUBLANES, LANES), F32)) * groups
    ends = lax.fori_loop(0, seg, step, init)
    gel = jax.nn.gelu(jnp.dot(n, w_ref[:, D_MODEL:], preferred_element_type=F32))
    hcur = state_ref[...]
    for s in range(SCAN_SEGMENTS):
        rs = slice(s * seg, (s + 1) * seg)
        hs = a_ref[:, rs, :] * hcur + b_ref[:, rs, :]
        for c in range(nchunk):
            cs = slice(c * LANES, (c + 1) * LANES)
            out_ref[rs, cs] = (hs[c] * gel[rs, cs]).astype(BF16)
        v, r = s // SUBLANES, s % SUBLANES
        hcur = ends[2 * v][:, r:r + 1, :] * hcur + ends[2 * v + 1][:, r:r + 1, :]
    state_ref[...] = hcur


def _rglru(h3, g, w, cw, cb, wa, ba, wx, bx, lam, tm):
    bsz, tp, _ = h3.shape
    row = pl.BlockSpec((None, tm, D_MODEL), lambda b, t: (b, t, 0))
    vec = _const_spec((1, D_MODEL))
    blkw = _const_spec((RNN_BLOCKS, RNN_BLOCK_DIM, RNN_BLOCK_DIM))
    return pl.pallas_call(
        _rglru_kernel,
        grid=(bsz, tp // tm),
        in_specs=[row, vec, _const_spec((D_MODEL, 2 * D_MODEL)), _const_spec((RNN_CONV, D_MODEL)), vec,
                  blkw, vec, blkw, vec, vec],
        out_specs=row,
        out_shape=jax.ShapeDtypeStruct((bsz, tp, D_MODEL), BF16),
        scratch_shapes=[pltpu.VMEM((tm + SUBLANES, D_MODEL), F32),
                        pltpu.VMEM((D_MODEL // LANES, tm, LANES), F32),
                        pltpu.VMEM((D_MODEL // LANES, tm, LANES), F32),
                        pltpu.VMEM((D_MODEL // LANES, 1, LANES), F32)],
        compiler_params=pltpu.CompilerParams(dimension_semantics=("arbitrary", "arbitrary"),
                                             vmem_limit_bytes=VMEM_LIMIT),
        name="rglru",
    )(h3, g, w, cw, cb, wa, ba, wx, bx, lam)


SCAN_SEGMENTS = 16
CONV_HIST = 32
CONV_ROWS = 128


def _conformer_kernel(h_ref, g_ref, w_ref, dw_ref, db_ref, lg_ref, lb_ref, out_ref, xs_ref, y_ref):
    tm = h_ref.shape[0]

    @pl.when(pl.program_id(1) == 0)
    def _():
        xs_ref[0, 0:CONV_HIST, :] = jnp.zeros((CONV_HIST, D_MODEL), F32)

    n = _rms(h_ref[...], g_ref[...]).astype(BF16)
    a = jnp.dot(n, w_ref[:, 0:D_MODEL], preferred_element_type=F32)
    gate = jnp.dot(n, w_ref[:, D_MODEL:], preferred_element_type=F32)
    xs_ref[0, CONV_HIST:CONV_HIST + tm, :] = a * jax.nn.sigmoid(gate)
    span = tm + CONV_HIST - SUBLANES
    for s in range(1, SUBLANES):
        xs_ref[s, 0:span, :] = xs_ref[0, s:s + span, :]

    first = CONV_HIST - CONV_KERNEL + 1
    row_chunks = range(0, tm, CONV_ROWS)
    for c0 in range(0, D_MODEL, LANES):
        cs = slice(c0, c0 + LANES)
        accs = [jnp.broadcast_to(db_ref[:, cs], (CONV_ROWS, LANES)) for _ in row_chunks]
        for kk in range(CONV_KERNEL):
            shift = (first + kk) % SUBLANES
            base = first + kk - shift
            wk = dw_ref[kk:kk + 1, cs]
            for ri, r0 in enumerate(row_chunks):
                accs[ri] = accs[ri] + wk * xs_ref[shift, base + r0:base + r0 + CONV_ROWS, cs]
        for ri, r0 in enumerate(row_chunks):
            y_ref[r0:r0 + CONV_ROWS, cs] = accs[ri]
    xs_ref[0, 0:CONV_HIST, :] = xs_ref[0, tm:tm + CONV_HIST, :]

    y = y_ref[...]
    mu = jnp.mean(y, axis=-1, keepdims=True)
    yc = y - mu
    z = yc * lax.rsqrt(jnp.mean(yc * yc, axis=-1, keepdims=True) + NORM_EPS) * lg_ref[...] + lb_ref[...]
    out_ref[...] = jax.nn.silu(z).astype(BF16)


def _conformer(h3, g, w, dw, db, lg, lb, tm):
    bsz, tp, _ = h3.shape
    row = pl.BlockSpec((None, tm, D_MODEL), lambda b, t: (b, t, 0))
    vec = _const_spec((1, D_MODEL))
    return pl.pallas_call(
        _conformer_kernel,
        grid=(bsz, tp // tm),
        in_specs=[row, vec, _const_spec((D_MODEL, 2 * D_MODEL)), _const_spec((CONV_KERNEL, D_MODEL)),
                  vec, vec, vec],
        out_specs=row,
        out_shape=jax.ShapeDtypeStruct((bsz, tp, D_MODEL), BF16),
        scratch_shapes=[pltpu.VMEM((SUBLANES, tm + CONV_HIST, D_MODEL), F32), pltpu.VMEM((tm, D_MODEL), F32)],
        compiler_params=pltpu.CompilerParams(dimension_semantics=("arbitrary", "arbitrary"),
                                             vmem_limit_bytes=VMEM_LIMIT),
        name="conformer",
    )(h3, g, w, dw, db, lg, lb)


def _merge_kernel(h_ref, at_ref, rn_ref, cv_ref, g_ref, wg_ref, wa_ref, wr_ref, wc_ref, wo_ref, out_ref):
    h = h_ref[...]
    n = _rms(h, g_ref[...]).astype(BF16)
    merged = None
    for i, (src, wref) in enumerate(((at_ref, wa_ref), (rn_ref, wr_ref), (cv_ref, wc_ref))):
        gate = jax.nn.sigmoid(jnp.dot(n, wg_ref[:, i * D_MODEL:(i + 1) * D_MODEL],
                                      preferred_element_type=F32))
        term = gate * jnp.dot(src[...], wref[...], preferred_element_type=F32)
        merged = term if merged is None else merged + term
    out_ref[...] = h + jnp.dot(merged.astype(BF16), wo_ref[...], preferred_element_type=F32)


def _merge(h2, attn, rnn, cnv, g, wg, wa, wr, wc, wo, tm):
    rows = h2.shape[0]
    row = pl.BlockSpec((tm, D_MODEL), lambda i: (i, 0))
    sq = _const_spec((D_MODEL, D_MODEL))
    return pl.pallas_call(
        _merge_kernel,
        grid=(rows // tm,),
        in_specs=[row, row, row, row, _const_spec((1, D_MODEL)), _const_spec((D_MODEL, 3 * D_MODEL)),
                  sq, sq, sq, sq],
        out_specs=row,
        out_shape=jax.ShapeDtypeStruct((rows, D_MODEL), F32),
        compiler_params=pltpu.CompilerParams(dimension_semantics=("arbitrary",),
                                             vmem_limit_bytes=VMEM_LIMIT),
        name="merge",
    )(h2, attn, rnn, cnv, g, wg, wa, wr, wc, wo)


def _ffn_kernel(h_ref, g_ref, wg_ref, wu_ref, wd_ref, out_ref):
    h = h_ref[...]
    f = _rms(h, g_ref[...]).astype(BF16)
    gate = jnp.dot(f, wg_ref[...], preferred_element_type=F32)
    up = jnp.dot(f, wu_ref[...], preferred_element_type=F32)
    act = (jax.nn.silu(gate) * up).astype(BF16)
    out_ref[...] = h + jnp.dot(act, wd_ref[...], preferred_element_type=F32)


def _ffn(h2, g, wg, wu, wd, tm):
    rows = h2.shape[0]
    dff = wg.shape[1]
    row = pl.BlockSpec((tm, D_MODEL), lambda i: (i, 0))
    return pl.pallas_call(
        _ffn_kernel,
        grid=(rows // tm,),
        in_specs=[row, _const_spec((1, D_MODEL)), _const_spec((D_MODEL, dff)), _const_spec((D_MODEL, dff)),
                  _const_spec((dff, D_MODEL))],
        out_specs=row,
        out_shape=jax.ShapeDtypeStruct((rows, D_MODEL), F32),
        compiler_params=pltpu.CompilerParams(dimension_semantics=("arbitrary",),
                                             vmem_limit_bytes=VMEM_LIMIT),
        name="ffn",
    )(h2, g, wg, wu, wd)


def _rope_tables(n, dim):
    inv = ROPE_THETA ** (-jnp.arange(0, dim, 2, dtype=F32) / dim)
    ang = jnp.arange(n, dtype=F32)[:, None] * inv[None, :]
    return jnp.cos(ang), jnp.sin(ang)


def kernel(x, meta, mix_norm_g, w_in, q_norm_g, k_norm_g, rnn_conv_w, rnn_conv_b, rnn_wa, rnn_ba, rnn_wx, rnn_bx, rnn_lambda, conv_dw_w, conv_dw_b, conv_ln_g, conv_ln_b, w_o_attn, w_o_rnn, w_o_conv, w_out, ffn_norm_g, w_ffn_gate, w_ffn_up, w_ffn_down):
    bsz, seq, _ = x.shape
    depth = w_in.shape[0]
    t_valid = seq + N_META
    tp = -(-t_valid // LANES) * LANES
    topk = min(TOPK_MAX, seq // 4)
    rows = bsz * tp
    tm_seq = _pick_tile(tp, (384, 256, 128))
    tm_flat = _pick_tile(rows, (512, 384, 256, 128))

    q_w, kv_w, qi_w = N_HEADS * HEAD_DIM, N_KV_HEADS * HEAD_DIM, IDX_HEADS * IDX_DIM
    o = np.cumsum([0, q_w, kv_w, kv_w, qi_w, IDX_DIM, IDX_HEADS, D_MODEL, D_MODEL, 2 * D_MODEL, 3 * D_MODEL])
    pad_w = LANES - IDX_DIM - IDX_HEADS
    w_attn = jnp.concatenate([w_in[:, :, o[0]:o[6]], jnp.zeros((depth, D_MODEL, pad_w), w_in.dtype)],
                             axis=2).astype(BF16)
    w_rnn = w_in[:, :, o[6]:o[8]].astype(BF16)
    w_cnv = w_in[:, :, o[8]:o[9]].astype(BF16)
    w_gate = w_in[:, :, o[9]:o[10]].astype(BF16)
    wa_b, wx_b = rnn_wa.astype(BF16), rnn_wx.astype(BF16)
    w_oa, w_or, w_oc, w_ot = (a.astype(BF16) for a in (w_o_attn, w_o_rnn, w_o_conv, w_out))
    w_fg, w_fu, w_fd = (a.astype(BF16) for a in (w_ffn_gate, w_ffn_up, w_ffn_down))

    cos_a, sin_a = _rope_tables(tp, HEAD_DIM)
    cos_i, sin_i = _rope_tables(tp, IDX_DIM)
    tabs = (jnp.concatenate([cos_a, cos_a], axis=1), jnp.concatenate([-sin_a, sin_a], axis=1),
            jnp.concatenate([cos_i] * 4, axis=1), jnp.concatenate([-sin_i, sin_i] * 2, axis=1))

    h = jnp.concatenate([jnp.broadcast_to(meta[None].astype(x.dtype), (bsz, N_META, D_MODEL)), x,
                         jnp.zeros((bsz, tp - t_valid, D_MODEL), x.dtype)], axis=1)
    vec = lambda a: a.reshape(1, -1)

    for l in range(depth):
        h2 = h.reshape(rows, D_MODEL)
        q, k, vt, qi, ki, wi = _attn_proj(h2, vec(mix_norm_g[l]), w_attn[l], vec(q_norm_g[l]),
                                          vec(k_norm_g[l]), tabs, tm_seq, tp // tm_seq)
        seq3 = lambda a: a.reshape(bsz, tp, a.shape[-1])
        attn = _dsa_attention(seq3(q), seq3(qi), seq3(wi), seq3(ki), seq3(k),
                              vt.reshape(bsz, tp // tm_seq, N_KV_HEADS * VT_ROWS, tm_seq),
                              t_valid, topk, tm_seq)
        attn = attn.reshape(rows, N_HEADS * HEAD_DIM)

        rnn = _rglru(h, vec(mix_norm_g[l]), w_rnn[l], rnn_conv_w[l], vec(rnn_conv_b[l]), wa_b[l],
                     vec(rnn_ba[l]), wx_b[l], vec(rnn_bx[l]), vec(rnn_lambda[l]), tm_seq)
        cnv = _conformer(h, vec(mix_norm_g[l]), w_cnv[l], conv_dw_w[l], vec(conv_dw_b[l]),
                         vec(conv_ln_g[l]), vec(conv_ln_b[l]), tm_seq)
        h2 = _merge(h2, attn, rnn.reshape(rows, D_MODEL), cnv.reshape(rows, D_MODEL), vec(mix_norm_g[l]),
                    w_gate[l], w_oa[l], w_or[l], w_oc[l], w_ot[l], tm_flat)
        h2 = _ffn(h2, vec(ffn_norm_g[l]), w_fg[l], w_fu[l], w_fd[l], tm_flat)
        h = h2.reshape(bsz, tp, D_MODEL)

    return h[:, N_META:t_valid]
```

```python
import functools

import jax
import jax.numpy as jnp
import numpy as np
from jax import lax
from jax.experimental import pallas as pl
from jax.experimental.pallas import tpu as pltpu

D_MODEL = 1024
CHUNK = 64
N_META = 16
N_HEADS = 8
N_KV_HEADS = 2
HEAD_DIM = 128
KV_GROUP = N_HEADS // N_KV_HEADS
IDX_HEADS = 8
IDX_DIM = 64
TOPK_MAX = 256
ROPE_THETA = 10000.0
RNN_BLOCKS = 8
RNN_BLOCK_DIM = D_MODEL // RNN_BLOCKS
RNN_CONV = 4
LRU_C = 8.0
CONV_KERNEL = 31
NORM_EPS = 1e-6

LANES = 128
SUBLANES = 8
PACKED_ROWS = 16
TQ = 256
VMEM_LIMIT = 56 * 1024 * 1024
LOG2E = 1.4426950408889634

HALF_BITS = 16
HALF_MASK = 2 ** HALF_BITS - 1
I16_MIN = -(2 ** (HALF_BITS - 1))
I32_MIN = -(2 ** 31)
KEY_MASKED = -(0x7F800000 + 1)
NEG_BIG = -(2.0 ** 100)
VT_ROWS = HEAD_DIM + PACKED_ROWS

F32 = jnp.float32
BF16 = jnp.bfloat16


def _rms(x, g):
    return x * lax.rsqrt(jnp.mean(x * x, axis=-1, keepdims=True) + NORM_EPS) * g


def _expm1(y, u):
    near = jnp.where(u == 1.0, y, (u - 1.0) * y / jnp.log(jnp.where(u == 1.0, 2.0, u)))
    return jnp.where(y > -0.5, near, u - 1.0)


def _const_spec(shape):
    nd = len(shape)
    return pl.BlockSpec(shape, lambda *_: (0,) * nd, pipeline_mode=pl.Buffered(1))


def _pick_tile(n, candidates):
    for c in candidates:
        if n % c == 0:
            return c
    raise ValueError(f"no tile for {n}")


def _attn_proj_kernel(x_ref, g_ref, w_ref, qg_ref, kg_ref, cosa_ref, sina_ref, cosi_ref, sini_ref,
                      q_ref, k_ref, vt_ref, qi_ref, ki_ref, wi_ref, *, idx_scale):
    n = _rms(x_ref[...], g_ref[...]).astype(BF16)
    cosa, sina = cosa_ref[...], sina_ref[...]
    cosi, sini = cosi_ref[...], sini_ref[...]
    lane = lax.broadcasted_iota(jnp.int32, cosi.shape, 1)
    low_half = (lane % IDX_DIM) < (IDX_DIM // 2)

    def rot_attn(a, g):
        a = _rms(a, g)
        return a * cosa + pltpu.roll(a, HEAD_DIM // 2, axis=1) * sina

    def rot_idx(a):
        partner = jnp.where(low_half, pltpu.roll(a, LANES - IDX_DIM // 2, axis=1),
                            pltpu.roll(a, IDX_DIM // 2, axis=1))
        return a * cosi + partner * sini

    q_w = N_HEADS * HEAD_DIM
    kv_w = N_KV_HEADS * HEAD_DIM
    qi_w = IDX_HEADS * IDX_DIM
    o_k, o_v, o_qi, o_ki = q_w, q_w + kv_w, q_w + 2 * kv_w, q_w + 2 * kv_w + qi_w

    q_scale = (HEAD_DIM ** -0.5) * LOG2E
    pq = jnp.dot(n, w_ref[:, 0:q_w], preferred_element_type=F32)
    for h in range(N_HEADS):
        sl = slice(h * HEAD_DIM, (h + 1) * HEAD_DIM)
        q_ref[:, sl] = (rot_attn(pq[:, sl], qg_ref[...]) * q_scale).astype(BF16)
    pk = jnp.dot(n, w_ref[:, o_k:o_qi], preferred_element_type=F32)
    for h in range(N_KV_HEADS):
        sl = slice(h * HEAD_DIM, (h + 1) * HEAD_DIM)
        k_ref[:, sl] = rot_attn(pk[:, sl], kg_ref[...]).astype(BF16)
        vt_ref[h * VT_ROWS:h * VT_ROWS + HEAD_DIM, :] = (
            pk[:, kv_w + h * HEAD_DIM:kv_w + (h + 1) * HEAD_DIM].T.astype(BF16))
        vt_ref[h * VT_ROWS + HEAD_DIM:(h + 1) * VT_ROWS, :] = jnp.ones((PACKED_ROWS, pk.shape[0]), BF16)
    pi = jnp.dot(n, w_ref[:, o_qi:], preferred_element_type=F32)
    for c in range(qi_w // LANES):
        sl = slice(c * LANES, (c + 1) * LANES)
        qi_ref[:, sl] = rot_idx(pi[:, sl]).astype(BF16)
    last = pi[:, qi_w:]
    ki = rot_idx(last)
    ki_ref[...] = jnp.where(lane < IDX_DIM, ki, pltpu.roll(ki, IDX_DIM, axis=1)).astype(BF16)
    wi_ref[...] = last * idx_scale


def _attn_proj(h2, g, w, qg, kg, tabs, tm, tiles_per_seq):
    rows = h2.shape[0]
    wcols = w.shape[1]
    row = lambda width: pl.BlockSpec((tm, width), lambda i: (i, 0))
    tab = pl.BlockSpec((tm, LANES), lambda i: (i % tiles_per_seq, 0))
    idx_scale = (IDX_HEADS ** -0.5) * (IDX_DIM ** -0.5)
    return pl.pallas_call(
        functools.partial(_attn_proj_kernel, idx_scale=idx_scale),
        grid=(rows // tm,),
        in_specs=[row(D_MODEL), _const_spec((1, D_MODEL)), _const_spec((D_MODEL, wcols)),
                  _const_spec((1, HEAD_DIM)), _const_spec((1, HEAD_DIM)), tab, tab, tab, tab],
        out_specs=[row(N_HEADS * HEAD_DIM), row(N_KV_HEADS * HEAD_DIM),
                   pl.BlockSpec((None, N_KV_HEADS * VT_ROWS, tm), lambda i: (i, 0, 0)),
                   row(IDX_HEADS * IDX_DIM), row(LANES), row(LANES)],
        out_shape=[jax.ShapeDtypeStruct((rows, N_HEADS * HEAD_DIM), BF16),
                   jax.ShapeDtypeStruct((rows, N_KV_HEADS * HEAD_DIM), BF16),
                   jax.ShapeDtypeStruct((rows // tm, N_KV_HEADS * VT_ROWS, tm), BF16),
                   jax.ShapeDtypeStruct((rows, IDX_HEADS * IDX_DIM), BF16),
                   jax.ShapeDtypeStruct((rows, LANES), BF16),
                   jax.ShapeDtypeStruct((rows, LANES), F32)],
        compiler_params=pltpu.CompilerParams(dimension_semantics=("arbitrary",),
                                             vmem_limit_bytes=VMEM_LIMIT),
        name="attn_proj",
    )(h2, g, w, qg, kg, *tabs)


def _dsa_kernel(q_ref, qi_ref, wi_ref, ki_ref, k_ref, vt_ref, out_ref,
                key_ref, hi_ref, lo_ref, bias_ref, qt_ref, qit_ref, acc_ref, m_ref, sa_ref, sb_ref, tri_ref,
                *, n_valid, topk, tk):
    tq = TQ
    gw = KV_GROUP * tq
    t0 = pl.program_id(1) * tq
    tcol = t0 + lax.broadcasted_iota(jnp.int32, (1, tq), 1)
    kb = jnp.where(tcol < N_META, N_META,
                   N_META + CHUNK * (1 + jnp.right_shift(tcol - N_META, CHUNK.bit_length() - 1)))
    kb = jnp.where(tcol < n_valid, jnp.minimum(kb, n_valid), n_valid)
    kext = jnp.minimum(n_valid, N_META + CHUNK * (1 + (t0 + tq - 1 - N_META) // CHUNK))
    n_tiles = (kext + tk - 1) // tk
    kb_first = jnp.where(t0 < N_META, N_META, N_META + CHUNK * (1 + (t0 - N_META) // CHUNK))

    @pl.when((pl.program_id(0) == 0) & (pl.program_id(1) == 0))
    def _():
        tri_ref[...] = jnp.where(lax.broadcasted_iota(jnp.int32, (tk, tk), 0)
                                 <= lax.broadcasted_iota(jnp.int32, (tk, tk), 1), 1.0, 0.0).astype(BF16)

    def scores(t, s_ref):
        r0 = pl.multiple_of(t * tk, tk)
        for n in range(N_KV_HEADS):
            kt = k_ref[pl.ds(r0, tk), n * HEAD_DIM:(n + 1) * HEAD_DIM]
            s_ref[n] = jnp.dot(kt, qt_ref[n], preferred_element_type=F32).astype(BF16)

    for h in range(N_HEADS):
        qh = q_ref[:, h * HEAD_DIM:(h + 1) * HEAD_DIM].astype(F32).T
        qt_ref[h // KV_GROUP, :, (h % KV_GROUP) * tq:(h % KV_GROUP + 1) * tq] = qh.astype(BF16)
    scores(0, sa_ref)
    row = lax.broadcasted_iota(jnp.int32, (LANES, tq), 0)
    for hp in range(IDX_HEADS // 2):
        pair = qi_ref[:, hp * LANES:(hp + 1) * LANES].astype(F32).T
        qit_ref[:, (2 * hp) * tq:(2 * hp + 1) * tq] = jnp.where(row < IDX_DIM, pair, 0.0).astype(BF16)
        qit_ref[:, (2 * hp + 1) * tq:(2 * hp + 2) * tq] = jnp.where(row >= IDX_DIM, pair, 0.0).astype(BF16)
    w = wi_ref[...].T[IDX_DIM:IDX_DIM + IDX_HEADS, :]

    def score_tile(i, carry, masked):
        r0 = pl.multiple_of(i * tk, tk)
        kt = ki_ref[pl.ds(r0, tk), :]
        acc = jnp.zeros((tk, tq), F32)
        for hp in range(IDX_HEADS // 2):
            lg = jnp.dot(kt, qit_ref[:, hp * 2 * tq:(hp + 1) * 2 * tq],
                         preferred_element_type=F32)
            for hh in range(2):
                h = 2 * hp + hh
                acc = acc + jnp.maximum(lg[:, hh * tq:(hh + 1) * tq], 0.0) * w[h:h + 1, :]
        bits = pltpu.bitcast(acc, jnp.int32)
        key = jnp.where(bits >= 0, bits, I32_MIN - bits)
        if masked:
            pos = r0 + lax.broadcasted_iota(jnp.int32, (tk, tq), 0)
            key = jnp.where(pos < kb, key, KEY_MASKED)
        key_ref[pl.ds(r0, tk), :] = key
        hi_ref[pl.ds(r0, tk), :] = lax.shift_right_arithmetic(key, HALF_BITS).astype(jnp.int16)
        return carry

    n_open = jnp.minimum(n_valid, kb_first) // tk
    lax.fori_loop(0, n_open, functools.partial(score_tile, masked=False), 0)
    lax.fori_loop(n_open, n_tiles, functools.partial(score_tile, masked=True), 0)

    def search16(ref, kth):
        def count_ge(cand):
            c16 = cand.astype(jnp.int16)

            def body(i, acc):
                r0 = pl.multiple_of(i * tk, tk)
                ind = jnp.where(ref[pl.ds(r0, tk), :] >= c16, jnp.bfloat16(1), jnp.bfloat16(0))
                parts = [ind[r * PACKED_ROWS:(r + 1) * PACKED_ROWS] for r in range(tk // PACKED_ROWS)]
                while len(parts) > 1:
                    nxt = [parts[r] + parts[r + 1] for r in range(0, len(parts) - 1, 2)]
                    parts = nxt + ([parts[-1]] if len(parts) % 2 else [])
                return acc + parts[0].astype(F32)

            acc = lax.fori_loop(0, n_tiles, body, jnp.zeros((PACKED_ROWS, tq), F32))
            return acc.sum(axis=0, keepdims=True)

        def step(it, carry):
            lo, c_lo = carry
            cand = lo + jnp.left_shift(jnp.int32(1), HALF_BITS - 1 - it)
            c = count_ge(cand)
            return jnp.where(c >= kth, cand, lo), jnp.where(c >= kth, c, c_lo)

        start = (jnp.full((1, tq), I16_MIN, jnp.int32), jnp.full((1, tq), n_tiles * tk, jnp.int32).astype(F32))
        return lax.fori_loop(0, HALF_BITS, step, start)

    tau_hi, c_high = search16(hi_ref, jnp.full((1, tq), float(topk), F32))

    def low_tile(i, acc):
        r0 = pl.multiple_of(i * tk, tk)
        blk = key_ref[pl.ds(r0, tk), :]
        hi = lax.shift_right_arithmetic(blk, HALF_BITS)
        low = jnp.where(hi == tau_hi, (blk & HALF_MASK) + I16_MIN, I16_MIN)
        lo_ref[pl.ds(r0, tk), :] = low.astype(jnp.int16)
        above = jnp.where(hi > tau_hi, 1, 0).astype(jnp.int32)
        return acc + above.reshape(tk // SUBLANES, SUBLANES, tq).sum(axis=0)

    c_above = lax.fori_loop(0, n_tiles, low_tile, jnp.zeros((SUBLANES, tq), jnp.int32))
    c_above = c_above.sum(axis=0, keepdims=True)
    c_above = c_above.astype(F32)
    tau_lo, c_low = search16(lo_ref, topk - c_above)
    tau = tau_hi * (HALF_MASK + 1) + (tau_lo - I16_MIN)
    c_ge = jnp.where(tau_lo == I16_MIN, c_high, c_above + c_low)
    excess = c_ge - topk
    cut = jnp.max(excess) > 0

    @pl.when(jnp.logical_not(cut))
    def _():
        def bias_tile(i, carry, masked):
            r0 = pl.multiple_of(i * tk, tk)
            keep = key_ref[pl.ds(r0, tk), :] >= tau
            if masked:
                keep = keep & (r0 + lax.broadcasted_iota(jnp.int32, (tk, tq), 0) < kb)
            bias_ref[pl.ds(r0, tk), :] = jnp.where(keep, 0.0, NEG_BIG).astype(BF16)
            return carry

        lax.fori_loop(0, n_open, functools.partial(bias_tile, masked=False), 0)
        lax.fori_loop(n_open, n_tiles, functools.partial(bias_tile, masked=True), 0)

    @pl.when(cut)
    def _():
        def tile_bias(t, later):
            r0 = pl.multiple_of(t * tk, tk)
            blk = key_ref[pl.ds(r0, tk), :]
            pos = r0 + lax.broadcasted_iota(jnp.int32, (tk, tq), 0)
            eq = blk == tau
            within = jnp.dot(tri_ref[...], jnp.where(eq, 1.0, 0.0).astype(BF16), preferred_element_type=F32)
            from_here = later + within
            keep = ((blk > tau) | (eq & (from_here > excess))) & (pos < kb)
            return r0, jnp.where(keep, 0.0, NEG_BIG).astype(BF16), from_here[0:1, :]

        def bias_pair(i, later):
            t = n_tiles - 1 - 2 * i
            ra, bias_a, later_a = tile_bias(t, later)
            rb, bias_b, later_b = tile_bias(jnp.maximum(t - 1, 0), later_a)
            bias_ref[pl.ds(rb, tk), :] = bias_b
            bias_ref[pl.ds(ra, tk), :] = bias_a
            return later_b

        lax.fori_loop(0, (n_tiles + 1) // 2, bias_pair, jnp.zeros((1, tq), F32))

    acc_ref[...] = jnp.zeros_like(acc_ref)
    m_ref[...] = jnp.full(m_ref.shape, NEG_BIG, F32)

    def softmax_pv(t, s_ref):
        r0 = pl.multiple_of(t * tk, tk)
        bt = bias_ref[pl.ds(r0, tk), :]
        bt = jnp.concatenate([bt] * KV_GROUP, axis=1)
        for n in range(N_KV_HEADS):
            s = s_ref[n] + bt
            m = m_ref[n]
            m_new = jnp.maximum(m, jnp.max(s, axis=0, keepdims=True).astype(F32))
            alpha = jnp.exp2(m - m_new)
            p = jnp.exp2(s - m_new.astype(BF16))
            m_ref[n] = m_new
            vt = vt_ref[t, n * VT_ROWS:(n + 1) * VT_ROWS, :]
            acc_ref[n] = alpha * acc_ref[n] + jnp.dot(vt, p, preferred_element_type=F32)

    last_tile = k_ref.shape[0] // tk - 1

    def tile_pair(i, carry):
        t = 2 * i
        scores(t + 1, sb_ref)
        softmax_pv(t, sa_ref)
        scores(jnp.minimum(t + 2, last_tile), sa_ref)
        softmax_pv(t + 1, sb_ref)
        return carry

    lax.fori_loop(0, n_tiles // 2, tile_pair, 0)

    @pl.when(n_tiles % 2 == 1)
    def _():
        softmax_pv(n_tiles - 1, sa_ref)

    for h in range(N_HEADS):
        n, g = h // KV_GROUP, h % KV_GROUP
        inv_l = 1.0 / acc_ref[n, HEAD_DIM:HEAD_DIM + 1, g * tq:(g + 1) * tq]
        oh = acc_ref[n, 0:HEAD_DIM, g * tq:(g + 1) * tq] * inv_l
        out_ref[:, h * HEAD_DIM:(h + 1) * HEAD_DIM] = oh.T.astype(BF16)


def _dsa_attention(q, qi, wi, ki, k, vt, n_valid, topk, tk):
    bsz, tp, _ = q.shape
    gw = KV_GROUP * TQ
    qrow = lambda width: pl.BlockSpec((None, TQ, width), lambda b, j: (b, j, 0))
    seq = lambda width: pl.BlockSpec((None, tp, width), lambda b, j: (b, 0, 0))
    return pl.pallas_call(
        functools.partial(_dsa_kernel, n_valid=n_valid, topk=topk, tk=tk),
        grid=(bsz, pl.cdiv(tp, TQ)),
        in_specs=[qrow(N_HEADS * HEAD_DIM), qrow(IDX_HEADS * IDX_DIM), qrow(LANES), seq(LANES),
                  seq(N_KV_HEADS * HEAD_DIM),
                  pl.BlockSpec((None, tp // tk, N_KV_HEADS * VT_ROWS, tk), lambda b, j: (b, 0, 0, 0))],
        out_specs=qrow(N_HEADS * HEAD_DIM),
        out_shape=jax.ShapeDtypeStruct((bsz, tp, N_HEADS * HEAD_DIM), BF16),
        scratch_shapes=[pltpu.VMEM((tp, TQ), jnp.int32), pltpu.VMEM((tp, TQ), jnp.int16),
                        pltpu.VMEM((tp, TQ), jnp.int16), pltpu.VMEM((tp, TQ), BF16),
                        pltpu.VMEM((N_KV_HEADS, HEAD_DIM, gw), BF16),
                        pltpu.VMEM((LANES, IDX_HEADS * TQ), BF16),
                        pltpu.VMEM((N_KV_HEADS, VT_ROWS, gw), F32),
                        pltpu.VMEM((N_KV_HEADS, 1, gw), F32),
                        pltpu.VMEM((N_KV_HEADS, tk, gw), BF16), pltpu.VMEM((N_KV_HEADS, tk, gw), BF16),
                        pltpu.VMEM((tk, tk), BF16)],
        compiler_params=pltpu.CompilerParams(dimension_semantics=("arbitrary", "arbitrary"),
                                             vmem_limit_bytes=VMEM_LIMIT),
        name="dsa_attention",
    )(q, qi, wi, ki, k, vt)


def _rglru_kernel(h_ref, g_ref, w_ref, cw_ref, cb_ref, wa_ref, ba_ref, wx_ref, bx_ref, lam_ref,
                  out_ref, xext_ref, a_ref, b_ref, state_ref):
    tm = h_ref.shape[0]
    hist = SUBLANES

    @pl.when(pl.program_id(1) == 0)
    def _():
        xext_ref[0:hist, :] = jnp.zeros((hist, D_MODEL), F32)
        state_ref[...] = jnp.zeros_like(state_ref)

    n = _rms(h_ref[...], g_ref[...]).astype(BF16)
    xext_ref[hist:hist + tm, :] = jnp.dot(n, w_ref[:, 0:D_MODEL], preferred_element_type=F32)
    u = cb_ref[...] + cw_ref[0:1, :] * xext_ref[hist - RNN_CONV + 1:hist - RNN_CONV + 1 + tm, :]
    for kk in range(1, RNN_CONV):
        o = hist - RNN_CONV + 1 + kk
        u = u + cw_ref[kk:kk + 1, :] * xext_ref[o:o + tm, :]
    xext_ref[0:hist, :] = xext_ref[tm:tm + hist, :]

    ub = u.astype(BF16)
    rs, is_ = [], []
    for blk in range(RNN_BLOCKS):
        sl = slice(blk * RNN_BLOCK_DIM, (blk + 1) * RNN_BLOCK_DIM)
        rs.append(jnp.dot(ub[:, sl], wa_ref[blk], preferred_element_type=F32))
        is_.append(jnp.dot(ub[:, sl], wx_ref[blk], preferred_element_type=F32))
    r = jax.nn.sigmoid(jnp.concatenate(rs, axis=1) + ba_ref[...])
    ig = jax.nn.sigmoid(jnp.concatenate(is_, axis=1) + bx_ref[...])
    nl = -lam_ref[...]
    softplus = jnp.maximum(nl, 0.0) + jnp.log1p(jnp.exp(-jnp.abs(nl)))
    log_a = -LRU_C * r * softplus
    a_all = jnp.exp(log_a)
    b_all = jnp.sqrt(-_expm1(2.0 * log_a, a_all * a_all)) * (ig * u)
    nchunk = D_MODEL // LANES
    for c in range(nchunk):
        a_ref[c] = a_all[:, c * LANES:(c + 1) * LANES]
        b_ref[c] = b_all[:, c * LANES:(c + 1) * LANES]

    seg = tm // SCAN_SEGMENTS
    groups = SCAN_SEGMENTS // SUBLANES

    def step(i, carry):
        out = []
        for v in range(groups):
            prod, loc = carry[2 * v], carry[2 * v + 1]
            rows = pl.ds(v * SUBLANES * seg + i, SUBLANES, stride=seg)
            a = a_ref[:, rows, :]
            prod = a * prod
            loc = a * loc + b_ref[:, rows, :]
            a_ref[:, rows, :] = prod
            b_ref[:, rows, :] = loc
            out += [prod, loc]
        return tuple(out)

    init = (jnp.ones((nchunk, SUBLANES, LANES), F32), jnp.zeros((nchunk, SUBLANES, LANES), F32)) * groups
    ends = lax.fori_loop(0, seg, step, init)
    gel = jax.nn.gelu(jnp.dot(n, w_ref[:, D_MODEL:], preferred_element_type=F32))
    hcur = state_ref[...]
    for s in range(SCAN_SEGMENTS):
        rs = slice(s * seg, (s + 1) * seg)
        hs = a_ref[:, rs, :] * hcur + b_ref[:, rs, :]
        for c in range(nchunk):
            cs = slice(c * LANES, (c + 1) * LANES)
            out_ref[rs, cs] = (hs[c] * gel[rs, cs]).astype(BF16)
        v, r = s // SUBLANES, s % SUBLANES
        hcur = ends[2 * v][:, r:r + 1, :] * hcur + ends[2 * v + 1][:, r:r + 1, :]
    state_ref[...] = hcur


def _rglru(h3, g, w, cw, cb, wa, ba, wx, bx, lam, tm):
    bsz, tp, _ = h3.shape
    row = pl.BlockSpec((None, tm, D_MODEL), lambda b, t: (b, t, 0))
    vec = _const_spec((1, D_MODEL))
    blkw = _const_spec((RNN_BLOCKS, RNN_BLOCK_DIM, RNN_BLOCK_DIM))
    return pl.pallas_call(
        _rglru_kernel,
        grid=(bsz, tp // tm),
        in_specs=[row, vec, _const_spec((D_MODEL, 2 * D_MODEL)), _const_spec((RNN_CONV, D_MODEL)), vec,
                  blkw, vec, blkw, vec, vec],
        out_specs=row,
        out_shape=jax.ShapeDtypeStruct((bsz, tp, D_MODEL), BF16),
        scratch_shapes=[pltpu.VMEM((tm + SUBLANES, D_MODEL), F32),
                        pltpu.VMEM((D_MODEL // LANES, tm, LANES), F32),
                        pltpu.VMEM((D_MODEL // LANES, tm, LANES), F32),
                        pltpu.VMEM((D_MODEL // LANES, 1, LANES), F32)],
        compiler_params=pltpu.CompilerParams(dimension_semantics=("arbitrary", "arbitrary"),
                                             vmem_limit_bytes=VMEM_LIMIT),
        name="rglru",
    )(h3, g, w, cw, cb, wa, ba, wx, bx, lam)


SCAN_SEGMENTS = 16
CONV_HIST = 32
CONV_ROWS = 128


def _conformer_kernel(h_ref, g_ref, w_ref, dw_ref, db_ref, lg_ref, lb_ref, out_ref, xs_ref, y_ref):
    tm = h_ref.shape[0]

    @pl.when(pl.program_id(1) == 0)
    def _():
        xs_ref[0, 0:CONV_HIST, :] = jnp.zeros((CONV_HIST, D_MODEL), F32)

    n = _rms(h_ref[...], g_ref[...]).astype(BF16)
    a = jnp.dot(n, w_ref[:, 0:D_MODEL], preferred_element_type=F32)
    gate = jnp.dot(n, w_ref[:, D_MODEL:], preferred_element_type=F32)
    xs_ref[0, CONV_HIST:CONV_HIST + tm, :] = a * jax.nn.sigmoid(gate)
    span = tm + CONV_HIST - SUBLANES
    for s in range(1, SUBLANES):
        xs_ref[s, 0:span, :] = xs_ref[0, s:s + span, :]

    first = CONV_HIST - CONV_KERNEL + 1
    row_chunks = range(0, tm, CONV_ROWS)
    for c0 in range(0, D_MODEL, LANES):
        cs = slice(c0, c0 + LANES)
        accs = [jnp.broadcast_to(db_ref[:, cs], (CONV_ROWS, LANES)) for _ in row_chunks]
        for kk in range(CONV_KERNEL):
            shift = (first + kk) % SUBLANES
            base = first + kk - shift
            wk = dw_ref[kk:kk + 1, cs]
            for ri, r0 in enumerate(row_chunks):
                accs[ri] = accs[ri] + wk * xs_ref[shift, base + r0:base + r0 + CONV_ROWS, cs]
        for ri, r0 in enumerate(row_chunks):
            y_ref[r0:r0 + CONV_ROWS, cs] = accs[ri]
    xs_ref[0, 0:CONV_HIST, :] = xs_ref[0, tm:tm + CONV_HIST, :]

    y = y_ref[...]
    mu = jnp.mean(y, axis=-1, keepdims=True)
    yc = y - mu
    z = yc * lax.rsqrt(jnp.mean(yc * yc, axis=-1, keepdims=True) + NORM_EPS) * lg_ref[...] + lb_ref[...]
    out_ref[...] = jax.nn.silu(z).astype(BF16)


def _conformer(h3, g, w, dw, db, lg, lb, tm):
    bsz, tp, _ = h3.shape
    row = pl.BlockSpec((None, tm, D_MODEL), lambda b, t: (b, t, 0))
    vec = _const_spec((1, D_MODEL))
    return pl.pallas_call(
        _conformer_kernel,
        grid=(bsz, tp // tm),
        in_specs=[row, vec, _const_spec((D_MODEL, 2 * D_MODEL)), _const_spec((CONV_KERNEL, D_MODEL)),
                  vec, vec, vec],
        out_specs=row,
        out_shape=jax.ShapeDtypeStruct((bsz, tp, D_MODEL), BF16),
        scratch_shapes=[pltpu.VMEM((SUBLANES, tm + CONV_HIST, D_MODEL), F32), pltpu.VMEM((tm, D_MODEL), F32)],
        compiler_params=pltpu.CompilerParams(dimension_semantics=("arbitrary", "arbitrary"),
                                             vmem_limit_bytes=VMEM_LIMIT),
        name="conformer",
    )(h3, g, w, dw, db, lg, lb)


def _merge_kernel(h_ref, at_ref, rn_ref, cv_ref, g_ref, wg_ref, wa_ref, wr_ref, wc_ref, wo_ref, out_ref):
    h = h_ref[...]
    n = _rms(h, g_ref[...]).astype(BF16)
    merged = None
    for i, (src, wref) in enumerate(((at_ref, wa_ref), (rn_ref, wr_ref), (cv_ref, wc_ref))):
        gate = jax.nn.sigmoid(jnp.dot(n, wg_ref[:, i * D_MODEL:(i + 1) * D_MODEL],
                                      preferred_element_type=F32))
        term = gate * jnp.dot(src[...], wref[...], preferred_element_type=F32)
        merged = term if merged is None else merged + term
    out_ref[...] = h + jnp.dot(merged.astype(BF16), wo_ref[...], preferred_element_type=F32)


def _merge(h2, attn, rnn, cnv, g, wg, wa, wr, wc, wo, tm):
    rows = h2.shape[0]
    row = pl.BlockSpec((tm, D_MODEL), lambda i: (i, 0))
    sq = _const_spec((D_MODEL, D_MODEL))
    return pl.pallas_call(
        _merge_kernel,
        grid=(rows // tm,),
        in_specs=[row, row, row, row, _const_spec((1, D_MODEL)), _const_spec((D_MODEL, 3 * D_MODEL)),
                  sq, sq, sq, sq],
        out_specs=row,
        out_shape=jax.ShapeDtypeStruct((rows, D_MODEL), F32),
        compiler_params=pltpu.CompilerParams(dimension_semantics=("arbitrary",),
                                             vmem_limit_bytes=VMEM_LIMIT),
        name="merge",
    )(h2, attn, rnn, cnv, g, wg, wa, wr, wc, wo)


def _ffn_kernel(h_ref, g_ref, wg_ref, wu_ref, wd_ref, out_ref):
    h = h_ref[...]
    f = _rms(h, g_ref[...]).astype(BF16)
    gate = jnp.dot(f, wg_ref[...], preferred_element_type=F32)
    up = jnp.dot(f, wu_ref[...], preferred_element_type=F32)
    act = (jax.nn.silu(gate) * up).astype(BF16)
    out_ref[...] = h + jnp.dot(act, wd_ref[...], preferred_element_type=F32)


def _ffn(h2, g, wg, wu, wd, tm):
    rows = h2.shape[0]
    dff = wg.shape[1]
    row = pl.BlockSpec((tm, D_MODEL), lambda i: (i, 0))
    return pl.pallas_call(
        _ffn_kernel,
        grid=(rows // tm,),
        in_specs=[row, _const_spec((1, D_MODEL)), _const_spec((D_MODEL, dff)), _const_spec((D_MODEL, dff)),
                  _const_spec((dff, D_MODEL))],
        out_specs=row,
        out_shape=jax.ShapeDtypeStruct((rows, D_MODEL), F32),
        compiler_params=pltpu.CompilerParams(dimension_semantics=("arbitrary",),
                                             vmem_limit_bytes=VMEM_LIMIT),
        name="ffn",
    )(h2, g, wg, wu, wd)


def _rope_tables(n, dim):
    inv = ROPE_THETA ** (-jnp.arange(0, dim, 2, dtype=F32) / dim)
    ang = jnp.arange(n, dtype=F32)[:, None] * inv[None, :]
    return jnp.cos(ang), jnp.sin(ang)


def kernel(x, meta, mix_norm_g, w_in, q_norm_g, k_norm_g, rnn_conv_w, rnn_conv_b, rnn_wa, rnn_ba, rnn_wx, rnn_bx, rnn_lambda, conv_dw_w, conv_dw_b, conv_ln_g, conv_ln_b, w_o_attn, w_o_rnn, w_o_conv, w_out, ffn_norm_g, w_ffn_gate, w_ffn_up, w_ffn_down):
    bsz, seq, _ = x.shape
    depth = w_in.shape[0]
    t_valid = seq + N_META
    tp = -(-t_valid // LANES) * LANES
    topk = min(TOPK_MAX, seq // 4)
    rows = bsz * tp
    tm_seq = _pick_tile(tp, (384, 256, 128))
    tm_flat = _pick_tile(rows, (512, 384, 256, 128))

    q_w, kv_w, qi_w = N_HEADS * HEAD_DIM, N_KV_HEADS * HEAD_DIM, IDX_HEADS * IDX_DIM
    o = np.cumsum([0, q_w, kv_w, kv_w, qi_w, IDX_DIM, IDX_HEADS, D_MODEL, D_MODEL, 2 * D_MODEL, 3 * D_MODEL])
    pad_w = LANES - IDX_DIM - IDX_HEADS
    w_attn = jnp.concatenate([w_in[:, :, o[0]:o[6]], jnp.zeros((depth, D_MODEL, pad_w), w_in.dtype)],
                             axis=2).astype(BF16)
    w_rnn = w_in[:, :, o[6]:o[8]].astype(BF16)
    w_cnv = w_in[:, :, o[8]:o[9]].astype(BF16)
    w_gate = w_in[:, :, o[9]:o[10]].astype(BF16)
    wa_b, wx_b = rnn_wa.astype(BF16), rnn_wx.astype(BF16)
    w_oa, w_or, w_oc, w_ot = (a.astype(BF16) for a in (w_o_attn, w_o_rnn, w_o_conv, w_out))
    w_fg, w_fu, w_fd = (a.astype(BF16) for a in (w_ffn_gate, w_ffn_up, w_ffn_down))

    cos_a, sin_a = _rope_tables(tp, HEAD_DIM)
    cos_i, sin_i = _rope_tables(tp, IDX_DIM)
    tabs = (jnp.concatenate([cos_a, cos_a], axis=1), jnp.concatenate([-sin_a, sin_a], axis=1),
            jnp.concatenate([cos_i] * 4, axis=1), jnp.concatenate([-sin_i, sin_i] * 2, axis=1))

    h = jnp.concatenate([jnp.broadcast_to(meta[None].astype(x.dtype), (bsz, N_META, D_MODEL)), x,
                         jnp.zeros((bsz, tp - t_valid, D_MODEL), x.dtype)], axis=1)
    vec = lambda a: a.reshape(1, -1)

    for l in range(depth):
        h2 = h.reshape(rows, D_MODEL)
        q, k, vt, qi, ki, wi = _attn_proj(h2, vec(mix_norm_g[l]), w_attn[l], vec(q_norm_g[l]),
                                          vec(k_norm_g[l]), tabs, tm_seq, tp // tm_seq)
        seq3 = lambda a: a.reshape(bsz, tp, a.shape[-1])
        attn = _dsa_attention(seq3(q), seq3(qi), seq3(wi), seq3(ki), seq3(k),
                              vt.reshape(bsz, tp // tm_seq, N_KV_HEADS * VT_ROWS, tm_seq),
                              t_valid, topk, tm_seq)
        attn = attn.reshape(rows, N_HEADS * HEAD_DIM)

        rnn = _rglru(h, vec(mix_norm_g[l]), w_rnn[l], rnn_conv_w[l], vec(rnn_conv_b[l]), wa_b[l],
                     vec(rnn_ba[l]), wx_b[l], vec(rnn_bx[l]), vec(rnn_lambda[l]), tm_seq)
        cnv = _conformer(h, vec(mix_norm_g[l]), w_cnv[l], conv_dw_w[l], vec(conv_dw_b[l]),
                         vec(conv_ln_g[l]), vec(conv_ln_b[l]), tm_seq)
        h2 = _merge(h2, attn, rnn.reshape(rows, D_MODEL), cnv.reshape(rows, D_MODEL), vec(mix_norm_g[l]),
                    w_gate[l], w_oa[l], w_or[l], w_oc[l], w_ot[l], tm_flat)
        h2 = _ffn(h2, vec(ffn_norm_g[l]), w_fg[l], w_fu[l], w_fd[l], tm_flat)
        h = h2.reshape(bsz, tp, D_MODEL)

    return h[:, N_META:t_valid]
```

```python
import functools

import jax
import jax.numpy as jnp
import numpy as np
from jax import lax
from jax.experimental import pallas as pl
from jax.experimental.pallas import tpu as pltpu

D_MODEL = 1024
CHUNK = 64
N_META = 16
N_HEADS = 8
N_KV_HEADS = 2
HEAD_DIM = 128
KV_GROUP = N_HEADS // N_KV_HEADS
IDX_HEADS = 8
IDX_DIM = 64
TOPK_MAX = 256
ROPE_THETA = 10000.0
RNN_BLOCKS = 8
RNN_BLOCK_DIM = D_MODEL // RNN_BLOCKS
RNN_CONV = 4
LRU_C = 8.0
CONV_KERNEL = 31
NORM_EPS = 1e-6

LANES = 128
SUBLANES = 8
PACKED_ROWS = 16
TQ = 256
VMEM_LIMIT = 56 * 1024 * 1024
LOG2E = 1.4426950408889634

HALF_BITS = 16
HALF_MASK = 2 ** HALF_BITS - 1
I16_MIN = -(2 ** (HALF_BITS - 1))
I32_MIN = -(2 ** 31)
KEY_MASKED = -(0x7F800000 + 1)
NEG_BIG = -(2.0 ** 100)
VT_ROWS = HEAD_DIM + PACKED_ROWS

F32 = jnp.float32
BF16 = jnp.bfloat16


def _rms(x, g):
    return x * lax.rsqrt(jnp.mean(x * x, axis=-1, keepdims=True) + NORM_EPS) * g


def _expm1(y, u):
    near = jnp.where(u == 1.0, y, (u - 1.0) * y / jnp.log(jnp.where(u == 1.0, 2.0, u)))
    return jnp.where(y > -0.5, near, u - 1.0)


def _const_spec(shape):
    nd = len(shape)
    return pl.BlockSpec(shape, lambda *_: (0,) * nd, pipeline_mode=pl.Buffered(1))


def _pick_tile(n, candidates):
    for c in candidates:
        if n % c == 0:
            return c
    raise ValueError(f"no tile for {n}")


def _attn_proj_kernel(x_ref, g_ref, w_ref, qg_ref, kg_ref, cosa_ref, sina_ref, cosi_ref, sini_ref,
                      q_ref, k_ref, vt_ref, qi_ref, ki_ref, wi_ref, *, idx_scale):
    n = _rms(x_ref[...], g_ref[...]).astype(BF16)
    cosa, sina = cosa_ref[...], sina_ref[...]
    cosi, sini = cosi_ref[...], sini_ref[...]
    lane = lax.broadcasted_iota(jnp.int32, cosi.shape, 1)
    low_half = (lane % IDX_DIM) < (IDX_DIM // 2)

    def rot_attn(a, g):
        a = _rms(a, g)
        return a * cosa + pltpu.roll(a, HEAD_DIM // 2, axis=1) * sina

    def rot_idx(a):
        partner = jnp.where(low_half, pltpu.roll(a, LANES - IDX_DIM // 2, axis=1),
                            pltpu.roll(a, IDX_DIM // 2, axis=1))
        return a * cosi + partner * sini

    q_w = N_HEADS * HEAD_DIM
    kv_w = N_KV_HEADS * HEAD_DIM
    qi_w = IDX_HEADS * IDX_DIM
    o_k, o_v, o_qi, o_ki = q_w, q_w + kv_w, q_w + 2 * kv_w, q_w + 2 * kv_w + qi_w

    q_scale = (HEAD_DIM ** -0.5) * LOG2E
    pq = jnp.dot(n, w_ref[:, 0:q_w], preferred_element_type=F32)
    for h in range(N_HEADS):
        sl = slice(h * HEAD_DIM, (h + 1) * HEAD_DIM)
        q_ref[:, sl] = (rot_attn(pq[:, sl], qg_ref[...]) * q_scale).astype(BF16)
    pk = jnp.dot(n, w_ref[:, o_k:o_qi], preferred_element_type=F32)
    for h in range(N_KV_HEADS):
        sl = slice(h * HEAD_DIM, (h + 1) * HEAD_DIM)
        k_ref[:, sl] = rot_attn(pk[:, sl], kg_ref[...]).astype(BF16)
        vt_ref[h * VT_ROWS:h * VT_ROWS + HEAD_DIM, :] = (
            pk[:, kv_w + h * HEAD_DIM:kv_w + (h + 1) * HEAD_DIM].T.astype(BF16))
        vt_ref[h * VT_ROWS + HEAD_DIM:(h + 1) * VT_ROWS, :] = jnp.ones((PACKED_ROWS, pk.shape[0]), BF16)
    pi = jnp.dot(n, w_ref[:, o_qi:], preferred_element_type=F32)
    for c in range(qi_w // LANES):
        sl = slice(c * LANES, (c + 1) * LANES)
        qi_ref[:, sl] = rot_idx(pi[:, sl]).astype(BF16)
    last = pi[:, qi_w:]
    ki = rot_idx(last)
    ki_ref[...] = jnp.where(lane < IDX_DIM, ki, pltpu.roll(ki, IDX_DIM, axis=1)).astype(BF16)
    wi_ref[...] = last * idx_scale


def _attn_proj(h2, g, w, qg, kg, tabs, tm, tiles_per_seq):
    rows = h2.shape[0]
    wcols = w.shape[1]
    row = lambda width: pl.BlockSpec((tm, width), lambda i: (i, 0))
    tab = pl.BlockSpec((tm, LANES), lambda i: (i % tiles_per_seq, 0))
    idx_scale = (IDX_HEADS ** -0.5) * (IDX_DIM ** -0.5)
    return pl.pallas_call(
        functools.partial(_attn_proj_kernel, idx_scale=idx_scale),
        grid=(rows // tm,),
        in_specs=[row(D_MODEL), _const_spec((1, D_MODEL)), _const_spec((D_MODEL, wcols)),
                  _const_spec((1, HEAD_DIM)), _const_spec((1, HEAD_DIM)), tab, tab, tab, tab],
        out_specs=[row(N_HEADS * HEAD_DIM), row(N_KV_HEADS * HEAD_DIM),
                   pl.BlockSpec((None, N_KV_HEADS * VT_ROWS, tm), lambda i: (i, 0, 0)),
                   row(IDX_HEADS * IDX_DIM), row(LANES), row(LANES)],
        out_shape=[jax.ShapeDtypeStruct((rows, N_HEADS * HEAD_DIM), BF16),
                   jax.ShapeDtypeStruct((rows, N_KV_HEADS * HEAD_DIM), BF16),
                   jax.ShapeDtypeStruct((rows // tm, N_KV_HEADS * VT_ROWS, tm), BF16),
                   jax.ShapeDtypeStruct((rows, IDX_HEADS * IDX_DIM), BF16),
                   jax.ShapeDtypeStruct((rows, LANES), BF16),
                   jax.ShapeDtypeStruct((rows, LANES), F32)],
        compiler_params=pltpu.CompilerParams(dimension_semantics=("arbitrary",),
                                             vmem_limit_bytes=VMEM_LIMIT),
        name="attn_proj",
    )(h2, g, w, qg, kg, *tabs)


def _dsa_kernel(q_ref, qi_ref, wi_ref, ki_ref, k_ref, vt_ref, out_ref,
                key_ref, hi_ref, raw_ref, lo_ref, bias_ref, qt_ref, qit_ref, acc_ref, m_ref, sa_ref, sb_ref,
                tri_ref,
                *, n_valid, topk, tk):
    tq = TQ
    gw = KV_GROUP * tq
    t0 = pl.program_id(1) * tq
    tcol = t0 + lax.broadcasted_iota(jnp.int32, (1, tq), 1)
    kb = jnp.where(tcol < N_META, N_META,
                   N_META + CHUNK * (1 + jnp.right_shift(tcol - N_META, CHUNK.bit_length() - 1)))
    kb = jnp.where(tcol < n_valid, jnp.minimum(kb, n_valid), n_valid)
    kext = jnp.minimum(n_valid, N_META + CHUNK * (1 + (t0 + tq - 1 - N_META) // CHUNK))
    n_tiles = (kext + tk - 1) // tk
    kb_first = jnp.where(t0 < N_META, N_META, N_META + CHUNK * (1 + (t0 - N_META) // CHUNK))

    @pl.when((pl.program_id(0) == 0) & (pl.program_id(1) == 0))
    def _():
        tri_ref[...] = jnp.where(lax.broadcasted_iota(jnp.int32, (tk, tk), 0)
                                 <= lax.broadcasted_iota(jnp.int32, (tk, tk), 1), 1.0, 0.0).astype(BF16)

    def scores(t, s_ref):
        r0 = pl.multiple_of(t * tk, tk)
        for n in range(N_KV_HEADS):
            kt = k_ref[pl.ds(r0, tk), n * HEAD_DIM:(n + 1) * HEAD_DIM]
            s_ref[n] = jnp.dot(kt, qt_ref[n], preferred_element_type=F32).astype(BF16)

    for h in range(N_HEADS):
        qh = q_ref[:, h * HEAD_DIM:(h + 1) * HEAD_DIM].astype(F32).T
        qt_ref[h // KV_GROUP, :, (h % KV_GROUP) * tq:(h % KV_GROUP + 1) * tq] = qh.astype(BF16)
    scores(0, sa_ref)
    row = lax.broadcasted_iota(jnp.int32, (LANES, tq), 0)
    for hp in range(IDX_HEADS // 2):
        pair = qi_ref[:, hp * LANES:(hp + 1) * LANES].astype(F32).T
        qit_ref[:, (2 * hp) * tq:(2 * hp + 1) * tq] = jnp.where(row < IDX_DIM, pair, 0.0).astype(BF16)
        qit_ref[:, (2 * hp + 1) * tq:(2 * hp + 2) * tq] = jnp.where(row >= IDX_DIM, pair, 0.0).astype(BF16)
    w = wi_ref[...].T[IDX_DIM:IDX_DIM + IDX_HEADS, :]

    def score_tile(i, carry, masked):
        r0 = pl.multiple_of(i * tk, tk)
        kt = ki_ref[pl.ds(r0, tk), :]
        acc = jnp.zeros((tk, tq), F32)
        for hp in range(IDX_HEADS // 2):
            lg = jnp.dot(kt, qit_ref[:, hp * 2 * tq:(hp + 1) * 2 * tq],
                         preferred_element_type=F32)
            for hh in range(2):
                h = 2 * hp + hh
                acc = acc + jnp.maximum(lg[:, hh * tq:(hh + 1) * tq], 0.0) * w[h:h + 1, :]
        bits = pltpu.bitcast(acc, jnp.int32)
        key = jnp.where(bits >= 0, bits, I32_MIN - bits)
        if masked:
            pos = r0 + lax.broadcasted_iota(jnp.int32, (tk, tq), 0)
            key = jnp.where(pos < kb, key, KEY_MASKED)
        key_ref[pl.ds(r0, tk), :] = key
        hi_ref[pl.ds(r0, tk), :] = lax.shift_right_arithmetic(key, HALF_BITS).astype(jnp.int16)
        raw_ref[pl.ds(r0, tk), :] = ((key & HALF_MASK) + I16_MIN).astype(jnp.int16)
        return carry

    n_open = jnp.minimum(n_valid, kb_first) // tk
    lax.fori_loop(0, n_open, functools.partial(score_tile, masked=False), 0)
    lax.fori_loop(n_open, n_tiles, functools.partial(score_tile, masked=True), 0)

    def tile_count(mask):
        ind = jnp.where(mask, jnp.bfloat16(1), jnp.bfloat16(0))
        parts = [ind[r * PACKED_ROWS:(r + 1) * PACKED_ROWS] for r in range(tk // PACKED_ROWS)]
        while len(parts) > 1:
            nxt = [parts[r] + parts[r + 1] for r in range(0, len(parts) - 1, 2)]
            parts = nxt + ([parts[-1]] if len(parts) % 2 else [])
        return parts[0].astype(F32)

    def search16(ref, kth):
        def count_ge(cand):
            c16 = cand.astype(jnp.int16)

            def body(i, acc):
                r0 = pl.multiple_of(i * tk, tk)
                return acc + tile_count(ref[pl.ds(r0, tk), :] >= c16)

            acc = lax.fori_loop(0, n_tiles, body, jnp.zeros((PACKED_ROWS, tq), F32))
            return acc.sum(axis=0, keepdims=True)

        def step(it, carry):
            lo, c_lo = carry
            cand = lo + jnp.left_shift(jnp.int32(1), HALF_BITS - 1 - it)
            c = count_ge(cand)
            return jnp.where(c >= kth, cand, lo), jnp.where(c >= kth, c, c_lo)

        start = (jnp.full((1, tq), I16_MIN, jnp.int32), jnp.full((1, tq), n_tiles * tk, jnp.int32).astype(F32))
        return lax.fori_loop(0, HALF_BITS, step, start)

    tau_hi, c_high = search16(hi_ref, jnp.full((1, tq), float(topk), F32))

    tau_hi16 = tau_hi.astype(jnp.int16)

    def low_tile(i, acc):
        r0 = pl.multiple_of(i * tk, tk)
        hi = hi_ref[pl.ds(r0, tk), :]
        lo_ref[pl.ds(r0, tk), :] = jnp.where(hi == tau_hi16, raw_ref[pl.ds(r0, tk), :], jnp.int16(I16_MIN))
        return acc + tile_count(hi > tau_hi16)

    c_above = lax.fori_loop(0, n_tiles, low_tile, jnp.zeros((PACKED_ROWS, tq), F32))
    c_above = c_above.sum(axis=0, keepdims=True)
    tau_lo, c_low = search16(lo_ref, topk - c_above)
    tau = tau_hi * (HALF_MASK + 1) + (tau_lo - I16_MIN)
    c_ge = jnp.where(tau_lo == I16_MIN, c_high, c_above + c_low)
    excess = c_ge - topk
    cut = jnp.max(excess) > 0

    @pl.when(jnp.logical_not(cut))
    def _():
        def bias_tile(i, carry, masked):
            r0 = pl.multiple_of(i * tk, tk)
            keep = key_ref[pl.ds(r0, tk), :] >= tau
            if masked:
                keep = keep & (r0 + lax.broadcasted_iota(jnp.int32, (tk, tq), 0) < kb)
            bias_ref[pl.ds(r0, tk), :] = jnp.where(keep, 0.0, NEG_BIG).astype(BF16)
            return carry

        lax.fori_loop(0, n_open, functools.partial(bias_tile, masked=False), 0)
        lax.fori_loop(n_open, n_tiles, functools.partial(bias_tile, masked=True), 0)

    @pl.when(cut)
    def _():
        def tile_bias(t, later):
            r0 = pl.multiple_of(t * tk, tk)
            blk = key_ref[pl.ds(r0, tk), :]
            pos = r0 + lax.broadcasted_iota(jnp.int32, (tk, tq), 0)
            eq = blk == tau
            within = jnp.dot(tri_ref[...], jnp.where(eq, 1.0, 0.0).astype(BF16), preferred_element_type=F32)
            from_here = later + within
            keep = ((blk > tau) | (eq & (from_here > excess))) & (pos < kb)
            return r0, jnp.where(keep, 0.0, NEG_BIG).astype(BF16), from_here[0:1, :]

        def bias_pair(i, later):
            t = n_tiles - 1 - 2 * i
            ra, bias_a, later_a = tile_bias(t, later)
            rb, bias_b, later_b = tile_bias(jnp.maximum(t - 1, 0), later_a)
            bias_ref[pl.ds(rb, tk), :] = bias_b
            bias_ref[pl.ds(ra, tk), :] = bias_a
            return later_b

        lax.fori_loop(0, (n_tiles + 1) // 2, bias_pair, jnp.zeros((1, tq), F32))

    acc_ref[...] = jnp.zeros_like(acc_ref)
    m_ref[...] = jnp.full(m_ref.shape, NEG_BIG, F32)

    def softmax_pv(t, s_ref):
        r0 = pl.multiple_of(t * tk, tk)
        bt = bias_ref[pl.ds(r0, tk), :]
        bt = jnp.concatenate([bt] * KV_GROUP, axis=1)
        for n in range(N_KV_HEADS):
            s = s_ref[n] + bt
            m = m_ref[n]
            m_new = jnp.maximum(m, jnp.max(s, axis=0, keepdims=True).astype(F32))
            alpha = jnp.exp2(m - m_new)
            p = jnp.exp2(s - m_new.astype(BF16))
            m_ref[n] = m_new
            vt = vt_ref[t, n * VT_ROWS:(n + 1) * VT_ROWS, :]
            acc_ref[n] = alpha * acc_ref[n] + jnp.dot(vt, p, preferred_element_type=F32)

    last_tile = k_ref.shape[0] // tk - 1

    def tile_pair(i, carry):
        t = 2 * i
        scores(t + 1, sb_ref)
        softmax_pv(t, sa_ref)
        scores(jnp.minimum(t + 2, last_tile), sa_ref)
        softmax_pv(t + 1, sb_ref)
        return carry

    lax.fori_loop(0, n_tiles // 2, tile_pair, 0)

    @pl.when(n_tiles % 2 == 1)
    def _():
        softmax_pv(n_tiles - 1, sa_ref)

    for h in range(N_HEADS):
        n, g = h // KV_GROUP, h % KV_GROUP
        inv_l = 1.0 / acc_ref[n, HEAD_DIM:HEAD_DIM + 1, g * tq:(g + 1) * tq]
        oh = acc_ref[n, 0:HEAD_DIM, g * tq:(g + 1) * tq] * inv_l
        out_ref[:, h * HEAD_DIM:(h + 1) * HEAD_DIM] = oh.T.astype(BF16)


def _dsa_attention(q, qi, wi, ki, k, vt, n_valid, topk, tk):
    bsz, tp, _ = q.shape
    gw = KV_GROUP * TQ
    qrow = lambda width: pl.BlockSpec((None, TQ, width), lambda b, j: (b, j, 0))
    seq = lambda width: pl.BlockSpec((None, tp, width), lambda b, j: (b, 0, 0))
    return pl.pallas_call(
        functools.partial(_dsa_kernel, n_valid=n_valid, topk=topk, tk=tk),
        grid=(bsz, pl.cdiv(tp, TQ)),
        in_specs=[qrow(N_HEADS * HEAD_DIM), qrow(IDX_HEADS * IDX_DIM), qrow(LANES), seq(LANES),
                  seq(N_KV_HEADS * HEAD_DIM),
                  pl.BlockSpec((None, tp // tk, N_KV_HEADS * VT_ROWS, tk), lambda b, j: (b, 0, 0, 0))],
        out_specs=qrow(N_HEADS * HEAD_DIM),
        out_shape=jax.ShapeDtypeStruct((bsz, tp, N_HEADS * HEAD_DIM), BF16),
        scratch_shapes=[pltpu.VMEM((tp, TQ), jnp.int32), pltpu.VMEM((tp, TQ), jnp.int16),
                        pltpu.VMEM((tp, TQ), jnp.int16), pltpu.VMEM((tp, TQ), jnp.int16),
                        pltpu.VMEM((tp, TQ), BF16),
                        pltpu.VMEM((N_KV_HEADS, HEAD_DIM, gw), BF16),
                        pltpu.VMEM((LANES, IDX_HEADS * TQ), BF16),
                        pltpu.VMEM((N_KV_HEADS, VT_ROWS, gw), F32),
                        pltpu.VMEM((N_KV_HEADS, 1, gw), F32),
                        pltpu.VMEM((N_KV_HEADS, tk, gw), BF16), pltpu.VMEM((N_KV_HEADS, tk, gw), BF16),
                        pltpu.VMEM((tk, tk), BF16)],
        compiler_params=pltpu.CompilerParams(dimension_semantics=("arbitrary", "arbitrary"),
                                             vmem_limit_bytes=VMEM_LIMIT),
        name="dsa_attention",
    )(q, qi, wi, ki, k, vt)


def _rglru_kernel(h_ref, g_ref, w_ref, cw_ref, cb_ref, wa_ref, ba_ref, wx_ref, bx_ref, lam_ref,
                  out_ref, xext_ref, a_ref, b_ref, state_ref):
    tm = h_ref.shape[0]
    hist = SUBLANES

    @pl.when(pl.program_id(1) == 0)
    def _():
        xext_ref[0:hist, :] = jnp.zeros((hist, D_MODEL), F32)
        state_ref[...] = jnp.zeros_like(state_ref)

    n = _rms(h_ref[...], g_ref[...]).astype(BF16)
    xext_ref[hist:hist + tm, :] = jnp.dot(n, w_ref[:, 0:D_MODEL], preferred_element_type=F32)
    u = cb_ref[...] + cw_ref[0:1, :] * xext_ref[hist - RNN_CONV + 1:hist - RNN_CONV + 1 + tm, :]
    for kk in range(1, RNN_CONV):
        o = hist - RNN_CONV + 1 + kk
        u = u + cw_ref[kk:kk + 1, :] * xext_ref[o:o + tm, :]
    xext_ref[0:hist, :] = xext_ref[tm:tm + hist, :]

    ub = u.astype(BF16)
    rs, is_ = [], []
    for blk in range(RNN_BLOCKS):
        sl = slice(blk * RNN_BLOCK_DIM, (blk + 1) * RNN_BLOCK_DIM)
        rs.append(jnp.dot(ub[:, sl], wa_ref[blk], preferred_element_type=F32))
        is_.append(jnp.dot(ub[:, sl], wx_ref[blk], preferred_element_type=F32))
    r = jax.nn.sigmoid(jnp.concatenate(rs, axis=1) + ba_ref[...])
    ig = jax.nn.sigmoid(jnp.concatenate(is_, axis=1) + bx_ref[...])
    nl = -lam_ref[...]
    softplus = jnp.maximum(nl, 0.0) + jnp.log1p(jnp.exp(-jnp.abs(nl)))
    log_a = -LRU_C * r * softplus
    a_all = jnp.exp(log_a)
    b_all = jnp.sqrt(-_expm1(2.0 * log_a, a_all * a_all)) * (ig * u)
    nchunk = D_MODEL // LANES
    for c in range(nchunk):
        a_ref[c] = a_all[:, c * LANES:(c + 1) * LANES]
        b_ref[c] = b_all[:, c * LANES:(c + 1) * LANES]

    seg = tm // SCAN_SEGMENTS
    groups = SCAN_SEGMENTS // SUBLANES

    def step(i, carry):
        out = []
        for v in range(groups):
            prod, loc = carry[2 * v], carry[2 * v + 1]
            rows = pl.ds(v * SUBLANES * seg + i, SUBLANES, stride=seg)
            a = a_ref[:, rows, :]
            prod = a * prod
            loc = a * loc + b_ref[:, rows, :]
            a_ref[:, rows, :] = prod
            b_ref[:, rows, :] = loc
            out += [prod, loc]
        return tuple(out)

    init = (jnp.ones((nchunk, SUBLANES, LANES), F32), jnp.zeros((nchunk, SUBLANES, LANES), F32)) * groups
    ends = lax.fori_loop(0, seg, step, init)
    gel = jax.nn.gelu(jnp.dot(n, w_ref[:, D_MODEL:], preferred_element_type=F32))
    hcur = state_ref[...]
    for s in range(SCAN_SEGMENTS):
        rs = slice(s * seg, (s + 1) * seg)
        hs = a_ref[:, rs, :] * hcur + b_ref[:, rs, :]
        for c in range(nchunk):
            cs = slice(c * LANES, (c + 1) * LANES)
            out_ref[rs, cs] = (hs[c] * gel[rs, cs]).astype(BF16)
        v, r = s // SUBLANES, s % SUBLANES
        hcur = ends[2 * v][:, r:r + 1, :] * hcur + ends[2 * v + 1][:, r:r + 1, :]
    state_ref[...] = hcur


def _rglru(h3, g, w, cw, cb, wa, ba, wx, bx, lam, tm):
    bsz, tp, _ = h3.shape
    row = pl.BlockSpec((None, tm, D_MODEL), lambda b, t: (b, t, 0))
    vec = _const_spec((1, D_MODEL))
    blkw = _const_spec((RNN_BLOCKS, RNN_BLOCK_DIM, RNN_BLOCK_DIM))
    return pl.pallas_call(
        _rglru_kernel,
        grid=(bsz, tp // tm),
        in_specs=[row, vec, _const_spec((D_MODEL, 2 * D_MODEL)), _const_spec((RNN_CONV, D_MODEL)), vec,
                  blkw, vec, blkw, vec, vec],
        out_specs=row,
        out_shape=jax.ShapeDtypeStruct((bsz, tp, D_MODEL), BF16),
        scratch_shapes=[pltpu.VMEM((tm + SUBLANES, D_MODEL), F32),
                        pltpu.VMEM((D_MODEL // LANES, tm, LANES), F32),
                        pltpu.VMEM((D_MODEL // LANES, tm, LANES), F32),
                        pltpu.VMEM((D_MODEL // LANES, 1, LANES), F32)],
        compiler_params=pltpu.CompilerParams(dimension_semantics=("arbitrary", "arbitrary"),
                                             vmem_limit_bytes=VMEM_LIMIT),
        name="rglru",
    )(h3, g, w, cw, cb, wa, ba, wx, bx, lam)


SCAN_SEGMENTS = 16
CONV_HIST = 32
CONV_ROWS = 128


def _conformer_kernel(h_ref, g_ref, w_ref, dw_ref, db_ref, lg_ref, lb_ref, out_ref, xs_ref, y_ref):
    tm = h_ref.shape[0]

    @pl.when(pl.program_id(1) == 0)
    def _():
        xs_ref[0, 0:CONV_HIST, :] = jnp.zeros((CONV_HIST, D_MODEL), F32)

    n = _rms(h_ref[...], g_ref[...]).astype(BF16)
    a = jnp.dot(n, w_ref[:, 0:D_MODEL], preferred_element_type=F32)
    gate = jnp.dot(n, w_ref[:, D_MODEL:], preferred_element_type=F32)
    xs_ref[0, CONV_HIST:CONV_HIST + tm, :] = a * jax.nn.sigmoid(gate)
    span = tm + CONV_HIST - SUBLANES
    for s in range(1, SUBLANES):
        xs_ref[s, 0:span, :] = xs_ref[0, s:s + span, :]

    first = CONV_HIST - CONV_KERNEL + 1
    row_chunks = range(0, tm, CONV_ROWS)
    for c0 in range(0, D_MODEL, LANES):
        cs = slice(c0, c0 + LANES)
        accs = [jnp.broadcast_to(db_ref[:, cs], (CONV_ROWS, LANES)) for _ in row_chunks]
        for kk in range(CONV_KERNEL):
            shift = (first + kk) % SUBLANES
            base = first + kk - shift
            wk = dw_ref[kk:kk + 1, cs]
            for ri, r0 in enumerate(row_chunks):
                accs[ri] = accs[ri] + wk * xs_ref[shift, base + r0:base + r0 + CONV_ROWS, cs]
        for ri, r0 in enumerate(row_chunks):
            y_ref[r0:r0 + CONV_ROWS, cs] = accs[ri]
    xs_ref[0, 0:CONV_HIST, :] = xs_ref[0, tm:tm + CONV_HIST, :]

    y = y_ref[...]
    mu = jnp.mean(y, axis=-1, keepdims=True)
    yc = y - mu
    z = yc * lax.rsqrt(jnp.mean(yc * yc, axis=-1, keepdims=True) + NORM_EPS) * lg_ref[...] + lb_ref[...]
    out_ref[...] = jax.nn.silu(z).astype(BF16)


def _conformer(h3, g, w, dw, db, lg, lb, tm):
    bsz, tp, _ = h3.shape
    row = pl.BlockSpec((None, tm, D_MODEL), lambda b, t: (b, t, 0))
    vec = _const_spec((1, D_MODEL))
    return pl.pallas_call(
        _conformer_kernel,
        grid=(bsz, tp // tm),
        in_specs=[row, vec, _const_spec((D_MODEL, 2 * D_MODEL)), _const_spec((CONV_KERNEL, D_MODEL)),
                  vec, vec, vec],
        out_specs=row,
        out_shape=jax.ShapeDtypeStruct((bsz, tp, D_MODEL), BF16),
        scratch_shapes=[pltpu.VMEM((SUBLANES, tm + CONV_HIST, D_MODEL), F32), pltpu.VMEM((tm, D_MODEL), F32)],
        compiler_params=pltpu.CompilerParams(dimension_semantics=("arbitrary", "arbitrary"),
                                             vmem_limit_bytes=VMEM_LIMIT),
        name="conformer",
    )(h3, g, w, dw, db, lg, lb)


def _merge_kernel(h_ref, at_ref, rn_ref, cv_ref, g_ref, wg_ref, wa_ref, wr_ref, wc_ref, wo_ref, out_ref):
    h = h_ref[...]
    n = _rms(h, g_ref[...]).astype(BF16)
    merged = None
    for i, (src, wref) in enumerate(((at_ref, wa_ref), (rn_ref, wr_ref), (cv_ref, wc_ref))):
        gate = jax.nn.sigmoid(jnp.dot(n, wg_ref[:, i * D_MODEL:(i + 1) * D_MODEL],
                                      preferred_element_type=F32))
        term = gate * jnp.dot(src[...], wref[...], preferred_element_type=F32)
        merged = term if merged is None else merged + term
    out_ref[...] = h + jnp.dot(merged.astype(BF16), wo_ref[...], preferred_element_type=F32)


def _merge(h2, attn, rnn, cnv, g, wg, wa, wr, wc, wo, tm):
    rows = h2.shape[0]
    row = pl.BlockSpec((tm, D_MODEL), lambda i: (i, 0))
    sq = _const_spec((D_MODEL, D_MODEL))
    return pl.pallas_call(
        _merge_kernel,
        grid=(rows // tm,),
        in_specs=[row, row, row, row, _const_spec((1, D_MODEL)), _const_spec((D_MODEL, 3 * D_MODEL)),
                  sq, sq, sq, sq],
        out_specs=row,
        out_shape=jax.ShapeDtypeStruct((rows, D_MODEL), F32),
        compiler_params=pltpu.CompilerParams(dimension_semantics=("arbitrary",),
                                             vmem_limit_bytes=VMEM_LIMIT),
        name="merge",
    )(h2, attn, rnn, cnv, g, wg, wa, wr, wc, wo)


def _ffn_kernel(h_ref, g_ref, wg_ref, wu_ref, wd_ref, out_ref):
    h = h_ref[...]
    f = _rms(h, g_ref[...]).astype(BF16)
    gate = jnp.dot(f, wg_ref[...], preferred_element_type=F32)
    up = jnp.dot(f, wu_ref[...], preferred_element_type=F32)
    act = (jax.nn.silu(gate) * up).astype(BF16)
    out_ref[...] = h + jnp.dot(act, wd_ref[...], preferred_element_type=F32)


def _ffn(h2, g, wg, wu, wd, tm):
    rows = h2.shape[0]
    dff = wg.shape[1]
    row = pl.BlockSpec((tm, D_MODEL), lambda i: (i, 0))
    return pl.pallas_call(
        _ffn_kernel,
        grid=(rows // tm,),
        in_specs=[row, _const_spec((1, D_MODEL)), _const_spec((D_MODEL, dff)), _const_spec((D_MODEL, dff)),
                  _const_spec((dff, D_MODEL))],
        out_specs=row,
        out_shape=jax.ShapeDtypeStruct((rows, D_MODEL), F32),
        compiler_params=pltpu.CompilerParams(dimension_semantics=("arbitrary",),
                                             vmem_limit_bytes=VMEM_LIMIT),
        name="ffn",
    )(h2, g, wg, wu, wd)


def _rope_tables(n, dim):
    inv = ROPE_THETA ** (-jnp.arange(0, dim, 2, dtype=F32) / dim)
    ang = jnp.arange(n, dtype=F32)[:, None] * inv[None, :]
    return jnp.cos(ang), jnp.sin(ang)


def kernel(x, meta, mix_norm_g, w_in, q_norm_g, k_norm_g, rnn_conv_w, rnn_conv_b, rnn_wa, rnn_ba, rnn_wx, rnn_bx, rnn_lambda, conv_dw_w, conv_dw_b, conv_ln_g, conv_ln_b, w_o_attn, w_o_rnn, w_o_conv, w_out, ffn_norm_g, w_ffn_gate, w_ffn_up, w_ffn_down):
    bsz, seq, _ = x.shape
    depth = w_in.shape[0]
    t_valid = seq + N_META
    tp = -(-t_valid // LANES) * LANES
    topk = min(TOPK_MAX, seq // 4)
    rows = bsz * tp
    tm_seq = _pick_tile(tp, (384, 256, 128))
    tm_flat = _pick_tile(rows, (512, 384, 256, 128))

    q_w, kv_w, qi_w = N_HEADS * HEAD_DIM, N_KV_HEADS * HEAD_DIM, IDX_HEADS * IDX_DIM
    o = np.cumsum([0, q_w, kv_w, kv_w, qi_w, IDX_DIM, IDX_HEADS, D_MODEL, D_MODEL, 2 * D_MODEL, 3 * D_MODEL])
    pad_w = LANES - IDX_DIM - IDX_HEADS
    w_attn = jnp.concatenate([w_in[:, :, o[0]:o[6]], jnp.zeros((depth, D_MODEL, pad_w), w_in.dtype)],
                             axis=2).astype(BF16)
    w_rnn = w_in[:, :, o[6]:o[8]].astype(BF16)
    w_cnv = w_in[:, :, o[8]:o[9]].astype(BF16)
    w_gate = w_in[:, :, o[9]:o[10]].astype(BF16)
    wa_b, wx_b = rnn_wa.astype(BF16), rnn_wx.astype(BF16)
    w_oa, w_or, w_oc, w_ot = (a.astype(BF16) for a in (w_o_attn, w_o_rnn, w_o_conv, w_out))
    w_fg, w_fu, w_fd = (a.astype(BF16) for a in (w_ffn_gate, w_ffn_up, w_ffn_down))

    cos_a, sin_a = _rope_tables(tp, HEAD_DIM)
    cos_i, sin_i = _rope_tables(tp, IDX_DIM)
    tabs = (jnp.concatenate([cos_a, cos_a], axis=1), jnp.concatenate([-sin_a, sin_a], axis=1),
            jnp.concatenate([cos_i] * 4, axis=1), jnp.concatenate([-sin_i, sin_i] * 2, axis=1))

    h = jnp.concatenate([jnp.broadcast_to(meta[None].astype(x.dtype), (bsz, N_META, D_MODEL)), x,
                         jnp.zeros((bsz, tp - t_valid, D_MODEL), x.dtype)], axis=1)
    vec = lambda a: a.reshape(1, -1)

    for l in range(depth):
        h2 = h.reshape(rows, D_MODEL)
        q, k, vt, qi, ki, wi = _attn_proj(h2, vec(mix_norm_g[l]), w_attn[l], vec(q_norm_g[l]),
                                          vec(k_norm_g[l]), tabs, tm_seq, tp // tm_seq)
        seq3 = lambda a: a.reshape(bsz, tp, a.shape[-1])
        attn = _dsa_attention(seq3(q), seq3(qi), seq3(wi), seq3(ki), seq3(k),
                              vt.reshape(bsz, tp // tm_seq, N_KV_HEADS * VT_ROWS, tm_seq),
                              t_valid, topk, tm_seq)
        attn = attn.reshape(rows, N_HEADS * HEAD_DIM)

        rnn = _rglru(h, vec(mix_norm_g[l]), w_rnn[l], rnn_conv_w[l], vec(rnn_conv_b[l]), wa_b[l],
                     vec(rnn_ba[l]), wx_b[l], vec(rnn_bx[l]), vec(rnn_lambda[l]), tm_seq)
        cnv = _conformer(h, vec(mix_norm_g[l]), w_cnv[l], conv_dw_w[l], vec(conv_dw_b[l]),
                         vec(conv_ln_g[l]), vec(conv_ln_b[l]), tm_seq)
        h2 = _merge(h2, attn, rnn.reshape(rows, D_MODEL), cnv.reshape(rows, D_MODEL), vec(mix_norm_g[l]),
                    w_gate[l], w_oa[l], w_or[l], w_oc[l], w_ot[l], tm_flat)
        h2 = _ffn(h2, vec(ffn_norm_g[l]), w_fg[l], w_fu[l], w_fd[l], tm_flat)
        h = h2.reshape(bsz, tp, D_MODEL)

    return h[:, N_META:t_valid]
```

```python
import functools

import jax
import jax.numpy as jnp
import numpy as np
from jax import lax
from jax.experimental import pallas as pl
from jax.experimental.pallas import tpu as pltpu

D_MODEL = 1024
CHUNK = 64
N_META = 16
N_HEADS = 8
N_KV_HEADS = 2
HEAD_DIM = 128
KV_GROUP = N_HEADS // N_KV_HEADS
IDX_HEADS = 8
IDX_DIM = 64
TOPK_MAX = 256
ROPE_THETA = 10000.0
RNN_BLOCKS = 8
RNN_BLOCK_DIM = D_MODEL // RNN_BLOCKS
RNN_CONV = 4
LRU_C = 8.0
CONV_KERNEL = 31
NORM_EPS = 1e-6

LANES = 128
SUBLANES = 8
PACKED_ROWS = 16
TQ = 256
VMEM_LIMIT = 56 * 1024 * 1024
LOG2E = 1.4426950408889634

HALF_BITS = 16
HALF_MASK = 2 ** HALF_BITS - 1
I16_MIN = -(2 ** (HALF_BITS - 1))
I32_MIN = -(2 ** 31)
KEY_MASKED = -(0x7F800000 + 1)
NEG_BIG = -(2.0 ** 100)
VT_ROWS = HEAD_DIM + PACKED_ROWS

F32 = jnp.float32
BF16 = jnp.bfloat16


def _rms(x, g):
    return x * lax.rsqrt(jnp.mean(x * x, axis=-1, keepdims=True) + NORM_EPS) * g


def _expm1(y, u):
    near = jnp.where(u == 1.0, y, (u - 1.0) * y / jnp.log(jnp.where(u == 1.0, 2.0, u)))
    return jnp.where(y > -0.5, near, u - 1.0)


def _const_spec(shape):
    nd = len(shape)
    return pl.BlockSpec(shape, lambda *_: (0,) * nd, pipeline_mode=pl.Buffered(1))


def _pick_tile(n, candidates):
    for c in candidates:
        if n % c == 0:
            return c
    raise ValueError(f"no tile for {n}")


def _attn_proj_kernel(x_ref, g_ref, w_ref, qg_ref, kg_ref, cosa_ref, sina_ref, cosi_ref, sini_ref,
                      q_ref, k_ref, vt_ref, qi_ref, ki_ref, wi_ref, *, idx_scale):
    n = _rms(x_ref[...], g_ref[...]).astype(BF16)
    cosa, sina = cosa_ref[...], sina_ref[...]
    cosi, sini = cosi_ref[...], sini_ref[...]
    lane = lax.broadcasted_iota(jnp.int32, cosi.shape, 1)
    low_half = (lane % IDX_DIM) < (IDX_DIM // 2)

    def rot_attn(a, g):
        a = _rms(a, g)
        return a * cosa + pltpu.roll(a, HEAD_DIM // 2, axis=1) * sina

    def rot_idx(a):
        partner = jnp.where(low_half, pltpu.roll(a, LANES - IDX_DIM // 2, axis=1),
                            pltpu.roll(a, IDX_DIM // 2, axis=1))
        return a * cosi + partner * sini

    q_w = N_HEADS * HEAD_DIM
    kv_w = N_KV_HEADS * HEAD_DIM
    qi_w = IDX_HEADS * IDX_DIM
    o_k, o_v, o_qi, o_ki = q_w, q_w + kv_w, q_w + 2 * kv_w, q_w + 2 * kv_w + qi_w

    q_scale = (HEAD_DIM ** -0.5) * LOG2E
    pq = jnp.dot(n, w_ref[:, 0:q_w], preferred_element_type=F32)
    for h in range(N_HEADS):
        sl = slice(h * HEAD_DIM, (h + 1) * HEAD_DIM)
        q_ref[:, sl] = (rot_attn(pq[:, sl], qg_ref[...]) * q_scale).astype(BF16)
    pk = jnp.dot(n, w_ref[:, o_k:o_qi], preferred_element_type=F32)
    for h in range(N_KV_HEADS):
        sl = slice(h * HEAD_DIM, (h + 1) * HEAD_DIM)
        k_ref[:, sl] = rot_attn(pk[:, sl], kg_ref[...]).astype(BF16)
        vt_ref[h * VT_ROWS:h * VT_ROWS + HEAD_DIM, :] = (
            pk[:, kv_w + h * HEAD_DIM:kv_w + (h + 1) * HEAD_DIM].T.astype(BF16))
        vt_ref[h * VT_ROWS + HEAD_DIM:(h + 1) * VT_ROWS, :] = jnp.ones((PACKED_ROWS, pk.shape[0]), BF16)
    pi = jnp.dot(n, w_ref[:, o_qi:], preferred_element_type=F32)
    for c in range(qi_w // LANES):
        sl = slice(c * LANES, (c + 1) * LANES)
        qi_ref[:, sl] = rot_idx(pi[:, sl]).astype(BF16)
    last = pi[:, qi_w:]
    ki = rot_idx(last)
    ki_ref[...] = jnp.where(lane < IDX_DIM, ki, pltpu.roll(ki, IDX_DIM, axis=1)).astype(BF16)
    wi_ref[...] = last * idx_scale


def _attn_proj(h2, g, w, qg, kg, tabs, tm, tiles_per_seq):
    rows = h2.shape[0]
    wcols = w.shape[1]
    row = lambda width: pl.BlockSpec((tm, width), lambda i: (i, 0))
    tab = pl.BlockSpec((tm, LANES), lambda i: (i % tiles_per_seq, 0))
    idx_scale = (IDX_HEADS ** -0.5) * (IDX_DIM ** -0.5)
    return pl.pallas_call(
        functools.partial(_attn_proj_kernel, idx_scale=idx_scale),
        grid=(rows // tm,),
        in_specs=[row(D_MODEL), _const_spec((1, D_MODEL)), _const_spec((D_MODEL, wcols)),
                  _const_spec((1, HEAD_DIM)), _const_spec((1, HEAD_DIM)), tab, tab, tab, tab],
        out_specs=[row(N_HEADS * HEAD_DIM), row(N_KV_HEADS * HEAD_DIM),
                   pl.BlockSpec((None, N_KV_HEADS * VT_ROWS, tm), lambda i: (i, 0, 0)),
                   row(IDX_HEADS * IDX_DIM), row(LANES), row(LANES)],
        out_shape=[jax.ShapeDtypeStruct((rows, N_HEADS * HEAD_DIM), BF16),
                   jax.ShapeDtypeStruct((rows, N_KV_HEADS * HEAD_DIM), BF16),
                   jax.ShapeDtypeStruct((rows // tm, N_KV_HEADS * VT_ROWS, tm), BF16),
                   jax.ShapeDtypeStruct((rows, IDX_HEADS * IDX_DIM), BF16),
                   jax.ShapeDtypeStruct((rows, LANES), BF16),
                   jax.ShapeDtypeStruct((rows, LANES), F32)],
        compiler_params=pltpu.CompilerParams(dimension_semantics=("arbitrary",),
                                             vmem_limit_bytes=VMEM_LIMIT),
        name="attn_proj",
    )(h2, g, w, qg, kg, *tabs)


def _dsa_kernel(q_ref, qi_ref, wi_ref, ki_ref, k_ref, vt_ref, out_ref,
                key_ref, hi_ref, raw_ref, lo_ref, bias_ref, qt_ref, qit_ref, acc_ref, m_ref, sa_ref, sb_ref,
                tri_ref,
                *, n_valid, topk, tk):
    tq = TQ
    gw = KV_GROUP * tq
    t0 = pl.program_id(1) * tq
    tcol = t0 + lax.broadcasted_iota(jnp.int32, (1, tq), 1)
    kb = jnp.where(tcol < N_META, N_META,
                   N_META + CHUNK * (1 + jnp.right_shift(tcol - N_META, CHUNK.bit_length() - 1)))
    kb = jnp.where(tcol < n_valid, jnp.minimum(kb, n_valid), n_valid)
    kext = jnp.minimum(n_valid, N_META + CHUNK * (1 + (t0 + tq - 1 - N_META) // CHUNK))
    n_tiles = (kext + tk - 1) // tk
    kb_first = jnp.where(t0 < N_META, N_META, N_META + CHUNK * (1 + (t0 - N_META) // CHUNK))

    @pl.when((pl.program_id(0) == 0) & (pl.program_id(1) == 0))
    def _():
        tri_ref[...] = jnp.where(lax.broadcasted_iota(jnp.int32, (tk, tk), 0)
                                 <= lax.broadcasted_iota(jnp.int32, (tk, tk), 1), 1.0, 0.0).astype(BF16)

    def scores(t, s_ref):
        r0 = pl.multiple_of(t * tk, tk)
        for n in range(N_KV_HEADS):
            kt = k_ref[pl.ds(r0, tk), n * HEAD_DIM:(n + 1) * HEAD_DIM]
            s_ref[n] = jnp.dot(kt, qt_ref[n], preferred_element_type=F32).astype(BF16)

    for h in range(N_HEADS):
        qh = q_ref[:, h * HEAD_DIM:(h + 1) * HEAD_DIM].astype(F32).T
        qt_ref[h // KV_GROUP, :, (h % KV_GROUP) * tq:(h % KV_GROUP + 1) * tq] = qh.astype(BF16)
    scores(0, sa_ref)
    row = lax.broadcasted_iota(jnp.int32, (LANES, tq), 0)
    for hp in range(IDX_HEADS // 2):
        pair = qi_ref[:, hp * LANES:(hp + 1) * LANES].astype(F32).T
        qit_ref[:, (2 * hp) * tq:(2 * hp + 1) * tq] = jnp.where(row < IDX_DIM, pair, 0.0).astype(BF16)
        qit_ref[:, (2 * hp + 1) * tq:(2 * hp + 2) * tq] = jnp.where(row >= IDX_DIM, pair, 0.0).astype(BF16)
    w = wi_ref[...].T[IDX_DIM:IDX_DIM + IDX_HEADS, :]

    def score_tile(i, carry, masked):
        r0 = pl.multiple_of(i * tk, tk)
        kt = ki_ref[pl.ds(r0, tk), :]
        acc = jnp.zeros((tk, tq), F32)
        for hp in range(IDX_HEADS // 2):
            lg = jnp.dot(kt, qit_ref[:, hp * 2 * tq:(hp + 1) * 2 * tq],
                         preferred_element_type=F32)
            for hh in range(2):
                h = 2 * hp + hh
                acc = acc + jnp.maximum(lg[:, hh * tq:(hh + 1) * tq], 0.0) * w[h:h + 1, :]
        bits = pltpu.bitcast(acc, jnp.int32)
        key = jnp.where(bits >= 0, bits, I32_MIN - bits)
        if masked:
            pos = r0 + lax.broadcasted_iota(jnp.int32, (tk, tq), 0)
            key = jnp.where(pos < kb, key, KEY_MASKED)
        key_ref[pl.ds(r0, tk), :] = key
        hi_ref[pl.ds(r0, tk), :] = lax.shift_right_arithmetic(key, HALF_BITS).astype(jnp.int16)
        raw_ref[pl.ds(r0, tk), :] = ((key & HALF_MASK) + I16_MIN).astype(jnp.int16)
        return carry

    n_open = jnp.minimum(n_valid, kb_first) // tk
    lax.fori_loop(0, n_open, functools.partial(score_tile, masked=False), 0)
    lax.fori_loop(n_open, n_tiles, functools.partial(score_tile, masked=True), 0)

    def tile_count(mask):
        ind = jnp.where(mask, jnp.bfloat16(1), jnp.bfloat16(0))
        parts = [ind[r * PACKED_ROWS:(r + 1) * PACKED_ROWS] for r in range(tk // PACKED_ROWS)]
        while len(parts) > 1:
            nxt = [parts[r] + parts[r + 1] for r in range(0, len(parts) - 1, 2)]
            parts = nxt + ([parts[-1]] if len(parts) % 2 else [])
        return parts[0].astype(F32)

    def search16(ref, kth):
        def count_ge(cand):
            c16 = cand.astype(jnp.int16)

            def body(i, acc):
                r0 = pl.multiple_of(i * tk, tk)
                return acc + tile_count(ref[pl.ds(r0, tk), :] >= c16)

            acc = lax.fori_loop(0, n_tiles, body, jnp.zeros((PACKED_ROWS, tq), F32))
            return acc.sum(axis=0, keepdims=True)

        def step(it, carry):
            lo, c_lo = carry
            cand = lo + jnp.left_shift(jnp.int32(1), HALF_BITS - 1 - it)
            c = count_ge(cand)
            return jnp.where(c >= kth, cand, lo), jnp.where(c >= kth, c, c_lo)

        start = (jnp.full((1, tq), I16_MIN, jnp.int32), jnp.full((1, tq), n_tiles * tk, jnp.int32).astype(F32))
        return lax.fori_loop(0, HALF_BITS, step, start)

    tau_hi, c_high = search16(hi_ref, jnp.full((1, tq), float(topk), F32))

    tau_hi16 = tau_hi.astype(jnp.int16)

    def low_tile(i, acc):
        r0 = pl.multiple_of(i * tk, tk)
        hi = hi_ref[pl.ds(r0, tk), :]
        lo_ref[pl.ds(r0, tk), :] = jnp.where(hi == tau_hi16, raw_ref[pl.ds(r0, tk), :], jnp.int16(I16_MIN))
        return acc + tile_count(hi > tau_hi16)

    c_above = lax.fori_loop(0, n_tiles, low_tile, jnp.zeros((PACKED_ROWS, tq), F32))
    c_above = c_above.sum(axis=0, keepdims=True)
    tau_lo, c_low = search16(lo_ref, topk - c_above)
    tau = tau_hi * (HALF_MASK + 1) + (tau_lo - I16_MIN)
    c_ge = jnp.where(tau_lo == I16_MIN, c_high, c_above + c_low)
    excess = c_ge - topk
    cut = jnp.max(excess) > 0

    @pl.when(jnp.logical_not(cut))
    def _():
        def bias_tile(i, carry, masked):
            r0 = pl.multiple_of(i * tk, tk)
            keep = key_ref[pl.ds(r0, tk), :] >= tau
            if masked:
                keep = keep & (r0 + lax.broadcasted_iota(jnp.int32, (tk, tq), 0) < kb)
            bias_ref[pl.ds(r0, tk), :] = jnp.where(keep, 0.0, NEG_BIG).astype(BF16)
            return carry

        lax.fori_loop(0, n_open, functools.partial(bias_tile, masked=False), 0)
        lax.fori_loop(n_open, n_tiles, functools.partial(bias_tile, masked=True), 0)

    @pl.when(cut)
    def _():
        def tile_bias(t, later):
            r0 = pl.multiple_of(t * tk, tk)
            blk = key_ref[pl.ds(r0, tk), :]
            pos = r0 + lax.broadcasted_iota(jnp.int32, (tk, tq), 0)
            eq = blk == tau
            within = jnp.dot(tri_ref[...], jnp.where(eq, 1.0, 0.0).astype(BF16), preferred_element_type=F32)
            from_here = later + within
            keep = ((blk > tau) | (eq & (from_here > excess))) & (pos < kb)
            return r0, jnp.where(keep, 0.0, NEG_BIG).astype(BF16), from_here[0:1, :]

        def bias_pair(i, later):
            t = n_tiles - 1 - 2 * i
            ra, bias_a, later_a = tile_bias(t, later)
            rb, bias_b, later_b = tile_bias(jnp.maximum(t - 1, 0), later_a)
            bias_ref[pl.ds(rb, tk), :] = bias_b
            bias_ref[pl.ds(ra, tk), :] = bias_a
            return later_b

        lax.fori_loop(0, (n_tiles + 1) // 2, bias_pair, jnp.zeros((1, tq), F32))

    acc_ref[...] = jnp.zeros_like(acc_ref)
    m_ref[...] = jnp.full(m_ref.shape, NEG_BIG, F32)

    def softmax_pv(t, s_ref):
        r0 = pl.multiple_of(t * tk, tk)
        bt = bias_ref[pl.ds(r0, tk), :]
        bt = jnp.concatenate([bt] * KV_GROUP, axis=1)
        for n in range(N_KV_HEADS):
            s = s_ref[n] + bt
            m = m_ref[n]
            m_new = jnp.maximum(m, jnp.max(s, axis=0, keepdims=True).astype(F32))
            alpha = jnp.exp2(m - m_new)
            p = jnp.exp2(s - m_new.astype(BF16))
            m_ref[n] = m_new
            vt = vt_ref[t, n * VT_ROWS:(n + 1) * VT_ROWS, :]
            acc_ref[n] = alpha * acc_ref[n] + jnp.dot(vt, p, preferred_element_type=F32)

    last_tile = k_ref.shape[0] // tk - 1

    def tile_pair(i, carry):
        t = 2 * i
        scores(t + 1, sb_ref)
        softmax_pv(t, sa_ref)
        scores(jnp.minimum(t + 2, last_tile), sa_ref)
        softmax_pv(t + 1, sb_ref)
        return carry

    lax.fori_loop(0, n_tiles // 2, tile_pair, 0)

    @pl.when(n_tiles % 2 == 1)
    def _():
        softmax_pv(n_tiles - 1, sa_ref)

    for h in range(N_HEADS):
        n, g = h // KV_GROUP, h % KV_GROUP
        inv_l = 1.0 / acc_ref[n, HEAD_DIM:HEAD_DIM + 1, g * tq:(g + 1) * tq]
        oh = acc_ref[n, 0:HEAD_DIM, g * tq:(g + 1) * tq] * inv_l
        out_ref[:, h * HEAD_DIM:(h + 1) * HEAD_DIM] = oh.T.astype(BF16)


def _dsa_attention(q, qi, wi, ki, k, vt, n_valid, topk, tk):
    bsz, tp, _ = q.shape
    gw = KV_GROUP * TQ
    qrow = lambda width: pl.BlockSpec((None, TQ, width), lambda b, j: (b, j, 0))
    seq = lambda width: pl.BlockSpec((None, tp, width), lambda b, j: (b, 0, 0))
    return pl.pallas_call(
        functools.partial(_dsa_kernel, n_valid=n_valid, topk=topk, tk=tk),
        grid=(bsz, pl.cdiv(tp, TQ)),
        in_specs=[qrow(N_HEADS * HEAD_DIM), qrow(IDX_HEADS * IDX_DIM), qrow(LANES), seq(LANES),
                  seq(N_KV_HEADS * HEAD_DIM),
                  pl.BlockSpec((None, tp // tk, N_KV_HEADS * VT_ROWS, tk), lambda b, j: (b, 0, 0, 0))],
        out_specs=qrow(N_HEADS * HEAD_DIM),
        out_shape=jax.ShapeDtypeStruct((bsz, tp, N_HEADS * HEAD_DIM), BF16),
        scratch_shapes=[pltpu.VMEM((tp, TQ), jnp.int32), pltpu.VMEM((tp, TQ), jnp.int16),
                        pltpu.VMEM((tp, TQ), jnp.int16), pltpu.VMEM((tp, TQ), jnp.int16),
                        pltpu.VMEM((tp, TQ), BF16),
                        pltpu.VMEM((N_KV_HEADS, HEAD_DIM, gw), BF16),
                        pltpu.VMEM((LANES, IDX_HEADS * TQ), BF16),
                        pltpu.VMEM((N_KV_HEADS, VT_ROWS, gw), F32),
                        pltpu.VMEM((N_KV_HEADS, 1, gw), F32),
                        pltpu.VMEM((N_KV_HEADS, tk, gw), BF16), pltpu.VMEM((N_KV_HEADS, tk, gw), BF16),
                        pltpu.VMEM((tk, tk), BF16)],
        compiler_params=pltpu.CompilerParams(dimension_semantics=("arbitrary", "arbitrary"),
                                             vmem_limit_bytes=VMEM_LIMIT),
        name="dsa_attention",
    )(q, qi, wi, ki, k, vt)


def _rglru_kernel(h_ref, g_ref, w_ref, cw_ref, cb_ref, wa_ref, ba_ref, wx_ref, bx_ref, lam_ref,
                  out_ref, xext_ref, a_ref, b_ref, state_ref):
    tm = h_ref.shape[0]
    hist = SUBLANES

    @pl.when(pl.program_id(1) == 0)
    def _():
        xext_ref[0:hist, :] = jnp.zeros((hist, D_MODEL), F32)
        state_ref[...] = jnp.zeros_like(state_ref)

    n = _rms(h_ref[...], g_ref[...]).astype(BF16)
    xext_ref[hist:hist + tm, :] = jnp.dot(n, w_ref[:, 0:D_MODEL], preferred_element_type=F32)
    u = cb_ref[...] + cw_ref[0:1, :] * xext_ref[hist - RNN_CONV + 1:hist - RNN_CONV + 1 + tm, :]
    for kk in range(1, RNN_CONV):
        o = hist - RNN_CONV + 1 + kk
        u = u + cw_ref[kk:kk + 1, :] * xext_ref[o:o + tm, :]
    xext_ref[0:hist, :] = xext_ref[tm:tm + hist, :]

    ub = u.astype(BF16)
    rs, is_ = [], []
    for blk in range(RNN_BLOCKS):
        sl = slice(blk * RNN_BLOCK_DIM, (blk + 1) * RNN_BLOCK_DIM)
        rs.append(jnp.dot(ub[:, sl], wa_ref[blk], preferred_element_type=F32))
        is_.append(jnp.dot(ub[:, sl], wx_ref[blk], preferred_element_type=F32))
    r = jax.nn.sigmoid(jnp.concatenate(rs, axis=1) + ba_ref[...])
    ig = jax.nn.sigmoid(jnp.concatenate(is_, axis=1) + bx_ref[...])
    nl = -lam_ref[...]
    softplus = jnp.maximum(nl, 0.0) + jnp.log1p(jnp.exp(-jnp.abs(nl)))
    log_a = -LRU_C * r * softplus
    a_all = jnp.exp(log_a)
    b_all = jnp.sqrt(-_expm1(2.0 * log_a, a_all * a_all)) * (ig * u)
    nchunk = D_MODEL // LANES
    for c in range(nchunk):
        a_ref[c] = a_all[:, c * LANES:(c + 1) * LANES]
        b_ref[c] = b_all[:, c * LANES:(c + 1) * LANES]

    seg = tm // SCAN_SEGMENTS
    groups = SCAN_SEGMENTS // SUBLANES

    def step(i, carry):
        out = []
        for v in range(groups):
            prod, loc = carry[2 * v], carry[2 * v + 1]
            rows = pl.ds(v * SUBLANES * seg + i, SUBLANES, stride=seg)
            a = a_ref[:, rows, :]
            prod = a * prod
            loc = a * loc + b_ref[:, rows, :]
            a_ref[:, rows, :] = prod
            b_ref[:, rows, :] = loc
            out += [prod, loc]
        return tuple(out)

    init = (jnp.ones((nchunk, SUBLANES, LANES), F32), jnp.zeros((nchunk, SUBLANES, LANES), F32)) * groups
    ends = lax.fori_loop(0, seg, step, init)
    gel = jax.nn.gelu(jnp.dot(n, w_ref[:, D_MODEL:], preferred_element_type=F32))
    hcur = state_ref[...]
    for s in range(SCAN_SEGMENTS):
        rs = slice(s * seg, (s + 1) * seg)
        hs = a_ref[:, rs, :] * hcur + b_ref[:, rs, :]
        for c in range(nchunk):
            cs = slice(c * LANES, (c + 1) * LANES)
            out_ref[rs, cs] = (hs[c] * gel[rs, cs]).astype(BF16)
        v, r = s // SUBLANES, s % SUBLANES
        hcur = ends[2 * v][:, r:r + 1, :] * hcur + ends[2 * v + 1][:, r:r + 1, :]
    state_ref[...] = hcur


def _rglru(h3, g, w, cw, cb, wa, ba, wx, bx, lam, tm):
    bsz, tp, _ = h3.shape
    row = pl.BlockSpec((None, tm, D_MODEL), lambda b, t: (b, t, 0))
    vec = _const_spec((1, D_MODEL))
    blkw = _const_spec((RNN_BLOCKS, RNN_BLOCK_DIM, RNN_BLOCK_DIM))
    return pl.pallas_call(
        _rglru_kernel,
        grid=(bsz, tp // tm),
        in_specs=[row, vec, _const_spec((D_MODEL, 2 * D_MODEL)), _const_spec((RNN_CONV, D_MODEL)), vec,
                  blkw, vec, blkw, vec, vec],
        out_specs=row,
        out_shape=jax.ShapeDtypeStruct((bsz, tp, D_MODEL), BF16),
        scratch_shapes=[pltpu.VMEM((tm + SUBLANES, D_MODEL), F32),
                        pltpu.VMEM((D_MODEL // LANES, tm, LANES), F32),
                        pltpu.VMEM((D_MODEL // LANES, tm, LANES), F32),
                        pltpu.VMEM((D_MODEL // LANES, 1, LANES), F32)],
        compiler_params=pltpu.CompilerParams(dimension_semantics=("arbitrary", "arbitrary"),
                                             vmem_limit_bytes=VMEM_LIMIT),
        name="rglru",
    )(h3, g, w, cw, cb, wa, ba, wx, bx, lam)


SCAN_SEGMENTS = 16
CONV_HIST = 32
CONV_ROWS = 128


def _conformer_body(h_ref, g_ref, w_ref, dw_ref, db_ref, lg_ref, lb_ref, out_ref, xs_ref, y_ref):
    tm = h_ref.shape[0]

    n = _rms(h_ref[...], g_ref[...]).astype(BF16)
    a = jnp.dot(n, w_ref[:, 0:D_MODEL], preferred_element_type=F32)
    gate = jnp.dot(n, w_ref[:, D_MODEL:], preferred_element_type=F32)
    xs_ref[0, CONV_HIST:CONV_HIST + tm, :] = a * jax.nn.sigmoid(gate)
    span = tm + CONV_HIST - SUBLANES
    for s in range(1, SUBLANES):
        xs_ref[s, 0:span, :] = xs_ref[0, s:s + span, :]

    first = CONV_HIST - CONV_KERNEL + 1
    row_chunks = range(0, tm, CONV_ROWS)
    for c0 in range(0, D_MODEL, LANES):
        cs = slice(c0, c0 + LANES)
        accs = [jnp.broadcast_to(db_ref[:, cs], (CONV_ROWS, LANES)) for _ in row_chunks]
        for kk in range(CONV_KERNEL):
            shift = (first + kk) % SUBLANES
            base = first + kk - shift
            wk = dw_ref[kk:kk + 1, cs]
            for ri, r0 in enumerate(row_chunks):
                accs[ri] = accs[ri] + wk * xs_ref[shift, base + r0:base + r0 + CONV_ROWS, cs]
        for ri, r0 in enumerate(row_chunks):
            y_ref[r0:r0 + CONV_ROWS, cs] = accs[ri]
    xs_ref[0, 0:CONV_HIST, :] = xs_ref[0, tm:tm + CONV_HIST, :]

    y = y_ref[...]
    mu = jnp.mean(y, axis=-1, keepdims=True)
    yc = y - mu
    z = yc * lax.rsqrt(jnp.mean(yc * yc, axis=-1, keepdims=True) + NORM_EPS) * lg_ref[...] + lb_ref[...]
    out_ref[...] = jax.nn.silu(z).astype(BF16)


def _attn_conf_kernel(h_ref, g_ref, wa_ref, qg_ref, kg_ref, cosa_ref, sina_ref, cosi_ref, sini_ref,
                      wc_ref, dw_ref, db_ref, lg_ref, lb_ref,
                      q_ref, k_ref, vt_ref, qi_ref, ki_ref, wi_ref, cnv_ref, xs_ref, y_ref, *, idx_scale):
    @pl.when(pl.program_id(1) == 0)
    def _():
        xs_ref[0, 0:CONV_HIST, :] = jnp.zeros((CONV_HIST, D_MODEL), F32)

    _attn_proj_kernel(h_ref, g_ref, wa_ref, qg_ref, kg_ref, cosa_ref, sina_ref, cosi_ref, sini_ref,
                      q_ref, k_ref, vt_ref, qi_ref, ki_ref, wi_ref, idx_scale=idx_scale)
    _conformer_body(h_ref, g_ref, wc_ref, dw_ref, db_ref, lg_ref, lb_ref, cnv_ref, xs_ref, y_ref)


def _attn_conf(h3, g, wa, qg, kg, tabs, wc, dw, db, lg, lb, tm):
    bsz, tp, _ = h3.shape
    row = lambda width: pl.BlockSpec((None, tm, width), lambda b, t: (b, t, 0))
    tab = pl.BlockSpec((tm, LANES), lambda b, t: (t, 0))
    vec = _const_spec((1, D_MODEL))
    hvec = _const_spec((1, HEAD_DIM))
    idx_scale = (IDX_HEADS ** -0.5) * (IDX_DIM ** -0.5)
    bf = lambda width: jax.ShapeDtypeStruct((bsz, tp, width), BF16)
    return pl.pallas_call(
        functools.partial(_attn_conf_kernel, idx_scale=idx_scale),
        grid=(bsz, tp // tm),
        in_specs=[row(D_MODEL), vec, _const_spec((D_MODEL, wa.shape[1])), hvec, hvec, tab, tab, tab, tab,
                  _const_spec((D_MODEL, 2 * D_MODEL)), _const_spec((CONV_KERNEL, D_MODEL)), vec, vec, vec],
        out_specs=[row(N_HEADS * HEAD_DIM), row(N_KV_HEADS * HEAD_DIM),
                   pl.BlockSpec((None, None, N_KV_HEADS * VT_ROWS, tm), lambda b, t: (b, t, 0, 0)),
                   row(IDX_HEADS * IDX_DIM), row(LANES), row(LANES), row(D_MODEL)],
        out_shape=[bf(N_HEADS * HEAD_DIM), bf(N_KV_HEADS * HEAD_DIM),
                   jax.ShapeDtypeStruct((bsz, tp // tm, N_KV_HEADS * VT_ROWS, tm), BF16),
                   bf(IDX_HEADS * IDX_DIM), bf(LANES), jax.ShapeDtypeStruct((bsz, tp, LANES), F32),
                   bf(D_MODEL)],
        scratch_shapes=[pltpu.VMEM((SUBLANES, tm + CONV_HIST, D_MODEL), F32), pltpu.VMEM((tm, D_MODEL), F32)],
        compiler_params=pltpu.CompilerParams(dimension_semantics=("arbitrary", "arbitrary"),
                                             vmem_limit_bytes=VMEM_LIMIT),
        name="attn_conf",
    )(h3, g, wa, qg, kg, *tabs, wc, dw, db, lg, lb)


def _merge_kernel(h_ref, at_ref, rn_ref, cv_ref, g_ref, wg_ref, wa_ref, wr_ref, wc_ref, wo_ref, out_ref):
    h = h_ref[...]
    n = _rms(h, g_ref[...]).astype(BF16)
    merged = None
    for i, (src, wref) in enumerate(((at_ref, wa_ref), (rn_ref, wr_ref), (cv_ref, wc_ref))):
        gate = jax.nn.sigmoid(jnp.dot(n, wg_ref[:, i * D_MODEL:(i + 1) * D_MODEL],
                                      preferred_element_type=F32))
        term = gate * jnp.dot(src[...], wref[...], preferred_element_type=F32)
        merged = term if merged is None else merged + term
    out_ref[...] = h + jnp.dot(merged.astype(BF16), wo_ref[...], preferred_element_type=F32)


def _merge(h2, attn, rnn, cnv, g, wg, wa, wr, wc, wo, tm):
    rows = h2.shape[0]
    row = pl.BlockSpec((tm, D_MODEL), lambda i: (i, 0))
    sq = _const_spec((D_MODEL, D_MODEL))
    return pl.pallas_call(
        _merge_kernel,
        grid=(rows // tm,),
        in_specs=[row, row, row, row, _const_spec((1, D_MODEL)), _const_spec((D_MODEL, 3 * D_MODEL)),
                  sq, sq, sq, sq],
        out_specs=row,
        out_shape=jax.ShapeDtypeStruct((rows, D_MODEL), F32),
        compiler_params=pltpu.CompilerParams(dimension_semantics=("arbitrary",),
                                             vmem_limit_bytes=VMEM_LIMIT),
        name="merge",
    )(h2, attn, rnn, cnv, g, wg, wa, wr, wc, wo)


def _ffn_kernel(h_ref, g_ref, wg_ref, wu_ref, wd_ref, out_ref):
    h = h_ref[...]
    f = _rms(h, g_ref[...]).astype(BF16)
    gate = jnp.dot(f, wg_ref[...], preferred_element_type=F32)
    up = jnp.dot(f, wu_ref[...], preferred_element_type=F32)
    act = (jax.nn.silu(gate) * up).astype(BF16)
    out_ref[...] = h + jnp.dot(act, wd_ref[...], preferred_element_type=F32)


def _ffn(h2, g, wg, wu, wd, tm):
    rows = h2.shape[0]
    dff = wg.shape[1]
    row = pl.BlockSpec((tm, D_MODEL), lambda i: (i, 0))
    return pl.pallas_call(
        _ffn_kernel,
        grid=(rows // tm,),
        in_specs=[row, _const_spec((1, D_MODEL)), _const_spec((D_MODEL, dff)), _const_spec((D_MODEL, dff)),
                  _const_spec((dff, D_MODEL))],
        out_specs=row,
        out_shape=jax.ShapeDtypeStruct((rows, D_MODEL), F32),
        compiler_params=pltpu.CompilerParams(dimension_semantics=("arbitrary",),
                                             vmem_limit_bytes=VMEM_LIMIT),
        name="ffn",
    )(h2, g, wg, wu, wd)


def _rope_tables(n, dim):
    inv = ROPE_THETA ** (-jnp.arange(0, dim, 2, dtype=F32) / dim)
    ang = jnp.arange(n, dtype=F32)[:, None] * inv[None, :]
    return jnp.cos(ang), jnp.sin(ang)


def kernel(x, meta, mix_norm_g, w_in, q_norm_g, k_norm_g, rnn_conv_w, rnn_conv_b, rnn_wa, rnn_ba, rnn_wx, rnn_bx, rnn_lambda, conv_dw_w, conv_dw_b, conv_ln_g, conv_ln_b, w_o_attn, w_o_rnn, w_o_conv, w_out, ffn_norm_g, w_ffn_gate, w_ffn_up, w_ffn_down):
    bsz, seq, _ = x.shape
    depth = w_in.shape[0]
    t_valid = seq + N_META
    tp = -(-t_valid // LANES) * LANES
    topk = min(TOPK_MAX, seq // 4)
    rows = bsz * tp
    tm_seq = _pick_tile(tp, (384, 256, 128))
    tm_flat = _pick_tile(rows, (512, 384, 256, 128))

    q_w, kv_w, qi_w = N_HEADS * HEAD_DIM, N_KV_HEADS * HEAD_DIM, IDX_HEADS * IDX_DIM
    o = np.cumsum([0, q_w, kv_w, kv_w, qi_w, IDX_DIM, IDX_HEADS, D_MODEL, D_MODEL, 2 * D_MODEL, 3 * D_MODEL])
    pad_w = LANES - IDX_DIM - IDX_HEADS
    w_attn = jnp.concatenate([w_in[:, :, o[0]:o[6]], jnp.zeros((depth, D_MODEL, pad_w), w_in.dtype)],
                             axis=2).astype(BF16)
    w_rnn = w_in[:, :, o[6]:o[8]].astype(BF16)
    w_cnv = w_in[:, :, o[8]:o[9]].astype(BF16)
    w_gate = w_in[:, :, o[9]:o[10]].astype(BF16)
    wa_b, wx_b = rnn_wa.astype(BF16), rnn_wx.astype(BF16)
    w_oa, w_or, w_oc, w_ot = (a.astype(BF16) for a in (w_o_attn, w_o_rnn, w_o_conv, w_out))
    w_fg, w_fu, w_fd = (a.astype(BF16) for a in (w_ffn_gate, w_ffn_up, w_ffn_down))

    cos_a, sin_a = _rope_tables(tp, HEAD_DIM)
    cos_i, sin_i = _rope_tables(tp, IDX_DIM)
    tabs = (jnp.concatenate([cos_a, cos_a], axis=1), jnp.concatenate([-sin_a, sin_a], axis=1),
            jnp.concatenate([cos_i] * 4, axis=1), jnp.concatenate([-sin_i, sin_i] * 2, axis=1))

    h = jnp.concatenate([jnp.broadcast_to(meta[None].astype(x.dtype), (bsz, N_META, D_MODEL)), x,
                         jnp.zeros((bsz, tp - t_valid, D_MODEL), x.dtype)], axis=1)
    vec = lambda a: a.reshape(1, -1)

    for l in range(depth):
        h2 = h.reshape(rows, D_MODEL)
        q, k, vt, qi, ki, wi, cnv = _attn_conf(
            h, vec(mix_norm_g[l]), w_attn[l], vec(q_norm_g[l]), vec(k_norm_g[l]), tabs,
            w_cnv[l], conv_dw_w[l], vec(conv_dw_b[l]), vec(conv_ln_g[l]), vec(conv_ln_b[l]), tm_seq)
        attn = _dsa_attention(q, qi, wi, ki, k, vt, t_valid, topk, tm_seq)
        attn = attn.reshape(rows, N_HEADS * HEAD_DIM)

        rnn = _rglru(h, vec(mix_norm_g[l]), w_rnn[l], rnn_conv_w[l], vec(rnn_conv_b[l]), wa_b[l],
                     vec(rnn_ba[l]), wx_b[l], vec(rnn_bx[l]), vec(rnn_lambda[l]), tm_seq)
        h2 = _merge(h2, attn, rnn.reshape(rows, D_MODEL), cnv.reshape(rows, D_MODEL), vec(mix_norm_g[l]),
                    w_gate[l], w_oa[l], w_or[l], w_oc[l], w_ot[l], tm_flat)
        h2 = _ffn(h2, vec(ffn_norm_g[l]), w_fg[l], w_fu[l], w_fd[l], tm_flat)
        h = h2.reshape(bsz, tp, D_MODEL)

    return h[:, N_META:t_valid]
```

```python
import functools

import jax
import jax.numpy as jnp
import numpy as np
from jax import lax
from jax.experimental import pallas as pl
from jax.experimental.pallas import tpu as pltpu

D_MODEL = 1024
CHUNK = 64
N_META = 16
N_HEADS = 8
N_KV_HEADS = 2
HEAD_DIM = 128
KV_GROUP = N_HEADS // N_KV_HEADS
IDX_HEADS = 8
IDX_DIM = 64
TOPK_MAX = 256
ROPE_THETA = 10000.0
RNN_BLOCKS = 8
RNN_BLOCK_DIM = D_MODEL // RNN_BLOCKS
RNN_CONV = 4
LRU_C = 8.0
CONV_KERNEL = 31
NORM_EPS = 1e-6

LANES = 128
SUBLANES = 8
PACKED_ROWS = 16
TQ = 256
VMEM_LIMIT = 56 * 1024 * 1024
LOG2E = 1.4426950408889634

HALF_BITS = 16
HALF_MASK = 2 ** HALF_BITS - 1
I16_MIN = -(2 ** (HALF_BITS - 1))
I32_MIN = -(2 ** 31)
KEY_MASKED = -(0x7F800000 + 1)
NEG_BIG = -(2.0 ** 100)
VT_ROWS = HEAD_DIM + PACKED_ROWS

F32 = jnp.float32
BF16 = jnp.bfloat16


def _rms(x, g):
    return x * lax.rsqrt(jnp.mean(x * x, axis=-1, keepdims=True) + NORM_EPS) * g


def _expm1(y, u):
    near = jnp.where(u == 1.0, y, (u - 1.0) * y / jnp.log(jnp.where(u == 1.0, 2.0, u)))
    return jnp.where(y > -0.5, near, u - 1.0)


def _const_spec(shape):
    nd = len(shape)
    return pl.BlockSpec(shape, lambda *_: (0,) * nd, pipeline_mode=pl.Buffered(1))


def _pick_tile(n, candidates):
    for c in candidates:
        if n % c == 0:
            return c
    raise ValueError(f"no tile for {n}")


def _attn_proj_pieces(n, w_ref, qg_ref, kg_ref, cosa_ref, sina_ref, cosi_ref, sini_ref,
                      q_ref, k_ref, vt_ref, qi_ref, ki_ref, wi_ref, idx_scale):
    cosa, sina = cosa_ref[...], sina_ref[...]
    cosi, sini = cosi_ref[...], sini_ref[...]
    lane = lax.broadcasted_iota(jnp.int32, cosi.shape, 1)
    low_half = (lane % IDX_DIM) < (IDX_DIM // 2)

    def rot_attn(a, g):
        a = _rms(a, g)
        return a * cosa + pltpu.roll(a, HEAD_DIM // 2, axis=1) * sina

    def rot_idx(a):
        partner = jnp.where(low_half, pltpu.roll(a, LANES - IDX_DIM // 2, axis=1),
                            pltpu.roll(a, IDX_DIM // 2, axis=1))
        return a * cosi + partner * sini

    q_w = N_HEADS * HEAD_DIM
    kv_w = N_KV_HEADS * HEAD_DIM
    qi_w = IDX_HEADS * IDX_DIM
    o_k, o_v, o_qi, o_ki = q_w, q_w + kv_w, q_w + 2 * kv_w, q_w + 2 * kv_w + qi_w

    q_scale = (HEAD_DIM ** -0.5) * LOG2E

    def q_heads(h0, h1):
        def piece():
            pq = jnp.dot(n, w_ref[:, h0 * HEAD_DIM:h1 * HEAD_DIM], preferred_element_type=F32)
            for h in range(h0, h1):
                sl = slice((h - h0) * HEAD_DIM, (h - h0 + 1) * HEAD_DIM)
                q_ref[:, h * HEAD_DIM:(h + 1) * HEAD_DIM] = (
                    rot_attn(pq[:, sl], qg_ref[...]) * q_scale).astype(BF16)
        return piece

    def kv():
        pk = jnp.dot(n, w_ref[:, o_k:o_qi], preferred_element_type=F32)
        for h in range(N_KV_HEADS):
            sl = slice(h * HEAD_DIM, (h + 1) * HEAD_DIM)
            k_ref[:, sl] = rot_attn(pk[:, sl], kg_ref[...]).astype(BF16)
            vt_ref[h * VT_ROWS:h * VT_ROWS + HEAD_DIM, :] = (
                pk[:, kv_w + h * HEAD_DIM:kv_w + (h + 1) * HEAD_DIM].T.astype(BF16))
            vt_ref[h * VT_ROWS + HEAD_DIM:(h + 1) * VT_ROWS, :] = jnp.ones((PACKED_ROWS, pk.shape[0]), BF16)

    def indexer():
        pi = jnp.dot(n, w_ref[:, o_qi:], preferred_element_type=F32)
        for c in range(qi_w // LANES):
            sl = slice(c * LANES, (c + 1) * LANES)
            qi_ref[:, sl] = rot_idx(pi[:, sl]).astype(BF16)
        last = pi[:, qi_w:]
        ki = rot_idx(last)
        ki_ref[...] = jnp.where(lane < IDX_DIM, ki, pltpu.roll(ki, IDX_DIM, axis=1)).astype(BF16)
        wi_ref[...] = last * idx_scale

    return [q_heads(0, N_HEADS // 2), q_heads(N_HEADS // 2, N_HEADS), kv, indexer]


def _dsa_kernel(q_ref, qi_ref, wi_ref, ki_ref, k_ref, vt_ref, out_ref,
                key_ref, hi_ref, raw_ref, lo_ref, bias_ref, qt_ref, qit_ref, acc_ref, m_ref, sa_ref, sb_ref,
                tri_ref,
                *, n_valid, topk, tk):
    tq = TQ
    gw = KV_GROUP * tq
    t0 = pl.program_id(1) * tq
    tcol = t0 + lax.broadcasted_iota(jnp.int32, (1, tq), 1)
    kb = jnp.where(tcol < N_META, N_META,
                   N_META + CHUNK * (1 + jnp.right_shift(tcol - N_META, CHUNK.bit_length() - 1)))
    kb = jnp.where(tcol < n_valid, jnp.minimum(kb, n_valid), n_valid)
    kext = jnp.minimum(n_valid, N_META + CHUNK * (1 + (t0 + tq - 1 - N_META) // CHUNK))
    n_tiles = (kext + tk - 1) // tk
    kb_first = jnp.where(t0 < N_META, N_META, N_META + CHUNK * (1 + (t0 - N_META) // CHUNK))

    @pl.when((pl.program_id(0) == 0) & (pl.program_id(1) == 0))
    def _():
        tri_ref[...] = jnp.where(lax.broadcasted_iota(jnp.int32, (tk, tk), 0)
                                 <= lax.broadcasted_iota(jnp.int32, (tk, tk), 1), 1.0, 0.0).astype(BF16)

    def scores(t, s_ref):
        r0 = pl.multiple_of(t * tk, tk)
        for n in range(N_KV_HEADS):
            kt = k_ref[pl.ds(r0, tk), n * HEAD_DIM:(n + 1) * HEAD_DIM]
            s_ref[n] = jnp.dot(kt, qt_ref[n], preferred_element_type=F32).astype(BF16)

    for h in range(N_HEADS):
        qh = q_ref[:, h * HEAD_DIM:(h + 1) * HEAD_DIM].astype(F32).T
        qt_ref[h // KV_GROUP, :, (h % KV_GROUP) * tq:(h % KV_GROUP + 1) * tq] = qh.astype(BF16)
    scores(0, sa_ref)
    row = lax.broadcasted_iota(jnp.int32, (LANES, tq), 0)
    for hp in range(IDX_HEADS // 2):
        pair = qi_ref[:, hp * LANES:(hp + 1) * LANES].astype(F32).T
        qit_ref[:, (2 * hp) * tq:(2 * hp + 1) * tq] = jnp.where(row < IDX_DIM, pair, 0.0).astype(BF16)
        qit_ref[:, (2 * hp + 1) * tq:(2 * hp + 2) * tq] = jnp.where(row >= IDX_DIM, pair, 0.0).astype(BF16)
    w = wi_ref[...].T[IDX_DIM:IDX_DIM + IDX_HEADS, :]

    def score_tile(i, carry, masked):
        r0 = pl.multiple_of(i * tk, tk)
        kt = ki_ref[pl.ds(r0, tk), :]
        acc = jnp.zeros((tk, tq), F32)
        for hp in range(IDX_HEADS // 2):
            lg = jnp.dot(kt, qit_ref[:, hp * 2 * tq:(hp + 1) * 2 * tq],
                         preferred_element_type=F32)
            for hh in range(2):
                h = 2 * hp + hh
                acc = acc + jnp.maximum(lg[:, hh * tq:(hh + 1) * tq], 0.0) * w[h:h + 1, :]
        bits = pltpu.bitcast(acc, jnp.int32)
        key = jnp.where(bits >= 0, bits, I32_MIN - bits)
        if masked:
            pos = r0 + lax.broadcasted_iota(jnp.int32, (tk, tq), 0)
            key = jnp.where(pos < kb, key, KEY_MASKED)
        key_ref[pl.ds(r0, tk), :] = key
        hi_ref[pl.ds(r0, tk), :] = lax.shift_right_arithmetic(key, HALF_BITS).astype(jnp.int16)
        raw_ref[pl.ds(r0, tk), :] = ((key & HALF_MASK) + I16_MIN).astype(jnp.int16)
        return carry

    n_open = jnp.minimum(n_valid, kb_first) // tk
    lax.fori_loop(0, n_open, functools.partial(score_tile, masked=False), 0)
    lax.fori_loop(n_open, n_tiles, functools.partial(score_tile, masked=True), 0)

    def tile_count(mask):
        ind = jnp.where(mask, jnp.bfloat16(1), jnp.bfloat16(0))
        parts = [ind[r * PACKED_ROWS:(r + 1) * PACKED_ROWS] for r in range(tk // PACKED_ROWS)]
        while len(parts) > 1:
            nxt = [parts[r] + parts[r + 1] for r in range(0, len(parts) - 1, 2)]
            parts = nxt + ([parts[-1]] if len(parts) % 2 else [])
        return parts[0].astype(F32)

    def search16(ref, kth):
        def count_ge(cand):
            c16 = cand.astype(jnp.int16)

            def body(i, acc):
                r0 = pl.multiple_of(i * tk, tk)
                return acc + tile_count(ref[pl.ds(r0, tk), :] >= c16)

            acc = lax.fori_loop(0, n_tiles, body, jnp.zeros((PACKED_ROWS, tq), F32))
            return acc.sum(axis=0, keepdims=True)

        def step(it, carry):
            lo, c_lo = carry
            cand = lo + jnp.left_shift(jnp.int32(1), HALF_BITS - 1 - it)
            c = count_ge(cand)
            return jnp.where(c >= kth, cand, lo), jnp.where(c >= kth, c, c_lo)

        start = (jnp.full((1, tq), I16_MIN, jnp.int32), jnp.full((1, tq), n_tiles * tk, jnp.int32).astype(F32))
        return lax.fori_loop(0, HALF_BITS, step, start)

    tau_hi, c_high = search16(hi_ref, jnp.full((1, tq), float(topk), F32))

    tau_hi16 = tau_hi.astype(jnp.int16)

    def low_tile(i, acc):
        r0 = pl.multiple_of(i * tk, tk)
        hi = hi_ref[pl.ds(r0, tk), :]
        lo_ref[pl.ds(r0, tk), :] = jnp.where(hi == tau_hi16, raw_ref[pl.ds(r0, tk), :], jnp.int16(I16_MIN))
        return acc + tile_count(hi > tau_hi16)

    c_above = lax.fori_loop(0, n_tiles, low_tile, jnp.zeros((PACKED_ROWS, tq), F32))
    c_above = c_above.sum(axis=0, keepdims=True)
    tau_lo, c_low = search16(lo_ref, topk - c_above)
    tau = tau_hi * (HALF_MASK + 1) + (tau_lo - I16_MIN)
    c_ge = jnp.where(tau_lo == I16_MIN, c_high, c_above + c_low)
    excess = c_ge - topk
    cut = jnp.max(excess) > 0

    @pl.when(jnp.logical_not(cut))
    def _():
        def bias_tile(i, carry, masked):
            r0 = pl.multiple_of(i * tk, tk)
            keep = key_ref[pl.ds(r0, tk), :] >= tau
            if masked:
                keep = keep & (r0 + lax.broadcasted_iota(jnp.int32, (tk, tq), 0) < kb)
            bias_ref[pl.ds(r0, tk), :] = jnp.where(keep, 0.0, NEG_BIG).astype(BF16)
            return carry

        lax.fori_loop(0, n_open, functools.partial(bias_tile, masked=False), 0)
        lax.fori_loop(n_open, n_tiles, functools.partial(bias_tile, masked=True), 0)

    @pl.when(cut)
    def _():
        def tile_bias(t, later):
            r0 = pl.multiple_of(t * tk, tk)
            blk = key_ref[pl.ds(r0, tk), :]
            pos = r0 + lax.broadcasted_iota(jnp.int32, (tk, tq), 0)
            eq = blk == tau
            within = jnp.dot(tri_ref[...], jnp.where(eq, 1.0, 0.0).astype(BF16), preferred_element_type=F32)
            from_here = later + within
            keep = ((blk > tau) | (eq & (from_here > excess))) & (pos < kb)
            return r0, jnp.where(keep, 0.0, NEG_BIG).astype(BF16), from_here[0:1, :]

        def bias_pair(i, later):
            t = n_tiles - 1 - 2 * i
            ra, bias_a, later_a = tile_bias(t, later)
            rb, bias_b, later_b = tile_bias(jnp.maximum(t - 1, 0), later_a)
            bias_ref[pl.ds(rb, tk), :] = bias_b
            bias_ref[pl.ds(ra, tk), :] = bias_a
            return later_b

        lax.fori_loop(0, (n_tiles + 1) // 2, bias_pair, jnp.zeros((1, tq), F32))

    acc_ref[...] = jnp.zeros_like(acc_ref)
    m_ref[...] = jnp.full(m_ref.shape, NEG_BIG, F32)

    def softmax_pv(t, s_ref):
        r0 = pl.multiple_of(t * tk, tk)
        bt = bias_ref[pl.ds(r0, tk), :]
        bt = jnp.concatenate([bt] * KV_GROUP, axis=1)
        for n in range(N_KV_HEADS):
            s = s_ref[n] + bt
            m = m_ref[n]
            m_new = jnp.maximum(m, jnp.max(s, axis=0, keepdims=True).astype(F32))
            alpha = jnp.exp2(m - m_new)
            p = jnp.exp2(s - m_new.astype(BF16))
            m_ref[n] = m_new
            vt = vt_ref[t, n * VT_ROWS:(n + 1) * VT_ROWS, :]
            acc_ref[n] = alpha * acc_ref[n] + jnp.dot(vt, p, preferred_element_type=F32)

    last_tile = k_ref.shape[0] // tk - 1

    def tile_pair(i, carry):
        t = 2 * i
        scores(t + 1, sb_ref)
        softmax_pv(t, sa_ref)
        scores(jnp.minimum(t + 2, last_tile), sa_ref)
        softmax_pv(t + 1, sb_ref)
        return carry

    lax.fori_loop(0, n_tiles // 2, tile_pair, 0)

    @pl.when(n_tiles % 2 == 1)
    def _():
        softmax_pv(n_tiles - 1, sa_ref)

    for h in range(N_HEADS):
        n, g = h // KV_GROUP, h % KV_GROUP
        inv_l = 1.0 / acc_ref[n, HEAD_DIM:HEAD_DIM + 1, g * tq:(g + 1) * tq]
        oh = acc_ref[n, 0:HEAD_DIM, g * tq:(g + 1) * tq] * inv_l
        out_ref[:, h * HEAD_DIM:(h + 1) * HEAD_DIM] = oh.T.astype(BF16)


def _dsa_attention(q, qi, wi, ki, k, vt, n_valid, topk, tk):
    bsz, tp, _ = q.shape
    gw = KV_GROUP * TQ
    qrow = lambda width: pl.BlockSpec((None, TQ, width), lambda b, j: (b, j, 0))
    seq = lambda width: pl.BlockSpec((None, tp, width), lambda b, j: (b, 0, 0))
    return pl.pallas_call(
        functools.partial(_dsa_kernel, n_valid=n_valid, topk=topk, tk=tk),
        grid=(bsz, pl.cdiv(tp, TQ)),
        in_specs=[qrow(N_HEADS * HEAD_DIM), qrow(IDX_HEADS * IDX_DIM), qrow(LANES), seq(LANES),
                  seq(N_KV_HEADS * HEAD_DIM),
                  pl.BlockSpec((None, tp // tk, N_KV_HEADS * VT_ROWS, tk), lambda b, j: (b, 0, 0, 0))],
        out_specs=qrow(N_HEADS * HEAD_DIM),
        out_shape=jax.ShapeDtypeStruct((bsz, tp, N_HEADS * HEAD_DIM), BF16),
        scratch_shapes=[pltpu.VMEM((tp, TQ), jnp.int32), pltpu.VMEM((tp, TQ), jnp.int16),
                        pltpu.VMEM((tp, TQ), jnp.int16), pltpu.VMEM((tp, TQ), jnp.int16),
                        pltpu.VMEM((tp, TQ), BF16),
                        pltpu.VMEM((N_KV_HEADS, HEAD_DIM, gw), BF16),
                        pltpu.VMEM((LANES, IDX_HEADS * TQ), BF16),
                        pltpu.VMEM((N_KV_HEADS, VT_ROWS, gw), F32),
                        pltpu.VMEM((N_KV_HEADS, 1, gw), F32),
                        pltpu.VMEM((N_KV_HEADS, tk, gw), BF16), pltpu.VMEM((N_KV_HEADS, tk, gw), BF16),
                        pltpu.VMEM((tk, tk), BF16)],
        compiler_params=pltpu.CompilerParams(dimension_semantics=("arbitrary", "arbitrary"),
                                             vmem_limit_bytes=VMEM_LIMIT),
        name="dsa_attention",
    )(q, qi, wi, ki, k, vt)


def _rglru_merge_kernel(h_ref, g_ref, w_ref, cw_ref, cb_ref, wa_ref, ba_ref, wx_ref, bx_ref, lam_ref,
                        at_ref, cv_ref, wg_ref, woa_ref, wor_ref, woc_ref, wo_ref,
                        out_ref, xext_ref, a_ref, b_ref, state_ref, part_ref, gr_ref, rnn_ref):
    tm = h_ref.shape[0]
    hist = SUBLANES

    @pl.when(pl.program_id(1) == 0)
    def _():
        xext_ref[0:hist, :] = jnp.zeros((hist, D_MODEL), F32)
        state_ref[...] = jnp.zeros_like(state_ref)

    n = _rms(h_ref[...], g_ref[...]).astype(BF16)
    xext_ref[hist:hist + tm, :] = jnp.dot(n, w_ref[:, 0:D_MODEL], preferred_element_type=F32)
    u = cb_ref[...] + cw_ref[0:1, :] * xext_ref[hist - RNN_CONV + 1:hist - RNN_CONV + 1 + tm, :]
    for kk in range(1, RNN_CONV):
        o = hist - RNN_CONV + 1 + kk
        u = u + cw_ref[kk:kk + 1, :] * xext_ref[o:o + tm, :]
    xext_ref[0:hist, :] = xext_ref[tm:tm + hist, :]

    ub = u.astype(BF16)
    nl = -lam_ref[...]
    softplus = jnp.maximum(nl, 0.0) + jnp.log1p(jnp.exp(-jnp.abs(nl)))
    nchunk = D_MODEL // LANES

    def rnn_gates(blk):
        sl = slice(blk * RNN_BLOCK_DIM, (blk + 1) * RNN_BLOCK_DIM)
        r = jax.nn.sigmoid(jnp.dot(ub[:, sl], wa_ref[blk], preferred_element_type=F32) + ba_ref[:, sl])
        ig = jax.nn.sigmoid(jnp.dot(ub[:, sl], wx_ref[blk], preferred_element_type=F32) + bx_ref[:, sl])
        log_a = -LRU_C * r * softplus[:, sl]
        a_blk = jnp.exp(log_a)
        a_ref[blk] = a_blk
        b_ref[blk] = jnp.sqrt(-_expm1(2.0 * log_a, a_blk * a_blk)) * (ig * u[:, sl])

    half = D_MODEL // 2

    def gate(i, cols):
        wcols = slice(i * D_MODEL + cols.start, i * D_MODEL + cols.stop)
        return jax.nn.sigmoid(jnp.dot(n, wg_ref[:, wcols], preferred_element_type=F32))

    def gated(i, src_ref, wo_ref_, cols):
        return gate(i, cols) * jnp.dot(src_ref[...], wo_ref_[:, cols], preferred_element_type=F32)

    lo_cols, hi_cols = slice(0, half), slice(half, D_MODEL)
    rnn_gates(0)
    attn_lo = gated(0, at_ref, woa_ref, lo_cols)
    rnn_gates(1)
    attn_hi = gated(0, at_ref, woa_ref, hi_cols)
    rnn_gates(2)
    part_ref[:, lo_cols] = attn_lo + gated(2, cv_ref, woc_ref, lo_cols)
    rnn_gates(3)
    part_ref[:, hi_cols] = attn_hi + gated(2, cv_ref, woc_ref, hi_cols)
    rnn_gates(4)
    gr_ref[:, lo_cols] = gate(1, lo_cols)
    rnn_gates(5)
    gr_ref[:, hi_cols] = gate(1, hi_cols)
    rnn_gates(6)
    rnn_gates(7)

    seg = tm // SCAN_SEGMENTS
    groups = SCAN_SEGMENTS // SUBLANES

    def step(i, carry):
        out = []
        for v in range(groups):
            prod, loc = carry[2 * v], carry[2 * v + 1]
            rows = pl.ds(v * SUBLANES * seg + i, SUBLANES, stride=seg)
            a = a_ref[:, rows, :]
            prod = a * prod
            loc = a * loc + b_ref[:, rows, :]
            a_ref[:, rows, :] = prod
            b_ref[:, rows, :] = loc
            out += [prod, loc]
        return tuple(out)

    init = (jnp.ones((nchunk, SUBLANES, LANES), F32), jnp.zeros((nchunk, SUBLANES, LANES), F32)) * groups
    ends = lax.fori_loop(0, seg, step, init)
    gel = jax.nn.gelu(jnp.dot(n, w_ref[:, D_MODEL:], preferred_element_type=F32))
    hcur = state_ref[...]
    for s in range(SCAN_SEGMENTS):
        rs = slice(s * seg, (s + 1) * seg)
        hs = a_ref[:, rs, :] * hcur + b_ref[:, rs, :]
        for c in range(nchunk):
            cs = slice(c * LANES, (c + 1) * LANES)
            rnn_ref[rs, cs] = (hs[c] * gel[rs, cs]).astype(BF16)
        v, r = s // SUBLANES, s % SUBLANES
        hcur = ends[2 * v][:, r:r + 1, :] * hcur + ends[2 * v + 1][:, r:r + 1, :]
    state_ref[...] = hcur

    merged = part_ref[...] + gr_ref[...] * jnp.dot(rnn_ref[...], wor_ref[...], preferred_element_type=F32)
    out_ref[...] = h_ref[...] + jnp.dot(merged.astype(BF16), wo_ref[...], preferred_element_type=F32)


def _rglru_merge(h3, g, w, cw, cb, wa, ba, wx, bx, lam, attn, cnv, wg, woa, wor, woc, wo, tm):
    bsz, tp, _ = h3.shape
    row = pl.BlockSpec((None, tm, D_MODEL), lambda b, t: (b, t, 0))
    vec = _const_spec((1, D_MODEL))
    blkw = _const_spec((RNN_BLOCKS, RNN_BLOCK_DIM, RNN_BLOCK_DIM))
    sq = _const_spec((D_MODEL, D_MODEL))
    return pl.pallas_call(
        _rglru_merge_kernel,
        grid=(bsz, tp // tm),
        in_specs=[row, vec, _const_spec((D_MODEL, 2 * D_MODEL)), _const_spec((RNN_CONV, D_MODEL)), vec,
                  blkw, vec, blkw, vec, vec,
                  row, row, _const_spec((D_MODEL, 3 * D_MODEL)), sq, sq, sq, sq],
        out_specs=row,
        out_shape=jax.ShapeDtypeStruct((bsz, tp, D_MODEL), F32),
        scratch_shapes=[pltpu.VMEM((tm + SUBLANES, D_MODEL), F32),
                        pltpu.VMEM((D_MODEL // LANES, tm, LANES), F32),
                        pltpu.VMEM((D_MODEL // LANES, tm, LANES), F32),
                        pltpu.VMEM((D_MODEL // LANES, 1, LANES), F32),
                        pltpu.VMEM((tm, D_MODEL), F32), pltpu.VMEM((tm, D_MODEL), F32),
                        pltpu.VMEM((tm, D_MODEL), BF16)],
        compiler_params=pltpu.CompilerParams(dimension_semantics=("arbitrary", "arbitrary"),
                                             vmem_limit_bytes=VMEM_LIMIT),
        name="rglru_merge",
    )(h3, g, w, cw, cb, wa, ba, wx, bx, lam, attn, cnv, wg, woa, wor, woc, wo)


SCAN_SEGMENTS = 16
CONV_HIST = 32
CONV_ROWS = 128


def _attn_conf_kernel(h_ref, g_ref, wa_ref, qg_ref, kg_ref, cosa_ref, sina_ref, cosi_ref, sini_ref,
                      wc_ref, dw_ref, db_ref, lg_ref, lb_ref,
                      q_ref, k_ref, vt_ref, qi_ref, ki_ref, wi_ref, cnv_ref, xs_ref, y_ref, *, idx_scale):
    tm = h_ref.shape[0]

    @pl.when(pl.program_id(1) == 0)
    def _():
        xs_ref[0, 0:CONV_HIST, :] = jnp.zeros((CONV_HIST, D_MODEL), F32)

    n = _rms(h_ref[...], g_ref[...]).astype(BF16)
    for piece in _attn_proj_pieces(n, wa_ref, qg_ref, kg_ref, cosa_ref, sina_ref, cosi_ref, sini_ref,
                                   q_ref, k_ref, vt_ref, qi_ref, ki_ref, wi_ref, idx_scale):
        piece()

    a = jnp.dot(n, wc_ref[:, 0:D_MODEL], preferred_element_type=F32)
    gate = jnp.dot(n, wc_ref[:, D_MODEL:], preferred_element_type=F32)
    xs_ref[0, CONV_HIST:CONV_HIST + tm, :] = a * jax.nn.sigmoid(gate)
    span = tm + CONV_HIST - SUBLANES
    for s in range(1, SUBLANES):
        xs_ref[s, 0:span, :] = xs_ref[0, s:s + span, :]

    first = CONV_HIST - CONV_KERNEL + 1
    row_chunks = range(0, tm, CONV_ROWS)

    def conv_chunk(c0):
        cs = slice(c0, c0 + LANES)
        accs = [jnp.broadcast_to(db_ref[:, cs], (CONV_ROWS, LANES)) for _ in row_chunks]
        for kk in range(CONV_KERNEL):
            shift = (first + kk) % SUBLANES
            base = first + kk - shift
            wk = dw_ref[kk:kk + 1, cs]
            for ri, r0 in enumerate(row_chunks):
                accs[ri] = accs[ri] + wk * xs_ref[shift, base + r0:base + r0 + CONV_ROWS, cs]
        for ri, r0 in enumerate(row_chunks):
            y_ref[r0:r0 + CONV_ROWS, cs] = accs[ri]

    for c0 in range(0, D_MODEL, LANES):
        conv_chunk(c0)
    xs_ref[0, 0:CONV_HIST, :] = xs_ref[0, tm:tm + CONV_HIST, :]

    y = y_ref[...]
    mu = jnp.mean(y, axis=-1, keepdims=True)
    yc = y - mu
    z = yc * lax.rsqrt(jnp.mean(yc * yc, axis=-1, keepdims=True) + NORM_EPS) * lg_ref[...] + lb_ref[...]
    cnv_ref[...] = jax.nn.silu(z).astype(BF16)


def _attn_conf(h3, g, wa, qg, kg, tabs, wc, dw, db, lg, lb, tm):
    bsz, tp, _ = h3.shape
    row = lambda width: pl.BlockSpec((None, tm, width), lambda b, t: (b, t, 0))
    tab = pl.BlockSpec((tm, LANES), lambda b, t: (t, 0))
    vec = _const_spec((1, D_MODEL))
    hvec = _const_spec((1, HEAD_DIM))
    idx_scale = (IDX_HEADS ** -0.5) * (IDX_DIM ** -0.5)
    bf = lambda width: jax.ShapeDtypeStruct((bsz, tp, width), BF16)
    return pl.pallas_call(
        functools.partial(_attn_conf_kernel, idx_scale=idx_scale),
        grid=(bsz, tp // tm),
        in_specs=[row(D_MODEL), vec, _const_spec((D_MODEL, wa.shape[1])), hvec, hvec, tab, tab, tab, tab,
                  _const_spec((D_MODEL, 2 * D_MODEL)), _const_spec((CONV_KERNEL, D_MODEL)), vec, vec, vec],
        out_specs=[row(N_HEADS * HEAD_DIM), row(N_KV_HEADS * HEAD_DIM),
                   pl.BlockSpec((None, None, N_KV_HEADS * VT_ROWS, tm), lambda b, t: (b, t, 0, 0)),
                   row(IDX_HEADS * IDX_DIM), row(LANES), row(LANES), row(D_MODEL)],
        out_shape=[bf(N_HEADS * HEAD_DIM), bf(N_KV_HEADS * HEAD_DIM),
                   jax.ShapeDtypeStruct((bsz, tp // tm, N_KV_HEADS * VT_ROWS, tm), BF16),
                   bf(IDX_HEADS * IDX_DIM), bf(LANES), jax.ShapeDtypeStruct((bsz, tp, LANES), F32),
                   bf(D_MODEL)],
        scratch_shapes=[pltpu.VMEM((SUBLANES, tm + CONV_HIST, D_MODEL), F32), pltpu.VMEM((tm, D_MODEL), F32)],
        compiler_params=pltpu.CompilerParams(dimension_semantics=("arbitrary", "arbitrary"),
                                             vmem_limit_bytes=VMEM_LIMIT),
        name="attn_conf",
    )(h3, g, wa, qg, kg, *tabs, wc, dw, db, lg, lb)


def _merge_kernel(h_ref, at_ref, rn_ref, cv_ref, g_ref, wg_ref, wa_ref, wr_ref, wc_ref, wo_ref, out_ref):
    h = h_ref[...]
    n = _rms(h, g_ref[...]).astype(BF16)
    merged = None
    for i, (src, wref) in enumerate(((at_ref, wa_ref), (rn_ref, wr_ref), (cv_ref, wc_ref))):
        gate = jax.nn.sigmoid(jnp.dot(n, wg_ref[:, i * D_MODEL:(i + 1) * D_MODEL],
                                      preferred_element_type=F32))
        term = gate * jnp.dot(src[...], wref[...], preferred_element_type=F32)
        merged = term if merged is None else merged + term
    out_ref[...] = h + jnp.dot(merged.astype(BF16), wo_ref[...], preferred_element_type=F32)


def _merge(h2, attn, rnn, cnv, g, wg, wa, wr, wc, wo, tm):
    rows = h2.shape[0]
    row = pl.BlockSpec((tm, D_MODEL), lambda i: (i, 0))
    sq = _const_spec((D_MODEL, D_MODEL))
    return pl.pallas_call(
        _merge_kernel,
        grid=(rows // tm,),
        in_specs=[row, row, row, row, _const_spec((1, D_MODEL)), _const_spec((D_MODEL, 3 * D_MODEL)),
                  sq, sq, sq, sq],
        out_specs=row,
        out_shape=jax.ShapeDtypeStruct((rows, D_MODEL), F32),
        compiler_params=pltpu.CompilerParams(dimension_semantics=("arbitrary",),
                                             vmem_limit_bytes=VMEM_LIMIT),
        name="merge",
    )(h2, attn, rnn, cnv, g, wg, wa, wr, wc, wo)


def _ffn_kernel(h_ref, g_ref, wg_ref, wu_ref, wd_ref, out_ref):
    h = h_ref[...]
    f = _rms(h, g_ref[...]).astype(BF16)
    gate = jnp.dot(f, wg_ref[...], preferred_element_type=F32)
    up = jnp.dot(f, wu_ref[...], preferred_element_type=F32)
    act = (jax.nn.silu(gate) * up).astype(BF16)
    out_ref[...] = h + jnp.dot(act, wd_ref[...], preferred_element_type=F32)


def _ffn(h2, g, wg, wu, wd, tm):
    rows = h2.shape[0]
    dff = wg.shape[1]
    row = pl.BlockSpec((tm, D_MODEL), lambda i: (i, 0))
    return pl.pallas_call(
        _ffn_kernel,
        grid=(rows // tm,),
        in_specs=[row, _const_spec((1, D_MODEL)), _const_spec((D_MODEL, dff)), _const_spec((D_MODEL, dff)),
                  _const_spec((dff, D_MODEL))],
        out_specs=row,
        out_shape=jax.ShapeDtypeStruct((rows, D_MODEL), F32),
        compiler_params=pltpu.CompilerParams(dimension_semantics=("arbitrary",),
                                             vmem_limit_bytes=VMEM_LIMIT),
        name="ffn",
    )(h2, g, wg, wu, wd)


def _rope_tables(n, dim):
    inv = ROPE_THETA ** (-jnp.arange(0, dim, 2, dtype=F32) / dim)
    ang = jnp.arange(n, dtype=F32)[:, None] * inv[None, :]
    return jnp.cos(ang), jnp.sin(ang)


def kernel(x, meta, mix_norm_g, w_in, q_norm_g, k_norm_g, rnn_conv_w, rnn_conv_b, rnn_wa, rnn_ba, rnn_wx, rnn_bx, rnn_lambda, conv_dw_w, conv_dw_b, conv_ln_g, conv_ln_b, w_o_attn, w_o_rnn, w_o_conv, w_out, ffn_norm_g, w_ffn_gate, w_ffn_up, w_ffn_down):
    bsz, seq, _ = x.shape
    depth = w_in.shape[0]
    t_valid = seq + N_META
    tp = -(-t_valid // LANES) * LANES
    topk = min(TOPK_MAX, seq // 4)
    rows = bsz * tp
    tm_seq = _pick_tile(tp, (384, 256, 128))
    tm_flat = _pick_tile(rows, (512, 384, 256, 128))

    q_w, kv_w, qi_w = N_HEADS * HEAD_DIM, N_KV_HEADS * HEAD_DIM, IDX_HEADS * IDX_DIM
    o = np.cumsum([0, q_w, kv_w, kv_w, qi_w, IDX_DIM, IDX_HEADS, D_MODEL, D_MODEL, 2 * D_MODEL, 3 * D_MODEL])
    pad_w = LANES - IDX_DIM - IDX_HEADS
    w_attn = jnp.concatenate([w_in[:, :, o[0]:o[6]], jnp.zeros((depth, D_MODEL, pad_w), w_in.dtype)],
                             axis=2).astype(BF16)
    w_rnn = w_in[:, :, o[6]:o[8]].astype(BF16)
    w_cnv = w_in[:, :, o[8]:o[9]].astype(BF16)
    w_gate = w_in[:, :, o[9]:o[10]].astype(BF16)
    wa_b, wx_b = rnn_wa.astype(BF16), rnn_wx.astype(BF16)
    w_oa, w_or, w_oc, w_ot = (a.astype(BF16) for a in (w_o_attn, w_o_rnn, w_o_conv, w_out))
    w_fg, w_fu, w_fd = (a.astype(BF16) for a in (w_ffn_gate, w_ffn_up, w_ffn_down))

    cos_a, sin_a = _rope_tables(tp, HEAD_DIM)
    cos_i, sin_i = _rope_tables(tp, IDX_DIM)
    tabs = (jnp.concatenate([cos_a, cos_a], axis=1), jnp.concatenate([-sin_a, sin_a], axis=1),
            jnp.concatenate([cos_i] * 4, axis=1), jnp.concatenate([-sin_i, sin_i] * 2, axis=1))

    h = jnp.concatenate([jnp.broadcast_to(meta[None].astype(x.dtype), (bsz, N_META, D_MODEL)), x,
                         jnp.zeros((bsz, tp - t_valid, D_MODEL), x.dtype)], axis=1)
    vec = lambda a: a.reshape(1, -1)

    for l in range(depth):
        q, k, vt, qi, ki, wi, cnv = _attn_conf(
            h, vec(mix_norm_g[l]), w_attn[l], vec(q_norm_g[l]), vec(k_norm_g[l]), tabs,
            w_cnv[l], conv_dw_w[l], vec(conv_dw_b[l]), vec(conv_ln_g[l]), vec(conv_ln_b[l]), tm_seq)
        attn = _dsa_attention(q, qi, wi, ki, k, vt, t_valid, topk, tm_seq)
        h = _rglru_merge(h, vec(mix_norm_g[l]), w_rnn[l], rnn_conv_w[l], vec(rnn_conv_b[l]), wa_b[l],
                         vec(rnn_ba[l]), wx_b[l], vec(rnn_bx[l]), vec(rnn_lambda[l]), attn, cnv,
                         w_gate[l], w_oa[l], w_or[l], w_oc[l], w_ot[l], tm_seq)
        h2 = _ffn(h.reshape(rows, D_MODEL), vec(ffn_norm_g[l]), w_fg[l], w_fu[l], w_fd[l], tm_flat)
        h = h2.reshape(bsz, tp, D_MODEL)

    return h[:, N_META:t_valid]
```
